```python
import math
import jax, jax.numpy as jnp
from jax import lax
import numpy as np

D_MODEL = 2048
BATCH = 1
SEQ = 16384
DEPTH = 1

ATTN_WIDTH = D_MODEL // 2
SSM_WIDTH = D_MODEL - ATTN_WIDTH
HEAD_DIM = 128
N_HEADS = ATTN_WIDTH // HEAD_DIM
ROT_DIM = HEAD_DIM // 4
ROPE_THETA = 500000.0
DILATED_PATTERNS = ((128, 1), (512, 4), (2048, 16))
ATT_BLOCK = 128
SSM_GROUP = 16
N_SSM_GROUPS = SSM_WIDTH // SSM_GROUP
SSM_STATE = 64
DT_MIN = 1e-3
DT_MAX = 1e-1
D_FF = -(-8 * D_MODEL // (3 * 256)) * 256
IN_WIDTH = 3 * ATTN_WIDTH + SSM_WIDTH
RMS_EPS = 1e-6

kernel_name = "hymba_dilated_attn_s5_hybrid"


def _rmsnorm(t, g):
    tf = t.astype(jnp.float32)
    tf = tf * lax.rsqrt(jnp.mean(tf * tf, axis=-1, keepdims=True) + RMS_EPS)
    return (tf * g.astype(jnp.float32)).astype(t.dtype)


def _rotary_tables(seq):
    pos = jnp.arange(seq, dtype=jnp.float32)
    inv_freq = ROPE_THETA ** (-jnp.arange(0, ROT_DIM, 2, dtype=jnp.float32) / ROT_DIM)
    ang = pos[:, None] * inv_freq[None, :]
    return jnp.cos(ang)[None, :, None, :], jnp.sin(ang)[None, :, None, :]


def _apply_partial_rope(t, cos, sin):
    rot, rest = t[..., :ROT_DIM], t[..., ROT_DIM:]
    x1, x2 = rot[..., :ROT_DIM // 2], rot[..., ROT_DIM // 2:]
    return jnp.concatenate([x1 * cos - x2 * sin, x2 * cos + x1 * sin, rest], axis=-1)


def _dilated_band_attention(q, k, v, dilation, band):
    b, s, h, hd = q.shape
    L = s // dilation
    nb = -(-L // ATT_BLOCK)
    pad = nb * ATT_BLOCK - L

    def to_lattice(t):
        return t.reshape(b, L, dilation, h, hd)

    ql = jnp.pad(to_lattice(q), ((0, 0), (0, pad), (0, 0), (0, 0), (0, 0)))
    kl = jnp.pad(to_lattice(k), ((0, 0), (ATT_BLOCK, pad), (0, 0), (0, 0), (0, 0)))
    vl = jnp.pad(to_lattice(v), ((0, 0), (ATT_BLOCK, pad), (0, 0), (0, 0), (0, 0)))
    qb = ql.reshape(b, nb, ATT_BLOCK, dilation, h, hd)
    kb = kl.reshape(b, nb + 1, ATT_BLOCK, dilation, h, hd)
    vb = vl.reshape(b, nb + 1, ATT_BLOCK, dilation, h, hd)
    kw = jnp.concatenate([kb[:, :-1], kb[:, 1:]], axis=2)
    vw = jnp.concatenate([vb[:, :-1], vb[:, 1:]], axis=2)

    scores = jnp.einsum('bnqrhd,bnkrhd->bnrhqk', qb, kw)
    qi = jnp.arange(ATT_BLOCK)[:, None]
    kj = jnp.arange(2 * ATT_BLOCK)[None, :]
    dist = qi - kj + ATT_BLOCK
    key_idx = jnp.arange(nb)[:, None, None] * ATT_BLOCK + kj[None] - ATT_BLOCK
    valid = (dist >= 0) & (dist <= band) & (key_idx >= 0)
    scores = jnp.where(valid[None, :, None, None], scores, -jnp.inf)
    m = jnp.max(scores, axis=-1)
    p = jnp.exp(scores - m[..., None])
    l = jnp.sum(p, axis=-1)
    o = jnp.einsum('bnrhqk,bnkrhd->bnqrhd', p, vw)

    def from_lattice(t):
        tail = t.shape[5:]
        t = t.reshape((b, nb * ATT_BLOCK, dilation, h) + tail)[:, :L]
        return t.reshape((b, s, h) + tail)

    m = jnp.moveaxis(m, -1, 2)
    l = jnp.moveaxis(l, -1, 2)
    return from_lattice(o), from_lattice(m), from_lattice(l)


def _dilated_attention_mixer(q, k, v, cos, sin):
    b, s, _ = q.shape
    split = lambda t: t.astype(jnp.float32).reshape(b, s, N_HEADS, HEAD_DIM)
    qh = _apply_partial_rope(split(q), cos, sin) * (HEAD_DIM ** -0.5)
    kh = _apply_partial_rope(split(k), cos, sin)
    vh = split(v)
    parts = [_dilated_band_attention(qh, kh, vh, dil, win // dil) for win, dil in DILATED_PATTERNS]
    m_all = jnp.max(jnp.stack([m for _, m, _ in parts], axis=0), axis=0)
    num = 0.0
    den = 0.0
    for o_i, m_i, l_i in parts:
        w_i = jnp.exp(m_i - m_all)
        num = num + w_i[..., None] * o_i
        den = den + w_i * l_i
    out = num / den[..., None]
    return out.reshape(b, s, ATTN_WIDTH).astype(q.dtype)


def _s5_mixer(u, a_re, a_im, log_dt, b_re, b_im, c_re, c_im, d_skip, w_glu, b_glu):
    bsz, s, _ = u.shape
    f32 = jnp.float32
    uf = u.astype(f32).reshape(bsz, s, N_SSM_GROUPS, SSM_GROUP)
    lam = lax.complex(a_re.astype(f32), a_im.astype(f32))
    dt = jnp.exp(log_dt.astype(f32))[:, None]
    a_bar = jnp.exp(lam * dt)
    b_mat = lax.complex(b_re.astype(f32), b_im.astype(f32))
    b_bar = ((a_bar - 1.0) / lam)[..., None] * b_mat
    bu = jnp.einsum('bsgp,gnp->bsgn', uf.astype(jnp.complex64), b_bar)
    a_seq = jnp.broadcast_to(a_bar, bu.shape)

    def combine(e1, e2):
        a1, x1 = e1
        a2, x2 = e2
        return a1 * a2, a2 * x1 + x2

    _, states = lax.associative_scan(combine, (a_seq, bu), axis=1)
    c_mat = lax.complex(c_re.astype(f32), c_im.astype(f32))
    y = jnp.einsum('bsgn,gpn->bsgp', states, c_mat).real + d_skip.astype(f32) * uf
    y = jax.nn.gelu(y.reshape(bsz, s, SSM_WIDTH))
    gate = jax.nn.sigmoid(y @ w_glu.astype(f32) + b_glu.astype(f32))
    return (y * gate).astype(u.dtype)


def setup_inputs(seed: int = 0) -> dict:
    key = jax.random.key(seed)
    ks = jax.random.split(key, 20)
    f32 = jnp.float32
    nrm = lambda k, shape, scale: jax.random.normal(k, shape, f32) * scale
    x = jax.random.normal(ks[0], (BATCH, SEQ, D_MODEL), f32)
    norm1_g = 1.0 + nrm(ks[1], (DEPTH, D_MODEL), 0.02)
    w_in = nrm(ks[2], (DEPTH, D_MODEL, IN_WIDTH), D_MODEL ** -0.5)
    a_re = -0.5 + nrm(ks[3], (DEPTH, N_SSM_GROUPS, SSM_STATE), 0.01)
    a_im = (math.pi * jnp.arange(SSM_STATE, dtype=f32))[None, None, :] + nrm(ks[4], (DEPTH, N_SSM_GROUPS, SSM_STATE), 0.01)
    log_dt = jax.random.uniform(ks[5], (DEPTH, N_SSM_GROUPS), f32, math.log(DT_MIN), math.log(DT_MAX))
    b_re = nrm(ks[6], (DEPTH, N_SSM_GROUPS, SSM_STATE, SSM_GROUP), (2 * SSM_GROUP) ** -0.5)
    b_im = nrm(ks[7], (DEPTH, N_SSM_GROUPS, SSM_STATE, SSM_GROUP), (2 * SSM_GROUP) ** -0.5)
    c_re = nrm(ks[8], (DEPTH, N_SSM_GROUPS, SSM_GROUP, SSM_STATE), (2 * SSM_STATE) ** -0.5)
    c_im = nrm(ks[9], (DEPTH, N_SSM_GROUPS, SSM_GROUP, SSM_STATE), (2 * SSM_STATE) ** -0.5)
    d_skip = nrm(ks[10], (DEPTH, N_SSM_GROUPS, SSM_GROUP), 1.0)
    w_glu = nrm(ks[11], (DEPTH, SSM_WIDTH, SSM_WIDTH), SSM_WIDTH ** -0.5)
    b_glu = nrm(ks[12], (DEPTH, SSM_WIDTH), 0.02)
    w_out = nrm(ks[13], (DEPTH, ATTN_WIDTH + SSM_WIDTH, D_MODEL), (ATTN_WIDTH + SSM_WIDTH) ** -0.5)
    norm2_g = 1.0 + nrm(ks[14], (DEPTH, D_MODEL), 0.02)
    w_gate = nrm(ks[15], (DEPTH, D_MODEL, D_FF), D_MODEL ** -0.5)
    w_up = nrm(ks[16], (DEPTH, D_MODEL, D_FF), D_MODEL ** -0.5)
    w_down = nrm(ks[17], (DEPTH, D_FF, D_MODEL), D_FF ** -0.5)
    final_g = 1.0 + nrm(ks[18], (D_MODEL,), 0.02)
    return {"x": x, "norm1_g": norm1_g, "w_in": w_in, "a_re": a_re, "a_im": a_im,
            "log_dt": log_dt, "b_re": b_re, "b_im": b_im, "c_re": c_re, "c_im": c_im,
            "d_skip": d_skip, "w_glu": w_glu, "b_glu": b_glu, "w_out": w_out,
            "norm2_g": norm2_g, "w_gate": w_gate, "w_up": w_up, "w_down": w_down,
            "final_g": final_g}


def reference(x, norm1_g, w_in, a_re, a_im, log_dt, b_re, b_im, c_re, c_im, d_skip,
              w_glu, b_glu, w_out, norm2_g, w_gate, w_up, w_down, final_g):
    _, s, _ = x.shape
    cos, sin = _rotary_tables(s)
    h = x
    for layer in range(DEPTH):
        hn = _rmsnorm(h, norm1_g[layer])
        proj = hn @ w_in[layer]
        q, k, v, u = jnp.split(proj, [ATTN_WIDTH, 2 * ATTN_WIDTH, 3 * ATTN_WIDTH], axis=-1)
        attn_out = _dilated_attention_mixer(q, k, v, cos, sin)
        ssm_out = _s5_mixer(u, a_re[layer], a_im[layer], log_dt[layer], b_re[layer],
                            b_im[layer], c_re[layer], c_im[layer], d_skip[layer],
                            w_glu[layer], b_glu[layer])
        h = h + jnp.concatenate([attn_out, ssm_out], axis=-1) @ w_out[layer]
        hn = _rmsnorm(h, norm2_g[layer])
        h = h + (jax.nn.silu(hn @ w_gate[layer]) * (hn @ w_up[layer])) @ w_down[layer]
    return _rmsnorm(h, final_g)
```

```python
import functools

import numpy as np
import jax
import jax.numpy as jnp
from jax import lax
from jax.experimental import pallas as pl
from jax.experimental.pallas import tpu as pltpu

F32 = jnp.float32
BF16 = jnp.bfloat16

D_MODEL = 2048
ATTN_WIDTH = 1024
SSM_WIDTH = 1024
HEAD_DIM = 128
N_HEADS = ATTN_WIDTH // HEAD_DIM
ROT_DIM = HEAD_DIM // 4
ROPE_THETA = 500000.0
BAND = 128
MAX_DIL = 16
SUPER = BAND * MAX_DIL
SSM_GROUP = 16
N_GROUPS = SSM_WIDTH // SSM_GROUP
SSM_STATE = 64
CHUNK = 16
D_FF = 5632
IN_WIDTH = 3 * ATTN_WIDTH + SSM_WIDTH
RMS_EPS = 1e-6
LANES = 128
SUBLANES = 8

VMEM_LIMIT = 56 * 1024 * 1024


def _cparams(sem):
    return pltpu.CompilerParams(dimension_semantics=sem, vmem_limit_bytes=VMEM_LIMIT)


TM_IN = 512
TN_IN = 512
N_ROPE_BLK = 2 * ATTN_WIDTH // TN_IN
N_Q_BLK = ATTN_WIDTH // TN_IN
N_QKV_BLK = 3 * ATTN_WIDTH // TN_IN
HEADS_PER_BLK = TN_IN // HEAD_DIM


def _inproj_kernel(x_ref, g_ref, w_ref, rope_ref, qkv_ref, u_ref, hn_ref):
    j = pl.program_id(1)

    @pl.when(j == 0)
    def _():
        x = x_ref[...]
        ms = jnp.mean(x * x, axis=-1, keepdims=True)
        hn_ref[...] = (x * lax.rsqrt(ms + RMS_EPS) * g_ref[...]).astype(BF16)

    acc = jnp.dot(hn_ref[...], w_ref[...], preferred_element_type=F32)

    @pl.when(j < N_ROPE_BLK)
    def _():
        cos = rope_ref[0]
        sin_hi = rope_ref[1]
        sin_lo = rope_ref[2]
        scale = jnp.where(j < N_Q_BLK, HEAD_DIM ** -0.5, 1.0).astype(F32)
        for hh in range(HEADS_PER_BLK):
            t = acc[:, hh * HEAD_DIM:(hh + 1) * HEAD_DIM]
            r = (t * cos + pltpu.roll(t, ROT_DIM // 2, 1) * sin_hi
                 + pltpu.roll(t, HEAD_DIM - ROT_DIM // 2, 1) * sin_lo)
            qkv_ref[hh] = (r * scale).astype(BF16)

    @pl.when((j >= N_ROPE_BLK) & (j < N_QKV_BLK))
    def _():
        for hh in range(HEADS_PER_BLK):
            qkv_ref[hh] = acc[:, hh * HEAD_DIM:(hh + 1) * HEAD_DIM].astype(BF16)

    @pl.when(j >= N_QKV_BLK)
    def _():
        u_ref[...] = acc.astype(BF16)


def _rope_tables(s):
    pos = jnp.arange(s, dtype=F32)
    inv_freq = ROPE_THETA ** (-jnp.arange(0, ROT_DIM, 2, dtype=F32) / ROT_DIM)
    ang = pos[:, None] * inv_freq[None, :]
    cos, sin = jnp.cos(ang), jnp.sin(ang)
    half = ROT_DIM // 2
    rest = HEAD_DIM - ROT_DIM
    c = jnp.concatenate([cos, cos, jnp.ones((s, rest), F32)], axis=1)
    s_hi = jnp.concatenate([jnp.zeros((s, half), F32), sin, jnp.zeros((s, rest), F32)], axis=1)
    s_lo = jnp.concatenate([-sin, jnp.zeros((s, half + rest), F32)], axis=1)
    return jnp.stack([c, s_hi, s_lo])


def _inproj(x, g, w_bf16, rope):
    s = x.shape[0]
    grid = (s // TM_IN, IN_WIDTH // TN_IN)
    return pl.pallas_call(
        _inproj_kernel,
        grid=grid,
        in_specs=[
            pl.BlockSpec((TM_IN, D_MODEL), lambda i, j: (i, 0)),
            pl.BlockSpec((1, D_MODEL), lambda i, j: (0, 0)),
            pl.BlockSpec((D_MODEL, TN_IN), lambda i, j: (0, j)),
            pl.BlockSpec((3, TM_IN, HEAD_DIM), lambda i, j: (0, i, 0)),
        ],
        out_specs=[
            pl.BlockSpec((HEADS_PER_BLK, TM_IN, HEAD_DIM),
                         lambda i, j: (jnp.minimum(j, N_QKV_BLK - 1), i, 0)),
            pl.BlockSpec((TM_IN, TN_IN), lambda i, j: (i, jnp.maximum(j - N_QKV_BLK, 0))),
        ],
        out_shape=[
            jax.ShapeDtypeStruct((3 * N_HEADS, s, HEAD_DIM), BF16),
            jax.ShapeDtypeStruct((s, SSM_WIDTH), BF16),
        ],
        scratch_shapes=[pltpu.VMEM((TM_IN, D_MODEL), BF16)],
        compiler_params=_cparams(("parallel", "arbitrary")),
        name="inproj",
    )(x, g, w_bf16, rope)


def _band_bias(tile, perm_mod, perm_mul):
    rho = np.arange(tile)
    lat = perm_mul * (rho % perm_mod) + rho // perm_mod
    jq = lat[:, None]
    jk = np.concatenate([lat - tile, lat])[None, :]
    dist = jq - jk
    valid = (dist >= 0) & (dist <= BAND)
    normal = np.where(valid, 0.0, -np.inf).astype(np.float32)
    first = np.where(valid & (jk >= 0), 0.0, -np.inf).astype(np.float32)
    return np.stack([normal, first])


def _attn_tile(q, k, v, bias, old):
    n = k.shape[0]
    s = lax.dot_general(q, k, (((1,), (1,)), ((), ())), preferred_element_type=F32) + bias
    mt = jnp.max(s, axis=-1, keepdims=True)
    v1 = jnp.concatenate([v, jnp.ones((n, LANES), BF16)], axis=1)
    if old is None:
        m_new = jnp.broadcast_to(mt, (q.shape[0], LANES))
    else:
        acc_o, m_o, l_o = old
        m_new = jnp.maximum(m_o, mt)
    p = jnp.exp(s - jnp.concatenate([m_new] * (n // LANES), axis=1))
    pv = jnp.dot(p.astype(BF16), v1, preferred_element_type=F32)
    o, l = pv[:, :HEAD_DIM], pv[:, HEAD_DIM:]
    if old is not None:
        alpha = jnp.exp(m_o - m_new)
        o = alpha * acc_o + o
        l = alpha * l_o + l
    return o, m_new, l


def _attn_kernel(q_ref, kp_ref, kc_ref, vp_ref, vc_ref, b16_ref, b4_ref, b1_ref,
                 o_ref, acc_ref, m_ref, l_ref):
    first = jnp.where(pl.program_id(1) == 0, 1, 0)

    def lane(r):
        return slice(r * LANES, (r + 1) * LANES)

    bias = b16_ref[first]
    for r in range(MAX_DIL):
        k = jnp.concatenate([kp_ref[:, lane(r)], kc_ref[:, lane(r)]], axis=0)
        v = jnp.concatenate([vp_ref[:, lane(r)], vc_ref[:, lane(r)]], axis=0)
        o, m, l = _attn_tile(q_ref[:, lane(r)], k, v, bias, None)
        acc_ref[:, lane(r)] = o
        m_ref[:, lane(r)] = m
        l_ref[:, lane(r)] = l

    def run_pattern(dil, rows, final):
        n_c = MAX_DIL // dil
        n_b = BAND // rows
        b_ref = b4_ref if dil == 4 else b1_ref
        for b in range(n_b):
            bias = b_ref[first] if b == 0 else b_ref[0]
            cur = slice(b * rows, (b + 1) * rows)
            prev = slice((b - 1) * rows, b * rows) if b > 0 else slice(BAND - rows, BAND)
            for r in range(dil):
                blocks = [lane(r + dil * c) for c in range(n_c)]

                def gather(ref, rsl):
                    return jnp.concatenate([ref[rsl, bl] for bl in blocks], axis=0)

                q = gather(q_ref, cur)
                k = jnp.concatenate(
                    [gather(kc_ref if b > 0 else kp_ref, prev), gather(kc_ref, cur)], axis=0)
                v = jnp.concatenate(
                    [gather(vc_ref if b > 0 else vp_ref, prev), gather(vc_ref, cur)], axis=0)
                old = (gather(acc_ref, cur), gather(m_ref, cur), gather(l_ref, cur))
                o, m, l = _attn_tile(q, k, v, bias, old)
                for c, bl in enumerate(blocks):
                    piece = slice(c * rows, (c + 1) * rows)
                    if final:
                        o_ref[cur, bl] = (o[piece] / l[piece]).astype(o_ref.dtype)
                    else:
                        acc_ref[cur, bl] = o[piece]
                        m_ref[cur, bl] = m[piece]
                        l_ref[cur, bl] = l[piece]

    run_pattern(4, 32, False)
    run_pattern(1, 16, True)


def _attention(qkv):
    s = qkv.shape[1]
    rows = s // MAX_DIL
    width = MAX_DIL * HEAD_DIM
    lat = qkv.reshape(3 * N_HEADS, rows, width)
    b16 = jnp.asarray(_band_bias(BAND, BAND, 1))
    b4 = jnp.asarray(_band_bias(BAND, BAND // 4, 4))
    b1 = jnp.asarray(_band_bias(2 * BAND, MAX_DIL, MAX_DIL))
    blk = (None, BAND, width)

    def spec(base, prev):
        if prev:
            return pl.BlockSpec(blk, lambda h, i: (base + h, jnp.maximum(i - 1, 0), 0))
        return pl.BlockSpec(blk, lambda h, i: (base + h, i, 0))

    def full(a):
        return pl.BlockSpec(a.shape, lambda h, i: (0, 0, 0))

    out = pl.pallas_call(
        _attn_kernel,
        grid=(N_HEADS, s // SUPER),
        in_specs=[spec(0, False), spec(N_HEADS, True), spec(N_HEADS, False),
                  spec(2 * N_HEADS, True), spec(2 * N_HEADS, False),
                  full(b16), full(b4), full(b1)],
        out_specs=pl.BlockSpec(blk, lambda h, i: (h, i, 0)),
        out_shape=jax.ShapeDtypeStruct((N_HEADS, rows, width), BF16),
        scratch_shapes=[pltpu.VMEM((BAND, width), F32)] * 3,
        compiler_params=_cparams(("parallel", "parallel")),
        name="dilated_attn",
    )(lat, lat, lat, lat, lat, b16, b4, b1)
    return out.reshape(N_HEADS, s, HEAD_DIM)


PAIR = 2
STATE_LANES = PAIR * SSM_STATE
FOLD = CHUNK * SSM_GROUP


def _shift_down(x, k, row):
    return jnp.where(row >= k, pltpu.roll(x, k, 0), 0.0)


def _s5_kernel(u_ref, we_ref, wy_ref, sc_ref, y_ref, e_ref, h_ref):
    n_rows = u_ref.shape[1]
    u0, u1 = u_ref[0], u_ref[1]
    e_ref[...] = jnp.dot(jnp.concatenate([u0, u1], axis=1), we_ref[...],
                         preferred_element_type=F32)

    row = lax.broadcasted_iota(jnp.int32, (SUBLANES, STATE_LANES), 0)
    pr, pi = sc_ref[0:8, :], sc_ref[8:16, :]
    steps = [(sc_ref[16 + 2 * i:17 + 2 * i, :], sc_ref[17 + 2 * i:18 + 2 * i, :], 1 << i)
             for i in range(3)]

    def block(b, carry):
        cr, ci = carry
        r0 = pl.multiple_of(b * SUBLANES, SUBLANES)
        xr = e_ref[pl.ds(r0, SUBLANES), 0:STATE_LANES]
        xi = e_ref[pl.ds(r0, SUBLANES), STATE_LANES:2 * STATE_LANES]
        for kr, ki, sh in steps:
            sr, si = _shift_down(xr, sh, row), _shift_down(xi, sh, row)
            xr, xi = xr + (kr * sr - ki * si), xi + (kr * si + ki * sr)
        hr = xr + (pr * cr - pi * ci)
        hi = xi + (pr * ci + pi * cr)
        h_ref[pl.ds(r0, SUBLANES), 0:STATE_LANES] = jnp.where(
            row >= 1, pltpu.roll(hr, 1, 0), cr)
        h_ref[pl.ds(r0, SUBLANES), STATE_LANES:2 * STATE_LANES] = jnp.where(
            row >= 1, pltpu.roll(hi, 1, 0), ci)
        return hr[SUBLANES - 1:SUBLANES, :], hi[SUBLANES - 1:SUBLANES, :]

    zero = jnp.zeros((1, STATE_LANES), F32)
    lax.fori_loop(0, n_rows // SUBLANES, block, (zero, zero))

    hin = h_ref[...].astype(BF16)
    for g, ug in enumerate((u0, u1)):
        y = jnp.dot(jnp.concatenate([ug, hin], axis=1), wy_ref[g], preferred_element_type=F32)
        y_ref[g] = y.astype(y_ref.dtype)


def _s5_weights(a_re, a_im, log_dt, b_re, b_im, c_re, c_im, d_skip):
    g, n, p = N_GROUPS, SSM_STATE, SSM_GROUP
    lam = lax.complex(a_re.astype(F32), a_im.astype(F32))
    dt = jnp.exp(log_dt.astype(F32))[:, None]
    a_bar = jnp.exp(lam * dt)
    b_bar = ((a_bar - 1.0) / lam)[..., None] * lax.complex(b_re.astype(F32), b_im.astype(F32))
    c_mat = lax.complex(c_re.astype(F32), c_im.astype(F32))

    def apow(k):
        return jnp.exp(lam * dt * k)

    pw = jnp.stack([apow(float(k)) for k in range(CHUNK + 1)])

    kern = jnp.einsum('gpn,tgn,gnq->tgpq', c_mat, pw[:CHUNK], b_bar).real
    kern = kern.at[0].add(jax.vmap(jnp.diag)(d_skip.astype(F32)))
    t_in = np.arange(CHUNK)[:, None]
    t_out = np.arange(CHUNK)[None, :]
    lag = t_out - t_in
    toe = kern[np.clip(lag, 0, CHUNK - 1)]
    toe = jnp.where((lag >= 0)[:, :, None, None, None], toe, 0.0)
    toe = toe.transpose(2, 0, 4, 1, 3).reshape(g, FOLD, FOLD)

    wb = jnp.einsum('tgn,gnq->gtqn', pw[CHUNK - 1::-1][:CHUNK], b_bar).reshape(g, FOLD, n)
    ca = jnp.einsum('gpn,tgn->gntp', c_mat, pw[1:]).reshape(g, n, FOLD)

    gp = g // PAIR
    zeros_e = jnp.zeros((gp, FOLD, n), F32)
    wbp = wb.reshape(gp, PAIR, FOLD, n)
    we = jnp.concatenate([
        jnp.concatenate([wbp[:, 0].real, zeros_e, wbp[:, 0].imag, zeros_e], axis=2),
        jnp.concatenate([zeros_e, wbp[:, 1].real, zeros_e, wbp[:, 1].imag], axis=2)], axis=1)
    cap = ca.reshape(gp, PAIR, n, FOLD)
    zeros_c = jnp.zeros((gp, n, FOLD), F32)
    wy0 = jnp.concatenate([toe.reshape(gp, PAIR, FOLD, FOLD)[:, 0],
                           cap[:, 0].real, zeros_c, -cap[:, 0].imag, zeros_c], axis=1)
    wy1 = jnp.concatenate([toe.reshape(gp, PAIR, FOLD, FOLD)[:, 1],
                           zeros_c, cap[:, 1].real, zeros_c, -cap[:, 1].imag], axis=1)
    wy = jnp.stack([wy0, wy1], axis=1)

    def pair_lanes(z):
        k = z.shape[0]
        return z.reshape(k, gp, PAIR * n).transpose(1, 0, 2)

    carry = jnp.stack([apow(float(CHUNK * (r + 1))) for r in range(SUBLANES)])
    step = jnp.stack([apow(float(CHUNK * (1 << i))) for i in range(3)])
    step_rows = jnp.stack([step.real, step.imag], axis=1).reshape(6, g, n)
    sc = jnp.concatenate([pair_lanes(carry.real), pair_lanes(carry.imag),
                          pair_lanes(step_rows), jnp.zeros((gp, 2, STATE_LANES), F32)], axis=1)
    return we.astype(BF16), wy.astype(BF16), sc


def _s5_core(u2, we, wy, sc):
    g, rows, _ = u2.shape
    return pl.pallas_call(
        _s5_kernel,
        grid=(g // PAIR,),
        in_specs=[
            pl.BlockSpec((PAIR, rows, FOLD), lambda k: (k, 0, 0)),
            pl.BlockSpec((None, PAIR * FOLD, 2 * STATE_LANES), lambda k: (k, 0, 0)),
            pl.BlockSpec((None, PAIR, FOLD + 2 * STATE_LANES, FOLD), lambda k: (k, 0, 0, 0)),
            pl.BlockSpec((None, 3 * SUBLANES, STATE_LANES), lambda k: (k, 0, 0)),
        ],
        out_specs=pl.BlockSpec((PAIR, rows, FOLD), lambda k: (k, 0, 0)),
        out_shape=jax.ShapeDtypeStruct((g, rows, FOLD), BF16),
        scratch_shapes=[pltpu.VMEM((rows, 2 * STATE_LANES), F32)] * 2,
        compiler_params=_cparams(("parallel",)),
        name="s5_scan",
    )(u2, we, wy, sc)


TM_OUT = 256


def _outproj_kernel(x_ref, a_ref, y_ref, wglu_ref, bglu_ref, wout_ref, g_ref, h_ref, hn_ref):
    y = jax.nn.gelu(y_ref[...].astype(F32))
    gate = jax.nn.sigmoid(
        jnp.dot(y.astype(BF16), wglu_ref[...], preferred_element_type=F32) + bglu_ref[...])
    ssm = (y * gate).astype(BF16)
    attn = jnp.concatenate([a_ref[hh] for hh in range(N_HEADS)], axis=1)
    mix = jnp.concatenate([attn, ssm], axis=1)
    h = x_ref[...] + jnp.dot(mix, wout_ref[...], preferred_element_type=F32)
    h_ref[...] = h
    ms = jnp.mean(h * h, axis=-1, keepdims=True)
    hn_ref[...] = (h * lax.rsqrt(ms + RMS_EPS) * g_ref[...]).astype(BF16)


def _outproj(x, attn, y, w_glu, b_glu, w_out, g2):
    s = x.shape[0]
    const = lambda i: (0, 0)
    return pl.pallas_call(
        _outproj_kernel,
        grid=(s // TM_OUT,),
        in_specs=[
            pl.BlockSpec((TM_OUT, D_MODEL), lambda i: (i, 0)),
            pl.BlockSpec((N_HEADS, TM_OUT, HEAD_DIM), lambda i: (0, i, 0)),
            pl.BlockSpec((TM_OUT, SSM_WIDTH), lambda i: (i, 0)),
            pl.BlockSpec((SSM_WIDTH, SSM_WIDTH), const),
            pl.BlockSpec((1, SSM_WIDTH), const),
            pl.BlockSpec((D_MODEL, D_MODEL), const),
            pl.BlockSpec((1, D_MODEL), const),
        ],
        out_specs=[pl.BlockSpec((TM_OUT, D_MODEL), lambda i: (i, 0))] * 2,
        out_shape=[jax.ShapeDtypeStruct((s, D_MODEL), F32),
                   jax.ShapeDtypeStruct((s, D_MODEL), BF16)],
        compiler_params=_cparams(("parallel",)),
        name="outproj",
    )(x, attn, y, w_glu, b_glu, w_out, g2)


TM_FFN = 512
TF_FFN = 512


def _ffn_kernel(hn_ref, h_ref, wg_ref, wu_ref, wd_ref, g_ref, o_ref, acc_ref):
    f = pl.program_id(1)
    hn = hn_ref[...]
    gate = jnp.dot(hn, wg_ref[...], preferred_element_type=F32)
    up = jnp.dot(hn, wu_ref[...], preferred_element_type=F32)
    act = (jax.nn.silu(gate) * up).astype(BF16)
    part = jnp.dot(act, wd_ref[...], preferred_element_type=F32)

    @pl.when(f == 0)
    def _():
        acc_ref[...] = part

    @pl.when(f > 0)
    def _():
        acc_ref[...] += part

    @pl.when(f == pl.num_programs(1) - 1)
    def _():
        h = h_ref[...] + acc_ref[...]
        ms = jnp.mean(h * h, axis=-1, keepdims=True)
        o_ref[...] = h * lax.rsqrt(ms + RMS_EPS) * g_ref[...]


def _ffn(hn, h, w_gate, w_up, w_down, g):
    s = h.shape[0]
    return pl.pallas_call(
        _ffn_kernel,
        grid=(s // TM_FFN, D_FF // TF_FFN),
        in_specs=[
            pl.BlockSpec((TM_FFN, D_MODEL), lambda i, f: (i, 0)),
            pl.BlockSpec((TM_FFN, D_MODEL), lambda i, f: (i, 0)),
            pl.BlockSpec((D_MODEL, TF_FFN), lambda i, f: (0, f)),
            pl.BlockSpec((D_MODEL, TF_FFN), lambda i, f: (0, f)),
            pl.BlockSpec((TF_FFN, D_MODEL), lambda i, f: (f, 0)),
            pl.BlockSpec((1, D_MODEL), lambda i, f: (0, 0)),
        ],
        out_specs=pl.BlockSpec((TM_FFN, D_MODEL), lambda i, f: (i, 0)),
        out_shape=jax.ShapeDtypeStruct((s, D_MODEL), F32),
        scratch_shapes=[pltpu.VMEM((TM_FFN, D_MODEL), F32)],
        compiler_params=_cparams(("parallel", "arbitrary")),
        name="ffn",
    )(hn, h, w_gate, w_up, w_down, g)


def _fold_time(u):
    s = u.shape[0]
    return (u.reshape(s // CHUNK, CHUNK, N_GROUPS, SSM_GROUP)
            .transpose(2, 0, 1, 3).reshape(N_GROUPS, s // CHUNK, FOLD))


def _unfold_time(y2):
    rows = y2.shape[1]
    return (y2.reshape(N_GROUPS, rows, CHUNK, SSM_GROUP)
            .transpose(1, 2, 0, 3).reshape(rows * CHUNK, SSM_WIDTH))


def kernel(x, norm1_g, w_in, a_re, a_im, log_dt, b_re, b_im, c_re, c_im, d_skip, w_glu, b_glu,
           w_out, norm2_g, w_gate, w_up, w_down, final_g):
    b, s, _ = x.shape
    assert b == 1 and s % SUPER == 0 and w_in.shape[0] == 1
    rope = _rope_tables(s)
    x2 = x[0]
    qkv, u = _inproj(x2, norm1_g[0][None, :], w_in[0].astype(BF16), rope)
    attn = _attention(qkv)
    we, wy, sc = _s5_weights(a_re[0], a_im[0], log_dt[0], b_re[0], b_im[0], c_re[0], c_im[0],
                             d_skip[0])
    y = _unfold_time(_s5_core(_fold_time(u), we, wy, sc))
    h, hn = _outproj(x2, attn, y, w_glu[0].astype(BF16), b_glu[0][None, :].astype(F32),
                     w_out[0].astype(BF16), norm2_g[0][None, :])
    out = _ffn(hn, h, w_gate[0].astype(BF16), w_up[0].astype(BF16), w_down[0].astype(BF16),
               final_g[None, :])
    return out[None]
```

```python
import numpy as np
import jax
import jax.numpy as jnp
from jax import lax
from jax.experimental import pallas as pl
from jax.experimental.pallas import tpu as pltpu

F32 = jnp.float32
BF16 = jnp.bfloat16
HI = lax.Precision.HIGHEST

D_MODEL = 2048
ATTN_WIDTH = 1024
SSM_WIDTH = 1024
HEAD_DIM = 128
N_HEADS = ATTN_WIDTH // HEAD_DIM
ROT_DIM = HEAD_DIM // 4
ROPE_THETA = 500000.0
BAND = 128
MAX_DIL = 16
SUPER = BAND * MAX_DIL
SSM_GROUP = 16
N_GROUPS = SSM_WIDTH // SSM_GROUP
SSM_STATE = 64
CHUNK = 16
D_FF = 5632
IN_WIDTH = 3 * ATTN_WIDTH + SSM_WIDTH
RMS_EPS = 1e-6
LANES = 128
SUBLANES = 8

TM = 512
TM_CHUNKS = TM // CHUNK

VMEM_LIMIT = 58 * 1024 * 1024


def _cparams(sem):
    return pltpu.CompilerParams(dimension_semantics=sem, vmem_limit_bytes=VMEM_LIMIT)


def _resident(shape):
    zeros = (0,) * len(shape)
    return pl.BlockSpec(shape, lambda *_: zeros, pipeline_mode=pl.Buffered(1))


def _rows(t):
    return slice(t * TM_CHUNKS, (t + 1) * TM_CHUNKS)


def _lane(r):
    return slice(r * LANES, (r + 1) * LANES)


TN_IN = 512
HEADS_PER_BLK = TN_IN // HEAD_DIM


def _inproj_kernel(x_ref, g_ref, w_ref, rope_ref, perm_ref, qkv_ref, u_ref, hn_ref):
    x = x_ref[...]
    ms = jnp.mean(x * x, axis=-1, keepdims=True)
    hn = (x * lax.rsqrt(ms + RMS_EPS) * g_ref[...]).astype(BF16)
    hn_ref[...] = jnp.dot(perm_ref[...], hn, preferred_element_type=F32).astype(BF16)

    cos, sin_hi, sin_lo = rope_ref[0], rope_ref[1], rope_ref[2]
    for j in range(IN_WIDTH // TN_IN):
        acc = jnp.dot(hn_ref[...], w_ref[:, j * TN_IN:(j + 1) * TN_IN],
                      preferred_element_type=F32)
        col = j * TN_IN
        if col >= 3 * ATTN_WIDTH:
            for t in range(CHUNK):
                u_ref[t, :, col - 3 * ATTN_WIDTH:col - 3 * ATTN_WIDTH + TN_IN] = (
                    acc[_rows(t), :].astype(BF16))
            continue
        for hh in range(HEADS_PER_BLK):
            r = acc[:, hh * HEAD_DIM:(hh + 1) * HEAD_DIM]
            if col < 2 * ATTN_WIDTH:
                r = (r * cos + pltpu.roll(r, ROT_DIM // 2, 1) * sin_hi
                     + pltpu.roll(r, HEAD_DIM - ROT_DIM // 2, 1) * sin_lo)
            if col < ATTN_WIDTH:
                r = r * (HEAD_DIM ** -0.5)
            r = r.astype(BF16)
            head = j * HEADS_PER_BLK + hh
            for t in range(CHUNK):
                qkv_ref[head, :, _lane(t)] = r[_rows(t), :]


def _rope_tables(s):
    tile = np.arange(s // TM)[:, None, None] * TM
    pos = (tile + np.arange(CHUNK)[None, :, None] + CHUNK * np.arange(TM_CHUNKS)[None, None, :])
    pos = jnp.asarray(pos.reshape(-1), F32)
    inv_freq = ROPE_THETA ** (-jnp.arange(0, ROT_DIM, 2, dtype=F32) / ROT_DIM)
    ang = pos[:, None] * inv_freq[None, :]
    cos, sin = jnp.cos(ang), jnp.sin(ang)
    half = ROT_DIM // 2
    rest = HEAD_DIM - ROT_DIM
    c = jnp.concatenate([cos, cos, jnp.ones((s, rest), F32)], axis=1)
    s_hi = jnp.concatenate([jnp.zeros((s, half), F32), sin, jnp.zeros((s, rest), F32)], axis=1)
    s_lo = jnp.concatenate([-sin, jnp.zeros((s, half + rest), F32)], axis=1)
    return jnp.stack([c, s_hi, s_lo])


def _tile_permutation():
    rho = np.arange(TM)
    pos = CHUNK * (rho % TM_CHUNKS) + rho // TM_CHUNKS
    return (pos[:, None] == np.arange(TM)[None, :]).astype(np.float32)


def _inproj(x, g, w_bf16, rope):
    s = x.shape[0]
    width = MAX_DIL * HEAD_DIM
    perm = jnp.asarray(_tile_permutation(), BF16)
    return pl.pallas_call(
        _inproj_kernel,
        grid=(s // TM,),
        in_specs=[
            pl.BlockSpec((TM, D_MODEL), lambda i: (i, 0)),
            _resident((1, D_MODEL)),
            _resident((D_MODEL, IN_WIDTH)),
            pl.BlockSpec((3, TM, HEAD_DIM), lambda i: (0, i, 0)),
            _resident((TM, TM)),
        ],
        out_specs=[
            pl.BlockSpec((3 * N_HEADS, TM_CHUNKS, width), lambda i: (0, i, 0)),
            pl.BlockSpec((CHUNK, TM_CHUNKS, SSM_WIDTH), lambda i: (0, i, 0)),
        ],
        out_shape=[
            jax.ShapeDtypeStruct((3 * N_HEADS, s // MAX_DIL, width), BF16),
            jax.ShapeDtypeStruct((CHUNK, s // CHUNK, SSM_WIDTH), BF16),
        ],
        scratch_shapes=[pltpu.VMEM((TM, D_MODEL), BF16)],
        compiler_params=_cparams(("parallel",)),
        name="inproj",
    )(x, g, w_bf16, rope, perm)


def _band_bias(tile, perm_mod, perm_mul):
    rho = np.arange(tile)
    lat = perm_mul * (rho % perm_mod) + rho // perm_mod
    jq = lat[:, None]
    jk = np.concatenate([lat - tile, lat])[None, :]
    dist = jq - jk
    valid = (dist >= 0) & (dist <= BAND)
    normal = np.where(valid, 0.0, -np.inf).astype(np.float32)
    first = np.where(valid & (jk >= 0), 0.0, -np.inf).astype(np.float32)
    return np.stack([normal, first])


def _attn_tile(q, k, v, bias, old):
    n = k.shape[0]
    s = lax.dot_general(q, k, (((1,), (1,)), ((), ())), preferred_element_type=F32) + bias
    mt = jnp.max(s, axis=-1, keepdims=True)
    v1 = jnp.concatenate([v, jnp.ones((n, LANES), BF16)], axis=1)
    if old is None:
        m_new = jnp.broadcast_to(mt, (q.shape[0], LANES))
    else:
        acc_o, m_o, l_o = old
        m_new = jnp.maximum(m_o, mt)
    p = jnp.exp(s - jnp.concatenate([m_new] * (n // LANES), axis=1))
    pv = jnp.dot(p.astype(BF16), v1, preferred_element_type=F32)
    o, l = pv[:, :HEAD_DIM], pv[:, HEAD_DIM:]
    if old is not None:
        alpha = jnp.exp(m_o - m_new)
        o = alpha * acc_o + o
        l = alpha * l_o + l
    return o, m_new, l


def _attn_kernel(q_ref, kp_ref, kc_ref, vp_ref, vc_ref, b16_ref, b4_ref, b1_ref,
                 o_ref, acc_ref, m_ref, l_ref):
    first = jnp.where(pl.program_id(1) == 0, 1, 0)

    bias = b16_ref[first]
    for r in range(MAX_DIL):
        k = jnp.concatenate([kp_ref[:, _lane(r)], kc_ref[:, _lane(r)]], axis=0)
        v = jnp.concatenate([vp_ref[:, _lane(r)], vc_ref[:, _lane(r)]], axis=0)
        o, m, l = _attn_tile(q_ref[:, _lane(r)], k, v, bias, None)
        acc_ref[:, _lane(r)] = o
        m_ref[:, _lane(r)] = m
        l_ref[:, _lane(r)] = l

    def run_pattern(dil, rows, final):
        n_c = MAX_DIL // dil
        n_b = BAND // rows
        b_ref = b4_ref if dil == 4 else b1_ref
        for b in range(n_b):
            bias = b_ref[first] if b == 0 else b_ref[0]
            cur = slice(b * rows, (b + 1) * rows)
            prev = slice((b - 1) * rows, b * rows) if b > 0 else slice(BAND - rows, BAND)
            for r in range(dil):
                blocks = [_lane(r + dil * c) for c in range(n_c)]

                def gather(ref, rsl):
                    return jnp.concatenate([ref[rsl, bl] for bl in blocks], axis=0)

                q = gather(q_ref, cur)
                k = jnp.concatenate(
                    [gather(kc_ref if b > 0 else kp_ref, prev), gather(kc_ref, cur)], axis=0)
                v = jnp.concatenate(
                    [gather(vc_ref if b > 0 else vp_ref, prev), gather(vc_ref, cur)], axis=0)
                old = (gather(acc_ref, cur), gather(m_ref, cur), gather(l_ref, cur))
                o, m, l = _attn_tile(q, k, v, bias, old)
                for c, bl in enumerate(blocks):
                    piece = slice(c * rows, (c + 1) * rows)
                    if final:
                        o_ref[cur, bl] = (o[piece] / l[piece]).astype(o_ref.dtype)
                    else:
                        acc_ref[cur, bl] = o[piece]
                        m_ref[cur, bl] = m[piece]
                        l_ref[cur, bl] = l[piece]

    run_pattern(4, 32, False)
    run_pattern(1, 16, True)


def _attention(qkv):
    rows, width = qkv.shape[1:]
    b16 = jnp.asarray(_band_bias(BAND, BAND, 1))
    b4 = jnp.asarray(_band_bias(BAND, BAND // 4, 4))
    b1 = jnp.asarray(_band_bias(2 * BAND, MAX_DIL, MAX_DIL))
    blk = (None, BAND, width)

    def spec(base, prev):
        if prev:
            return pl.BlockSpec(blk, lambda h, i: (base + h, jnp.maximum(i - 1, 0), 0))
        return pl.BlockSpec(blk, lambda h, i: (base + h, i, 0))

    return pl.pallas_call(
        _attn_kernel,
        grid=(N_HEADS, rows // BAND),
        in_specs=[spec(0, False), spec(N_HEADS, True), spec(N_HEADS, False),
                  spec(2 * N_HEADS, True), spec(2 * N_HEADS, False),
                  _resident(b16.shape), _resident(b4.shape), _resident(b1.shape)],
        out_specs=pl.BlockSpec(blk, lambda h, i: (h, i, 0)),
        out_shape=jax.ShapeDtypeStruct((N_HEADS, rows, width), BF16),
        scratch_shapes=[pltpu.VMEM((BAND, width), F32)] * 3,
        compiler_params=_cparams(("parallel", "parallel")),
        name="dilated_attn",
    )(qkv, qkv, qkv, qkv, qkv, b16, b4, b1)


GROUPS_PER_TILE = LANES // SSM_GROUP
N_TILES = N_GROUPS // GROUPS_PER_TILE
STATE_W = GROUPS_PER_TILE * SSM_STATE
FOLD_W = CHUNK * LANES
MXU_W = 256


def _shift_down(x, k, row):
    return jnp.where(row >= k, pltpu.roll(x, k, 0), 0.0)


def _s5_kernel(u_ref, we_ref, wc_ref, bd_ref, sc_ref, y_ref, u3_ref, wt_ref, h_ref):
    n_rows = u_ref.shape[1]
    for t in range(CHUNK):
        u3_ref[:, _lane(t)] = u_ref[t]
    zero_blk = jnp.zeros((LANES, LANES), BF16)
    for tp in range(CHUNK):
        for t in range(tp + 1):
            wt_ref[_lane(t), _lane(tp)] = bd_ref[tp - t]
        if tp % 2 == 0:
            wt_ref[_lane(tp + 1), _lane(tp)] = zero_blk

    h_ref[...] = jnp.dot(u3_ref[...], we_ref[...], preferred_element_type=F32)

    row = lax.broadcasted_iota(jnp.int32, (SUBLANES, STATE_W), 0)

    def block(b, carry):
        cr, ci = carry
        r0 = pl.multiple_of(b * SUBLANES, SUBLANES)
        xr = h_ref[pl.ds(r0, SUBLANES), 0:STATE_W]
        xi = h_ref[pl.ds(r0, SUBLANES), STATE_W:2 * STATE_W]
        for i in range(3):
            kr = sc_ref[16 + 2 * i:17 + 2 * i, :]
            ki = sc_ref[17 + 2 * i:18 + 2 * i, :]
            sr, si = _shift_down(xr, 1 << i, row), _shift_down(xi, 1 << i, row)
            xr, xi = xr + (kr * sr - ki * si), xi + (kr * si + ki * sr)
        pr, pi = sc_ref[0:8, :], sc_ref[8:16, :]
        hr = xr + (pr * cr - pi * ci)
        hi = xi + (pr * ci + pi * cr)
        h_ref[pl.ds(r0, SUBLANES), 0:STATE_W] = jnp.where(row >= 1, pltpu.roll(hr, 1, 0), cr)
        h_ref[pl.ds(r0, SUBLANES), STATE_W:2 * STATE_W] = jnp.where(
            row >= 1, pltpu.roll(hi, 1, 0), ci)
        return hr[SUBLANES - 1:SUBLANES, :], hi[SUBLANES - 1:SUBLANES, :]

    zero = jnp.zeros((1, STATE_W), F32)
    lax.fori_loop(0, n_rows // SUBLANES, block, (zero, zero))

    hin = h_ref[...].astype(BF16)
    for ct in range(FOLD_W // MXU_W):
        cols = slice(ct * MXU_W, (ct + 1) * MXU_W)
        kk = (ct + 1) * MXU_W
        y = (jnp.dot(u3_ref[:, :kk], wt_ref[:kk, cols], preferred_element_type=F32)
             + jnp.dot(hin, wc_ref[:, cols], preferred_element_type=F32))
        y_ref[2 * ct] = y[:, :LANES].astype(y_ref.dtype)
        y_ref[2 * ct + 1] = y[:, LANES:].astype(y_ref.dtype)


def _s5_weights(a_re, a_im, log_dt, b_re, b_im, c_re, c_im, d_skip):
    g, n, p = N_GROUPS, SSM_STATE, SSM_GROUP
    nt, gl = N_TILES, GROUPS_PER_TILE
    ar, ai = a_re.astype(F32), a_im.astype(F32)
    dt = jnp.exp(log_dt.astype(F32))[:, None]

    def apow(k):
        mag, ph = jnp.exp(ar * dt * k), ai * dt * k
        return mag * jnp.cos(ph), mag * jnp.sin(ph)

    abr, abi = apow(1.0)
    nr, ni, den = abr - 1.0, abi, ar * ar + ai * ai
    fr, fi = (nr * ar + ni * ai) / den, (ni * ar - nr * ai) / den
    bre, bim = b_re.astype(F32), b_im.astype(F32)
    bbr = fr[..., None] * bre - fi[..., None] * bim
    bbi = fr[..., None] * bim + fi[..., None] * bre
    cre, cim = c_re.astype(F32), c_im.astype(F32)

    pw = [apow(float(k)) for k in range(CHUNK + 1)]
    pwr = jnp.stack([z[0] for z in pw])
    pwi = jnp.stack([z[1] for z in pw])

    mr = pwr[:CHUNK, :, :, None] * bbr[None] - pwi[:CHUNK, :, :, None] * bbi[None]
    mi = pwr[:CHUNK, :, :, None] * bbi[None] + pwi[:CHUNK, :, :, None] * bbr[None]
    kern = (jnp.einsum('gpn,tgnq->tgpq', cre, mr, precision=HI)
            - jnp.einsum('gpn,tgnq->tgpq', cim, mi, precision=HI))
    kern = kern.at[0].add(jax.vmap(jnp.diag)(d_skip.astype(F32)))
    eye = jnp.eye(gl, dtype=F32)
    bd = jnp.einsum('tjgpq,gh->jtgqhp', kern.reshape(CHUNK, nt, gl, p, p), eye)
    bd = bd.reshape(nt, CHUNK, LANES, LANES)

    m_rev = jnp.stack([mr[::-1], mi[::-1]])
    we = jnp.einsum('ctjgnq,gh->jtgqchn', m_rev.reshape(2, CHUNK, nt, gl, n, p), eye)
    we = we.reshape(nt, FOLD_W, 2 * STATE_W)
    car = cre[None] * pwr[1:, :, None, :] - cim[None] * pwi[1:, :, None, :]
    cai = cre[None] * pwi[1:, :, None, :] + cim[None] * pwr[1:, :, None, :]
    ca = jnp.stack([car, -cai])
    wc = jnp.einsum('ctjgpn,gh->jcgnthp', ca.reshape(2, CHUNK, nt, gl, p, n), eye)
    wc = wc.reshape(nt, 2 * STATE_W, FOLD_W)

    def tile_lanes(z):
        return z.reshape(z.shape[0], nt, gl * n).transpose(1, 0, 2)

    carry = [apow(float(CHUNK * (r + 1))) for r in range(SUBLANES)]
    step = [apow(float(CHUNK * (1 << i))) for i in range(3)]
    step_rows = jnp.stack([z[part] for z in step for part in (0, 1)])
    sc = jnp.concatenate([tile_lanes(jnp.stack([z[0] for z in carry])),
                          tile_lanes(jnp.stack([z[1] for z in carry])),
                          tile_lanes(step_rows),
                          jnp.zeros((nt, 2, STATE_W), F32)], axis=1)
    return we.astype(BF16), wc.astype(BF16), bd.astype(BF16), sc


def _s5_core(u, we, wc, bd, sc):
    _, rows, _ = u.shape
    blk = pl.BlockSpec((CHUNK, rows, LANES), lambda j: (0, 0, j))
    return pl.pallas_call(
        _s5_kernel,
        grid=(N_TILES,),
        in_specs=[
            blk,
            pl.BlockSpec((None, FOLD_W, 2 * STATE_W), lambda j: (j, 0, 0)),
            pl.BlockSpec((None, 2 * STATE_W, FOLD_W), lambda j: (j, 0, 0)),
            pl.BlockSpec((None, CHUNK, LANES, LANES), lambda j: (j, 0, 0, 0)),
            pl.BlockSpec((None, 3 * SUBLANES, STATE_W), lambda j: (j, 0, 0)),
        ],
        out_specs=blk,
        out_shape=jax.ShapeDtypeStruct(u.shape, BF16),
        scratch_shapes=[pltpu.VMEM((rows, FOLD_W), BF16),
                        pltpu.VMEM((FOLD_W, FOLD_W), BF16),
                        pltpu.VMEM((rows, 2 * STATE_W), F32)],
        compiler_params=_cparams(("parallel",)),
        name="s5_scan",
    )(u, we, wc, bd, sc)


def _outproj_kernel(x_ref, a_ref, y_ref, wglu_ref, bglu_ref, wout_ref, g_ref, unperm_ref,
                    h_ref, hn_ref):
    y = jax.nn.gelu(y_ref[...].reshape(TM, SSM_WIDTH).astype(F32))
    gate = jax.nn.sigmoid(
        jnp.dot(y.astype(BF16), wglu_ref[...], preferred_element_type=F32) + bglu_ref[...])
    ssm = (y * gate).astype(BF16)
    attn = jnp.concatenate(
        [jnp.concatenate([a_ref[hh, :, _lane(t)] for t in range(CHUNK)], axis=0)
         for hh in range(N_HEADS)], axis=1)
    mix = jnp.concatenate([attn, ssm], axis=1)
    mix = jnp.dot(unperm_ref[...], mix, preferred_element_type=F32).astype(BF16)
    h = x_ref[...] + jnp.dot(mix, wout_ref[...], preferred_element_type=F32)
    h_ref[...] = h
    ms = jnp.mean(h * h, axis=-1, keepdims=True)
    hn_ref[...] = (h * lax.rsqrt(ms + RMS_EPS) * g_ref[...]).astype(BF16)


def _outproj(x, attn, y, w_glu, b_glu, w_out, g2):
    s = x.shape[0]
    width = MAX_DIL * HEAD_DIM
    unperm = jnp.asarray(_tile_permutation().T, BF16)
    return pl.pallas_call(
        _outproj_kernel,
        grid=(s // TM,),
        in_specs=[
            pl.BlockSpec((TM, D_MODEL), lambda i: (i, 0)),
            pl.BlockSpec((N_HEADS, TM_CHUNKS, width), lambda i: (0, i, 0)),
            pl.BlockSpec((CHUNK, TM_CHUNKS, SSM_WIDTH), lambda i: (0, i, 0)),
            _resident((SSM_WIDTH, SSM_WIDTH)),
            _resident((1, SSM_WIDTH)),
            _resident((D_MODEL, D_MODEL)),
            _resident((1, D_MODEL)),
            _resident((TM, TM)),
        ],
        out_specs=[pl.BlockSpec((TM, D_MODEL), lambda i: (i, 0))] * 2,
        out_shape=[jax.ShapeDtypeStruct((s, D_MODEL), F32),
                   jax.ShapeDtypeStruct((s, D_MODEL), BF16)],
        compiler_params=_cparams(("parallel",)),
        name="outproj",
    )(x, attn, y, w_glu, b_glu, w_out, g2, unperm)


TM_FFN = 512
TF_FFN = 512


def _ffn_kernel(hn_ref, h_ref, wg_ref, wu_ref, wd_ref, g_ref, o_ref, acc_ref):
    f = pl.program_id(1)
    hn = hn_ref[...]
    gate = jnp.dot(hn, wg_ref[...], preferred_element_type=F32)
    up = jnp.dot(hn, wu_ref[...], preferred_element_type=F32)
    act = (jax.nn.silu(gate) * up).astype(BF16)
    part = jnp.dot(act, wd_ref[...], preferred_element_type=F32)

    @pl.when(f == 0)
    def _():
        acc_ref[...] = part

    @pl.when(f > 0)
    def _():
        acc_ref[...] += part

    @pl.when(f == pl.num_programs(1) - 1)
    def _():
        h = h_ref[...] + acc_ref[...]
        ms = jnp.mean(h * h, axis=-1, keepdims=True)
        o_ref[...] = h * lax.rsqrt(ms + RMS_EPS) * g_ref[...]


def _ffn(hn, h, w_gate, w_up, w_down, g):
    s = h.shape[0]
    return pl.pallas_call(
        _ffn_kernel,
        grid=(s // TM_FFN, D_FF // TF_FFN),
        in_specs=[
            pl.BlockSpec((TM_FFN, D_MODEL), lambda i, f: (i, 0)),
            pl.BlockSpec((TM_FFN, D_MODEL), lambda i, f: (i, 0)),
            pl.BlockSpec((D_MODEL, TF_FFN), lambda i, f: (0, f)),
            pl.BlockSpec((D_MODEL, TF_FFN), lambda i, f: (0, f)),
            pl.BlockSpec((TF_FFN, D_MODEL), lambda i, f: (f, 0)),
            pl.BlockSpec((1, D_MODEL), lambda i, f: (0, 0)),
        ],
        out_specs=pl.BlockSpec((TM_FFN, D_MODEL), lambda i, f: (i, 0)),
        out_shape=jax.ShapeDtypeStruct((s, D_MODEL), F32),
        scratch_shapes=[pltpu.VMEM((TM_FFN, D_MODEL), F32)],
        compiler_params=_cparams(("parallel", "arbitrary")),
        name="ffn",
    )(hn, h, w_gate, w_up, w_down, g)


def kernel(x, norm1_g, w_in, a_re, a_im, log_dt, b_re, b_im, c_re, c_im, d_skip, w_glu, b_glu,
           w_out, norm2_g, w_gate, w_up, w_down, final_g):
    b, s, _ = x.shape
    assert b == 1 and s % SUPER == 0 and w_in.shape[0] == 1
    x2 = x[0]
    qkv, u = _inproj(x2, norm1_g[0][None, :], w_in[0].astype(BF16), _rope_tables(s))
    attn = _attention(qkv)
    we, wc, bd, sc = _s5_weights(a_re[0], a_im[0], log_dt[0], b_re[0], b_im[0], c_re[0],
                                 c_im[0], d_skip[0])
    y = _s5_core(u, we, wc, bd, sc)
    h, hn = _outproj(x2, attn, y, w_glu[0].astype(BF16), b_glu[0][None, :].astype(F32),
                     w_out[0].astype(BF16), norm2_g[0][None, :])
    out = _ffn(hn, h, w_gate[0].astype(BF16), w_up[0].astype(BF16), w_down[0].astype(BF16),
               final_g[None, :])
    return out[None]
```

```python
import numpy as np
import jax
import jax.numpy as jnp
from jax import lax
from jax.experimental import pallas as pl
from jax.experimental.pallas import tpu as pltpu

F32 = jnp.float32
BF16 = jnp.bfloat16
HI = lax.Precision.HIGHEST

D_MODEL = 2048
ATTN_WIDTH = 1024
SSM_WIDTH = 1024
HEAD_DIM = 128
N_HEADS = ATTN_WIDTH // HEAD_DIM
ROT_DIM = HEAD_DIM // 4
ROPE_THETA = 500000.0
BAND = 128
MAX_DIL = 16
SUPER = BAND * MAX_DIL
SSM_GROUP = 16
N_GROUPS = SSM_WIDTH // SSM_GROUP
SSM_STATE = 64
CHUNK = 16
D_FF = 5632
IN_WIDTH = 3 * ATTN_WIDTH + SSM_WIDTH
RMS_EPS = 1e-6
LANES = 128
SUBLANES = 8

TM = 512
TM_CHUNKS = TM // CHUNK

VMEM_LIMIT = 58 * 1024 * 1024


def _cparams(sem):
    return pltpu.CompilerParams(dimension_semantics=sem, vmem_limit_bytes=VMEM_LIMIT)


def _resident(shape):
    zeros = (0,) * len(shape)
    return pl.BlockSpec(shape, lambda *_: zeros, pipeline_mode=pl.Buffered(1))


def _rows(t):
    return slice(t * TM_CHUNKS, (t + 1) * TM_CHUNKS)


def _lane(r):
    return slice(r * LANES, (r + 1) * LANES)


TN_IN = 512
HEADS_PER_BLK = TN_IN // HEAD_DIM


def _inproj_kernel(x_ref, g_ref, w_ref, rope_ref, perm_ref, qkv_ref, u_ref, hn_ref):
    x = x_ref[...]
    ms = jnp.mean(x * x, axis=-1, keepdims=True)
    hn = (x * lax.rsqrt(ms + RMS_EPS) * g_ref[...]).astype(BF16)
    hn_ref[...] = jnp.dot(perm_ref[...], hn, preferred_element_type=F32).astype(BF16)

    cos, sin_hi, sin_lo = rope_ref[0], rope_ref[1], rope_ref[2]
    for j in range(IN_WIDTH // TN_IN):
        acc = jnp.dot(hn_ref[...], w_ref[:, j * TN_IN:(j + 1) * TN_IN],
                      preferred_element_type=F32)
        col = j * TN_IN
        if col >= 3 * ATTN_WIDTH:
            for t in range(CHUNK):
                u_ref[t, :, col - 3 * ATTN_WIDTH:col - 3 * ATTN_WIDTH + TN_IN] = (
                    acc[_rows(t), :].astype(BF16))
            continue
        for hh in range(HEADS_PER_BLK):
            r = acc[:, hh * HEAD_DIM:(hh + 1) * HEAD_DIM]
            if col < 2 * ATTN_WIDTH:
                r = (r * cos + pltpu.roll(r, ROT_DIM // 2, 1) * sin_hi
                     + pltpu.roll(r, HEAD_DIM - ROT_DIM // 2, 1) * sin_lo)
            if col < ATTN_WIDTH:
                r = r * (HEAD_DIM ** -0.5)
            r = r.astype(BF16)
            head = j * HEADS_PER_BLK + hh
            for t in range(CHUNK):
                qkv_ref[head, :, _lane(t)] = r[_rows(t), :]


def _rope_tables(s):
    tile = np.arange(s // TM)[:, None, None] * TM
    pos = (tile + np.arange(CHUNK)[None, :, None] + CHUNK * np.arange(TM_CHUNKS)[None, None, :])
    pos = jnp.asarray(pos.reshape(-1), F32)
    inv_freq = ROPE_THETA ** (-jnp.arange(0, ROT_DIM, 2, dtype=F32) / ROT_DIM)
    ang = pos[:, None] * inv_freq[None, :]
    cos, sin = jnp.cos(ang), jnp.sin(ang)
    half = ROT_DIM // 2
    rest = HEAD_DIM - ROT_DIM
    c = jnp.concatenate([cos, cos, jnp.ones((s, rest), F32)], axis=1)
    s_hi = jnp.concatenate([jnp.zeros((s, half), F32), sin, jnp.zeros((s, rest), F32)], axis=1)
    s_lo = jnp.concatenate([-sin, jnp.zeros((s, half + rest), F32)], axis=1)
    return jnp.stack([c, s_hi, s_lo])


def _tile_permutation():
    rho = np.arange(TM)
    pos = CHUNK * (rho % TM_CHUNKS) + rho // TM_CHUNKS
    return (pos[:, None] == np.arange(TM)[None, :]).astype(np.float32)


def _inproj(x, g, w_bf16, rope):
    s = x.shape[0]
    width = MAX_DIL * HEAD_DIM
    perm = jnp.asarray(_tile_permutation(), BF16)
    return pl.pallas_call(
        _inproj_kernel,
        grid=(s // TM,),
        in_specs=[
            pl.BlockSpec((TM, D_MODEL), lambda i: (i, 0)),
            _resident((1, D_MODEL)),
            _resident((D_MODEL, IN_WIDTH)),
            pl.BlockSpec((3, TM, HEAD_DIM), lambda i: (0, i, 0)),
            _resident((TM, TM)),
        ],
        out_specs=[
            pl.BlockSpec((3 * N_HEADS, TM_CHUNKS, width), lambda i: (0, i, 0)),
            pl.BlockSpec((CHUNK, TM_CHUNKS, SSM_WIDTH), lambda i: (0, i, 0)),
        ],
        out_shape=[
            jax.ShapeDtypeStruct((3 * N_HEADS, s // MAX_DIL, width), BF16),
            jax.ShapeDtypeStruct((CHUNK, s // CHUNK, SSM_WIDTH), BF16),
        ],
        scratch_shapes=[pltpu.VMEM((TM, D_MODEL), BF16)],
        compiler_params=_cparams(("parallel",)),
        name="inproj",
    )(x, g, w_bf16, rope, perm)


def _band_bias(tile, perm_mod, perm_mul):
    rho = np.arange(tile)
    lat = perm_mul * (rho % perm_mod) + rho // perm_mod
    jq = lat[:, None]
    jk = np.concatenate([lat - tile, lat])[None, :]
    dist = jq - jk
    valid = (dist >= 0) & (dist <= BAND)
    normal = np.where(valid, 0.0, -np.inf).astype(np.float32)
    first = np.where(valid & (jk >= 0), 0.0, -np.inf).astype(np.float32)
    return np.stack([normal, first])


def _attn_tile(q, k, v, bias, old):
    n = k.shape[0]
    s = lax.dot_general(q, k, (((1,), (1,)), ((), ())), preferred_element_type=F32) + bias
    mt = jnp.max(s, axis=-1, keepdims=True)
    v1 = jnp.concatenate([v, jnp.ones((n, LANES), BF16)], axis=1)
    if old is None:
        m_new = jnp.broadcast_to(mt, (q.shape[0], LANES))
    else:
        acc_o, m_o, l_o = old
        m_new = jnp.maximum(m_o, mt)
    p = jnp.exp(s - jnp.concatenate([m_new] * (n // LANES), axis=1))
    pv = jnp.dot(p.astype(BF16), v1, preferred_element_type=F32)
    o, l = pv[:, :HEAD_DIM], pv[:, HEAD_DIM:]
    if old is not None:
        alpha = jnp.exp(m_o - m_new)
        o = alpha * acc_o + o
        l = alpha * l_o + l
    return o, m_new, l


def _attn_kernel(q_ref, kp_ref, kc_ref, vp_ref, vc_ref, b16_ref, b4_ref, b1_ref,
                 o_ref, acc_ref, m_ref, l_ref):
    first = jnp.where(pl.program_id(1) == 0, 1, 0)

    bias = b16_ref[first]
    for r in range(MAX_DIL):
        k = jnp.concatenate([kp_ref[:, _lane(r)], kc_ref[:, _lane(r)]], axis=0)
        v = jnp.concatenate([vp_ref[:, _lane(r)], vc_ref[:, _lane(r)]], axis=0)
        o, m, l = _attn_tile(q_ref[:, _lane(r)], k, v, bias, None)
        acc_ref[:, _lane(r)] = o
        m_ref[:, _lane(r)] = m
        l_ref[:, _lane(r)] = l

    def run_pattern(dil, rows, final):
        n_c = MAX_DIL // dil
        n_b = BAND // rows
        b_ref = b4_ref if dil == 4 else b1_ref
        for b in range(n_b):
            bias = b_ref[first] if b == 0 else b_ref[0]
            cur = slice(b * rows, (b + 1) * rows)
            prev = slice((b - 1) * rows, b * rows) if b > 0 else slice(BAND - rows, BAND)
            for r in range(dil):
                blocks = [_lane(r + dil * c) for c in range(n_c)]

                def gather(ref, rsl):
                    return jnp.concatenate([ref[rsl, bl] for bl in blocks], axis=0)

                q = gather(q_ref, cur)
                k = jnp.concatenate(
                    [gather(kc_ref if b > 0 else kp_ref, prev), gather(kc_ref, cur)], axis=0)
                v = jnp.concatenate(
                    [gather(vc_ref if b > 0 else vp_ref, prev), gather(vc_ref, cur)], axis=0)
                old = (gather(acc_ref, cur), gather(m_ref, cur), gather(l_ref, cur))
                o, m, l = _attn_tile(q, k, v, bias, old)
                for c, bl in enumerate(blocks):
                    piece = slice(c * rows, (c + 1) * rows)
                    if final:
                        o_ref[cur, bl] = (o[piece] / l[piece]).astype(o_ref.dtype)
                    else:
                        acc_ref[cur, bl] = o[piece]
                        m_ref[cur, bl] = m[piece]
                        l_ref[cur, bl] = l[piece]

    run_pattern(4, 32, False)
    run_pattern(1, 16, True)


def _attention(qkv):
    rows, width = qkv.shape[1:]
    b16 = jnp.asarray(_band_bias(BAND, BAND, 1))
    b4 = jnp.asarray(_band_bias(BAND, BAND // 4, 4))
    b1 = jnp.asarray(_band_bias(2 * BAND, MAX_DIL, MAX_DIL))
    blk = (None, BAND, width)

    def spec(base, prev):
        if prev:
            return pl.BlockSpec(blk, lambda h, i: (base + h, jnp.maximum(i - 1, 0), 0))
        return pl.BlockSpec(blk, lambda h, i: (base + h, i, 0))

    return pl.pallas_call(
        _attn_kernel,
        grid=(N_HEADS, rows // BAND),
        in_specs=[spec(0, False), spec(N_HEADS, True), spec(N_HEADS, False),
                  spec(2 * N_HEADS, True), spec(2 * N_HEADS, False),
                  _resident(b16.shape), _resident(b4.shape), _resident(b1.shape)],
        out_specs=pl.BlockSpec(blk, lambda h, i: (h, i, 0)),
        out_shape=jax.ShapeDtypeStruct((N_HEADS, rows, width), BF16),
        scratch_shapes=[pltpu.VMEM((BAND, width), F32)] * 3,
        compiler_params=_cparams(("parallel", "parallel")),
        name="dilated_attn",
    )(qkv, qkv, qkv, qkv, qkv, b16, b4, b1)


GROUPS_PER_TILE = LANES // SSM_GROUP
N_TILES = N_GROUPS // GROUPS_PER_TILE
STATE_W = GROUPS_PER_TILE * SSM_STATE
FOLD_W = CHUNK * LANES
MXU_W = 256


def _shift_down(x, k, row):
    return jnp.where(row >= k, pltpu.roll(x, k, 0), 0.0)


def _s5_kernel(u_ref, wer_ref, wei_ref, wcr_ref, wci_ref, mask_ref, bd_ref, sc_ref, y_ref,
               u3_ref, wt_ref, we_ref, wct_ref, h_ref):
    n_rows = u_ref.shape[1]
    for t in range(CHUNK):
        u3_ref[:, _lane(t)] = u_ref[t]
    mask = mask_ref[...]
    reps = STATE_W // LANES
    for dst, re_ref, im_ref in ((we_ref, wer_ref, wei_ref), (wct_ref, wcr_ref, wci_ref)):
        dst[:, 0:STATE_W] = jnp.concatenate([re_ref[...]] * reps, axis=1) * mask
        dst[:, STATE_W:2 * STATE_W] = jnp.concatenate([im_ref[...]] * reps, axis=1) * mask
    zero_blk = jnp.zeros((LANES, LANES), BF16)
    for tp in range(CHUNK):
        for t in range(tp + 1):
            wt_ref[_lane(t), _lane(tp)] = bd_ref[tp - t]
        if tp % 2 == 0:
            wt_ref[_lane(tp + 1), _lane(tp)] = zero_blk

    h_ref[...] = jnp.dot(u3_ref[...], we_ref[...], preferred_element_type=F32)

    row = lax.broadcasted_iota(jnp.int32, (SUBLANES, STATE_W), 0)

    def block(b, carry):
        cr, ci = carry
        r0 = pl.multiple_of(b * SUBLANES, SUBLANES)
        xr = h_ref[pl.ds(r0, SUBLANES), 0:STATE_W]
        xi = h_ref[pl.ds(r0, SUBLANES), STATE_W:2 * STATE_W]
        for i in range(3):
            kr = sc_ref[16 + 2 * i:17 + 2 * i, :]
            ki = sc_ref[17 + 2 * i:18 + 2 * i, :]
            sr, si = _shift_down(xr, 1 << i, row), _shift_down(xi, 1 << i, row)
            xr, xi = xr + (kr * sr - ki * si), xi + (kr * si + ki * sr)
        pr, pi = sc_ref[0:8, :], sc_ref[8:16, :]
        hr = xr + (pr * cr - pi * ci)
        hi = xi + (pr * ci + pi * cr)
        h_ref[pl.ds(r0, SUBLANES), 0:STATE_W] = jnp.where(row >= 1, pltpu.roll(hr, 1, 0), cr)
        h_ref[pl.ds(r0, SUBLANES), STATE_W:2 * STATE_W] = jnp.where(
            row >= 1, pltpu.roll(hi, 1, 0), ci)
        return hr[SUBLANES - 1:SUBLANES, :], hi[SUBLANES - 1:SUBLANES, :]

    zero = jnp.zeros((1, STATE_W), F32)
    lax.fori_loop(0, n_rows // SUBLANES, block, (zero, zero))

    hin = h_ref[...].astype(BF16)
    for ct in range(FOLD_W // MXU_W):
        cols = slice(ct * MXU_W, (ct + 1) * MXU_W)
        kk = (ct + 1) * MXU_W
        y = (jnp.dot(u3_ref[:, :kk], wt_ref[:kk, cols], preferred_element_type=F32)
             + lax.dot_general(hin, wct_ref[cols, :], (((1,), (1,)), ((), ())),
                               preferred_element_type=F32))
        y_ref[2 * ct] = y[:, :LANES].astype(y_ref.dtype)
        y_ref[2 * ct + 1] = y[:, LANES:].astype(y_ref.dtype)


def _s5_weights(a_re, a_im, log_dt, b_re, b_im, c_re, c_im, d_skip):
    g, n, p = N_GROUPS, SSM_STATE, SSM_GROUP
    nt, gl = N_TILES, GROUPS_PER_TILE
    ar, ai = a_re.astype(F32), a_im.astype(F32)
    dt = jnp.exp(log_dt.astype(F32))[:, None]

    def apow(ks):
        k = jnp.asarray(ks, F32)[:, None, None]
        mag, ph = jnp.exp(ar * dt * k), ai * dt * k
        return mag * jnp.cos(ph), mag * jnp.sin(ph)

    pwr, pwi = apow(np.arange(CHUNK + 1))
    abr, abi = pwr[1], pwi[1]
    nr, ni, den = abr - 1.0, abi, ar * ar + ai * ai
    fr, fi = (nr * ar + ni * ai) / den, (ni * ar - nr * ai) / den
    bre, bim = b_re.astype(F32), b_im.astype(F32)
    bbr = fr[..., None] * bre - fi[..., None] * bim
    bbi = fr[..., None] * bim + fi[..., None] * bre
    cre, cim = c_re.astype(F32), c_im.astype(F32)

    mr = pwr[:CHUNK, :, :, None] * bbr[None] - pwi[:CHUNK, :, :, None] * bbi[None]
    mi = pwr[:CHUNK, :, :, None] * bbi[None] + pwi[:CHUNK, :, :, None] * bbr[None]
    kern = (jnp.einsum('gpn,tgnq->tgpq', cre, mr, precision=HI)
            - jnp.einsum('gpn,tgnq->tgpq', cim, mi, precision=HI))
    kern = kern.at[0].add(jax.vmap(jnp.diag)(d_skip.astype(F32)))
    same_group = np.kron(np.eye(gl, dtype=np.float32), np.ones((p, p), np.float32))
    bd = kern.reshape(CHUNK, nt, gl, p, p).transpose(1, 0, 2, 4, 3).reshape(nt, CHUNK, LANES, p)
    bd = jnp.tile(bd, (1, 1, 1, gl)) * same_group

    def columns(z):
        z = z.reshape(CHUNK, nt, gl, p, n).transpose(1, 0, 2, 3, 4).reshape(nt, FOLD_W, n)
        return jnp.concatenate([z, z], axis=2).astype(BF16)

    car = cre[None] * pwr[1:, :, None, :] - cim[None] * pwi[1:, :, None, :]
    cai = cre[None] * pwi[1:, :, None, :] + cim[None] * pwr[1:, :, None, :]
    cols = [columns(mr[::-1].transpose(0, 1, 3, 2)), columns(mi[::-1].transpose(0, 1, 3, 2)),
            columns(car), columns(-cai)]

    def tile_lanes(z):
        return z.reshape(z.shape[0], nt, gl * n).transpose(1, 0, 2)

    cyr, cyi = apow(CHUNK * (np.arange(SUBLANES) + 1))
    str_, sti = apow(CHUNK * (1 << np.arange(3)))
    step_rows = jnp.stack([str_, sti], axis=1).reshape(6, g, n)
    sc = jnp.concatenate([tile_lanes(cyr), tile_lanes(cyi), tile_lanes(step_rows),
                          jnp.zeros((nt, 2, STATE_W), F32)], axis=1)
    return cols, bd.astype(BF16), sc


def _group_mask():
    row_g = (np.arange(FOLD_W) // SSM_GROUP) % GROUPS_PER_TILE
    col_h = np.arange(STATE_W) // SSM_STATE
    return (row_g[:, None] == col_h[None, :]).astype(np.float32)


def _s5_core(u, cols, bd, sc):
    _, rows, _ = u.shape
    blk = pl.BlockSpec((CHUNK, rows, LANES), lambda j: (0, 0, j))
    col_spec = pl.BlockSpec((None, FOLD_W, LANES), lambda j: (j, 0, 0))
    mask = jnp.asarray(_group_mask(), BF16)
    return pl.pallas_call(
        _s5_kernel,
        grid=(N_TILES,),
        in_specs=[
            blk, col_spec, col_spec, col_spec, col_spec,
            _resident(mask.shape),
            pl.BlockSpec((None, CHUNK, LANES, LANES), lambda j: (j, 0, 0, 0)),
            pl.BlockSpec((None, 3 * SUBLANES, STATE_W), lambda j: (j, 0, 0)),
        ],
        out_specs=blk,
        out_shape=jax.ShapeDtypeStruct(u.shape, BF16),
        scratch_shapes=[pltpu.VMEM((rows, FOLD_W), BF16),
                        pltpu.VMEM((FOLD_W, FOLD_W), BF16),
                        pltpu.VMEM((FOLD_W, 2 * STATE_W), BF16),
                        pltpu.VMEM((FOLD_W, 2 * STATE_W), BF16),
                        pltpu.VMEM((rows, 2 * STATE_W), F32)],
        compiler_params=_cparams(("parallel",)),
        name="s5_scan",
    )(u, *cols, mask, bd, sc)


def _outproj_kernel(x_ref, a_ref, y_ref, wglu_ref, bglu_ref, wout_ref, g_ref, unperm_ref,
                    h_ref, hn_ref):
    y = jax.nn.gelu(y_ref[...].reshape(TM, SSM_WIDTH).astype(F32))
    gate = jax.nn.sigmoid(
        jnp.dot(y.astype(BF16), wglu_ref[...], preferred_element_type=F32) + bglu_ref[...])
    ssm = (y * gate).astype(BF16)
    attn = jnp.concatenate(
        [jnp.concatenate([a_ref[hh, :, _lane(t)] for t in range(CHUNK)], axis=0)
         for hh in range(N_HEADS)], axis=1)
    mix = jnp.concatenate([attn, ssm], axis=1)
    mix = jnp.dot(unperm_ref[...], mix, preferred_element_type=F32).astype(BF16)
    h = x_ref[...] + jnp.dot(mix, wout_ref[...], preferred_element_type=F32)
    h_ref[...] = h
    ms = jnp.mean(h * h, axis=-1, keepdims=True)
    hn_ref[...] = (h * lax.rsqrt(ms + RMS_EPS) * g_ref[...]).astype(BF16)


def _outproj(x, attn, y, w_glu, b_glu, w_out, g2):
    s = x.shape[0]
    width = MAX_DIL * HEAD_DIM
    unperm = jnp.asarray(_tile_permutation().T, BF16)
    return pl.pallas_call(
        _outproj_kernel,
        grid=(s // TM,),
        in_specs=[
            pl.BlockSpec((TM, D_MODEL), lambda i: (i, 0)),
            pl.BlockSpec((N_HEADS, TM_CHUNKS, width), lambda i: (0, i, 0)),
            pl.BlockSpec((CHUNK, TM_CHUNKS, SSM_WIDTH), lambda i: (0, i, 0)),
            _resident((SSM_WIDTH, SSM_WIDTH)),
            _resident((1, SSM_WIDTH)),
            _resident((D_MODEL, D_MODEL)),
            _resident((1, D_MODEL)),
            _resident((TM, TM)),
        ],
        out_specs=[pl.BlockSpec((TM, D_MODEL), lambda i: (i, 0))] * 2,
        out_shape=[jax.ShapeDtypeStruct((s, D_MODEL), F32),
                   jax.ShapeDtypeStruct((s, D_MODEL), BF16)],
        compiler_params=_cparams(("parallel",)),
        name="outproj",
    )(x, attn, y, w_glu, b_glu, w_out, g2, unperm)


TM_FFN = 512
TF_FFN = 512


N_F = D_FF // TF_FFN


def _ffn_kernel(hn_ref, h_ref, wg_ref, wu_ref, wd_ref, g_ref, o_ref, acc_ref, act_a, act_b):
    k = pl.program_id(0)
    f_prev = lax.rem(k + (N_F - 1), N_F)

    @pl.when(k == 0)
    def _():
        act_b[...] = jnp.zeros_like(act_b)
        acc_ref[...] = jnp.zeros_like(acc_ref)

    def step(act_prev, act_next):
        part = jnp.dot(act_prev[...], wd_ref[...], preferred_element_type=F32)
        hn = hn_ref[...]
        gate = jnp.dot(hn, wg_ref[...], preferred_element_type=F32)
        up = jnp.dot(hn, wu_ref[...], preferred_element_type=F32)
        act_next[...] = (jax.nn.silu(gate) * up).astype(BF16)
        acc_ref[...] = jnp.where(f_prev == 0, part, acc_ref[...] + part)

    @pl.when(lax.rem(k, 2) == 0)
    def _():
        step(act_b, act_a)

    @pl.when(lax.rem(k, 2) == 1)
    def _():
        step(act_a, act_b)

    @pl.when((f_prev == N_F - 1) & (k > 0))
    def _():
        h = h_ref[...] + acc_ref[...]
        ms = jnp.mean(h * h, axis=-1, keepdims=True)
        o_ref[...] = h * lax.rsqrt(ms + RMS_EPS) * g_ref[...]


def _ffn(hn, h, w_gate, w_up, w_down, g):
    s = h.shape[0]
    n_i = s // TM_FFN

    def prev_tile(k):
        return jnp.maximum(k - 1, 0) // N_F

    return pl.pallas_call(
        _ffn_kernel,
        grid=(n_i * N_F + 1,),
        in_specs=[
            pl.BlockSpec((TM_FFN, D_MODEL), lambda k: (jnp.minimum(k // N_F, n_i - 1), 0)),
            pl.BlockSpec((TM_FFN, D_MODEL), lambda k: (prev_tile(k), 0)),
            pl.BlockSpec((D_MODEL, TF_FFN), lambda k: (0, k % N_F)),
            pl.BlockSpec((D_MODEL, TF_FFN), lambda k: (0, k % N_F)),
            pl.BlockSpec((TF_FFN, D_MODEL), lambda k: (jnp.maximum(k - 1, 0) % N_F, 0)),
            _resident((1, D_MODEL)),
        ],
        out_specs=pl.BlockSpec((TM_FFN, D_MODEL), lambda k: (prev_tile(k), 0)),
        out_shape=jax.ShapeDtypeStruct((s, D_MODEL), F32),
        scratch_shapes=[pltpu.VMEM((TM_FFN, D_MODEL), F32),
                        pltpu.VMEM((TM_FFN, TF_FFN), BF16),
                        pltpu.VMEM((TM_FFN, TF_FFN), BF16)],
        compiler_params=_cparams(("arbitrary",)),
        name="ffn",
    )(hn, h, w_gate, w_up, w_down, g)


def kernel(x, norm1_g, w_in, a_re, a_im, log_dt, b_re, b_im, c_re, c_im, d_skip, w_glu, b_glu,
           w_out, norm2_g, w_gate, w_up, w_down, final_g):
    b, s, _ = x.shape
    assert b == 1 and s % SUPER == 0 and w_in.shape[0] == 1
    x2 = x[0]
    qkv, u = _inproj(x2, norm1_g[0][None, :], w_in[0].astype(BF16), _rope_tables(s))
    attn = _attention(qkv)
    cols, bd, sc = _s5_weights(a_re[0], a_im[0], log_dt[0], b_re[0], b_im[0], c_re[0],
                               c_im[0], d_skip[0])
    y = _s5_core(u, cols, bd, sc)
    h, hn = _outproj(x2, attn, y, w_glu[0].astype(BF16), b_glu[0][None, :].astype(F32),
                     w_out[0].astype(BF16), norm2_g[0][None, :])
    out = _ffn(hn, h, w_gate[0].astype(BF16), w_up[0].astype(BF16), w_down[0].astype(BF16),
               final_g[None, :])
    return out[None]
```

```python
import functools

import numpy as np
import jax
import jax.numpy as jnp
from jax import lax
from jax.experimental import pallas as pl
from jax.experimental.pallas import tpu as pltpu

F32 = jnp.float32
BF16 = jnp.bfloat16
HI = lax.Precision.HIGHEST

D_MODEL = 2048
ATTN_WIDTH = 1024
SSM_WIDTH = 1024
HEAD_DIM = 128
N_HEADS = ATTN_WIDTH // HEAD_DIM
ROT_DIM = HEAD_DIM // 4
ROPE_THETA = 500000.0
BAND = 128
MAX_DIL = 16
SUPER = BAND * MAX_DIL
SSM_GROUP = 16
N_GROUPS = SSM_WIDTH // SSM_GROUP
SSM_STATE = 64
CHUNK = 16
D_FF = 5632
IN_WIDTH = 3 * ATTN_WIDTH + SSM_WIDTH
RMS_EPS = 1e-6
LANES = 128
SUBLANES = 8

TM = 512
TM_CHUNKS = TM // CHUNK

VMEM_LIMIT = 58 * 1024 * 1024


def _cparams(sem):
    return pltpu.CompilerParams(dimension_semantics=sem, vmem_limit_bytes=VMEM_LIMIT)


def _resident(shape):
    zeros = (0,) * len(shape)
    return pl.BlockSpec(shape, lambda *_: zeros, pipeline_mode=pl.Buffered(1))


def _rows(t):
    return slice(t * TM_CHUNKS, (t + 1) * TM_CHUNKS)


def _lane(r):
    return slice(r * LANES, (r + 1) * LANES)


TN_IN = 512
HEADS_PER_BLK = TN_IN // HEAD_DIM


def _inproj_kernel(x_ref, g_ref, w_ref, rb_ref, ro_ref, rs_ref, perm_ref, qkv_ref, u_ref, hn_ref):
    x = x_ref[...]
    ms = jnp.mean(x * x, axis=-1, keepdims=True)
    hn = (x * lax.rsqrt(ms + RMS_EPS) * g_ref[...]).astype(BF16)
    hn_ref[...] = jnp.dot(perm_ref[...], hn, preferred_element_type=F32).astype(BF16)

    cb, sb = rb_ref[0:1, :], rb_ref[1:2, :]
    co, so = ro_ref[0], ro_ref[1]
    cos, sin = cb * co - sb * so, sb * co + cb * so
    sin_hi, sin_lo = sin * rs_ref[0:1, :], sin * rs_ref[1:2, :]
    for j in range(IN_WIDTH // TN_IN):
        acc = jnp.dot(hn_ref[...], w_ref[:, j * TN_IN:(j + 1) * TN_IN],
                      preferred_element_type=F32)
        col = j * TN_IN
        if col >= 3 * ATTN_WIDTH:
            for t in range(CHUNK):
                u_ref[t, :, col - 3 * ATTN_WIDTH:col - 3 * ATTN_WIDTH + TN_IN] = (
                    acc[_rows(t), :].astype(BF16))
            continue
        for hh in range(HEADS_PER_BLK):
            r = acc[:, hh * HEAD_DIM:(hh + 1) * HEAD_DIM]
            if col < 2 * ATTN_WIDTH:
                r = (r * cos + pltpu.roll(r, ROT_DIM // 2, 1) * sin_hi
                     + pltpu.roll(r, HEAD_DIM - ROT_DIM // 2, 1) * sin_lo)
            if col < ATTN_WIDTH:
                r = r * (HEAD_DIM ** -0.5)
            r = r.astype(BF16)
            head = j * HEADS_PER_BLK + hh
            for t in range(CHUNK):
                qkv_ref[head, :, _lane(t)] = r[_rows(t), :]


def _rope_tables(s):
    half = ROT_DIM // 2
    freq = np.zeros(HEAD_DIM)
    freq[:ROT_DIM] = np.tile(ROPE_THETA ** (-np.arange(0, ROT_DIM, 2) / ROT_DIM), 2)
    base = (np.arange(s // TM) * TM)[:, None] * freq[None, :]
    rho = np.arange(TM)
    off = (CHUNK * (rho % TM_CHUNKS) + rho // TM_CHUNKS)[:, None] * freq[None, :]
    signs = np.zeros((2, HEAD_DIM))
    signs[0, half:ROT_DIM] = 1.0
    signs[1, :half] = -1.0
    as_f32 = lambda a: jnp.asarray(a.astype(np.float32))
    return (as_f32(np.stack([np.cos(base), np.sin(base)], axis=1)),
            as_f32(np.stack([np.cos(off), np.sin(off)])), as_f32(signs))


def _tile_permutation():
    rho = np.arange(TM)
    pos = CHUNK * (rho % TM_CHUNKS) + rho // TM_CHUNKS
    return (pos[:, None] == np.arange(TM)[None, :]).astype(np.float32)


def _inproj(x, g, w_bf16, rope):
    s = x.shape[0]
    width = MAX_DIL * HEAD_DIM
    perm = jnp.asarray(_tile_permutation(), BF16)
    rope_base, rope_off, rope_signs = rope
    return pl.pallas_call(
        _inproj_kernel,
        grid=(s // TM,),
        in_specs=[
            pl.BlockSpec((TM, D_MODEL), lambda i: (i, 0)),
            _resident((1, D_MODEL)),
            _resident((D_MODEL, IN_WIDTH)),
            pl.BlockSpec((None, 2, HEAD_DIM), lambda i: (i, 0, 0)),
            _resident(rope_off.shape),
            _resident(rope_signs.shape),
            _resident((TM, TM)),
        ],
        out_specs=[
            pl.BlockSpec((3 * N_HEADS, TM_CHUNKS, width), lambda i: (0, i, 0)),
            pl.BlockSpec((CHUNK, TM_CHUNKS, SSM_WIDTH), lambda i: (0, i, 0)),
        ],
        out_shape=[
            jax.ShapeDtypeStruct((3 * N_HEADS, s // MAX_DIL, width), BF16),
            jax.ShapeDtypeStruct((CHUNK, s // CHUNK, SSM_WIDTH), BF16),
        ],
        scratch_shapes=[pltpu.VMEM((TM, D_MODEL), BF16)],
        compiler_params=_cparams(("parallel",)),
        name="inproj",
    )(x, g, w_bf16, rope_base, rope_off, rope_signs, perm)


def _band_bias(tile, perm_mod, perm_mul):
    rho = np.arange(tile)
    lat = perm_mul * (rho % perm_mod) + rho // perm_mod
    jq = lat[:, None]
    jk = np.concatenate([lat - tile, lat])[None, :]
    dist = jq - jk
    valid = (dist >= 0) & (dist <= BAND)
    normal = np.where(valid, 0.0, -np.inf).astype(np.float32)
    first = np.where(valid & (jk >= 0), 0.0, -np.inf).astype(np.float32)
    return np.stack([normal, first])


def _attn_tile(q, k, v, bias, old):
    n = k.shape[0]
    s = lax.dot_general(q, k, (((1,), (1,)), ((), ())), preferred_element_type=F32) + bias
    mt = jnp.max(s, axis=-1, keepdims=True)
    v1 = jnp.concatenate([v, jnp.ones((n, LANES), BF16)], axis=1)
    if old is None:
        m_new = jnp.broadcast_to(mt, (q.shape[0], LANES))
    else:
        acc_o, m_o, l_o = old
        m_new = jnp.maximum(m_o, mt)
    p = jnp.exp(s - jnp.concatenate([m_new] * (n // LANES), axis=1))
    pv = jnp.dot(p.astype(BF16), v1, preferred_element_type=F32)
    o, l = pv[:, :HEAD_DIM], pv[:, HEAD_DIM:]
    if old is not None:
        alpha = jnp.exp(m_o - m_new)
        o = alpha * acc_o + o
        l = alpha * l_o + l
    return o, m_new, l


def _attn_kernel(q_ref, kp_ref, kc_ref, vp_ref, vc_ref, b16_ref, b4_ref, b1_ref,
                 o_ref, acc_ref, m_ref, l_ref):
    first = jnp.where(pl.program_id(1) == 0, 1, 0)

    bias = b16_ref[first]
    for r in range(MAX_DIL):
        k = jnp.concatenate([kp_ref[:, _lane(r)], kc_ref[:, _lane(r)]], axis=0)
        v = jnp.concatenate([vp_ref[:, _lane(r)], vc_ref[:, _lane(r)]], axis=0)
        o, m, l = _attn_tile(q_ref[:, _lane(r)], k, v, bias, None)
        acc_ref[:, _lane(r)] = o
        m_ref[:, _lane(r)] = m
        l_ref[:, _lane(r)] = l

    def run_pattern(dil, rows, final):
        n_c = MAX_DIL // dil
        n_b = BAND // rows
        b_ref = b4_ref if dil == 4 else b1_ref
        for b in range(n_b):
            bias = b_ref[first] if b == 0 else b_ref[0]
            cur = slice(b * rows, (b + 1) * rows)
            prev = slice((b - 1) * rows, b * rows) if b > 0 else slice(BAND - rows, BAND)
            for r in range(dil):
                blocks = [_lane(r + dil * c) for c in range(n_c)]

                def gather(ref, rsl):
                    return jnp.concatenate([ref[rsl, bl] for bl in blocks], axis=0)

                q = gather(q_ref, cur)
                k = jnp.concatenate(
                    [gather(kc_ref if b > 0 else kp_ref, prev), gather(kc_ref, cur)], axis=0)
                v = jnp.concatenate(
                    [gather(vc_ref if b > 0 else vp_ref, prev), gather(vc_ref, cur)], axis=0)
                old = (gather(acc_ref, cur), gather(m_ref, cur), gather(l_ref, cur))
                o, m, l = _attn_tile(q, k, v, bias, old)
                for c, bl in enumerate(blocks):
                    piece = slice(c * rows, (c + 1) * rows)
                    if final:
                        o_ref[cur, bl] = (o[piece] / l[piece]).astype(o_ref.dtype)
                    else:
                        acc_ref[cur, bl] = o[piece]
                        m_ref[cur, bl] = m[piece]
                        l_ref[cur, bl] = l[piece]

    run_pattern(4, 32, False)
    run_pattern(1, 16, True)


def _attention(qkv):
    rows, width = qkv.shape[1:]
    b16 = jnp.asarray(_band_bias(BAND, BAND, 1))
    b4 = jnp.asarray(_band_bias(BAND, BAND // 4, 4))
    b1 = jnp.asarray(_band_bias(2 * BAND, MAX_DIL, MAX_DIL))
    blk = (None, BAND, width)

    def spec(base, prev):
        if prev:
            return pl.BlockSpec(blk, lambda h, i: (base + h, jnp.maximum(i - 1, 0), 0))
        return pl.BlockSpec(blk, lambda h, i: (base + h, i, 0))

    return pl.pallas_call(
        _attn_kernel,
        grid=(N_HEADS, rows // BAND),
        in_specs=[spec(0, False), spec(N_HEADS, True), spec(N_HEADS, False),
                  spec(2 * N_HEADS, True), spec(2 * N_HEADS, False),
                  _resident(b16.shape), _resident(b4.shape), _resident(b1.shape)],
        out_specs=pl.BlockSpec(blk, lambda h, i: (h, i, 0)),
        out_shape=jax.ShapeDtypeStruct((N_HEADS, rows, width), BF16),
        scratch_shapes=[pltpu.VMEM((BAND, width), F32)] * 3,
        compiler_params=_cparams(("parallel", "parallel")),
        name="dilated_attn",
    )(qkv, qkv, qkv, qkv, qkv, b16, b4, b1)


GROUPS_PER_TILE = LANES // SSM_GROUP
N_TILES = N_GROUPS // GROUPS_PER_TILE
STATE_W = GROUPS_PER_TILE * SSM_STATE
FOLD_W = CHUNK * LANES
MXU_W = 256


def _shift_down(x, k, row):
    return jnp.where(row >= k, pltpu.roll(x, k, 0), 0.0)


def _s5_kernel(u_ref, wer_ref, wei_ref, wcr_ref, wci_ref, mask_ref, bd_ref, sc_ref, y_ref,
               u3_ref, wt_ref, we_ref, wct_ref, h_ref):
    n_rows = u_ref.shape[1]
    for t in range(CHUNK):
        u3_ref[:, _lane(t)] = u_ref[t]
    mask = mask_ref[...]
    reps = STATE_W // LANES
    for dst, re_ref, im_ref in ((we_ref, wer_ref, wei_ref), (wct_ref, wcr_ref, wci_ref)):
        dst[:, 0:STATE_W] = jnp.concatenate([re_ref[...]] * reps, axis=1) * mask
        dst[:, STATE_W:2 * STATE_W] = jnp.concatenate([im_ref[...]] * reps, axis=1) * mask
    zero_blk = jnp.zeros((LANES, LANES), BF16)
    for tp in range(CHUNK):
        for t in range(tp + 1):
            wt_ref[_lane(t), _lane(tp)] = bd_ref[tp - t]
        if tp % 2 == 0:
            wt_ref[_lane(tp + 1), _lane(tp)] = zero_blk

    h_ref[...] = jnp.dot(u3_ref[...], we_ref[...], preferred_element_type=F32)

    row = lax.broadcasted_iota(jnp.int32, (SUBLANES, STATE_W), 0)

    def block(b, carry):
        cr, ci = carry
        r0 = pl.multiple_of(b * SUBLANES, SUBLANES)
        xr = h_ref[pl.ds(r0, SUBLANES), 0:STATE_W]
        xi = h_ref[pl.ds(r0, SUBLANES), STATE_W:2 * STATE_W]
        for i in range(3):
            kr = sc_ref[16 + 2 * i:17 + 2 * i, :]
            ki = sc_ref[17 + 2 * i:18 + 2 * i, :]
            sr, si = _shift_down(xr, 1 << i, row), _shift_down(xi, 1 << i, row)
            xr, xi = xr + (kr * sr - ki * si), xi + (kr * si + ki * sr)
        pr, pi = sc_ref[0:8, :], sc_ref[8:16, :]
        hr = xr + (pr * cr - pi * ci)
        hi = xi + (pr * ci + pi * cr)
        h_ref[pl.ds(r0, SUBLANES), 0:STATE_W] = jnp.where(row >= 1, pltpu.roll(hr, 1, 0), cr)
        h_ref[pl.ds(r0, SUBLANES), STATE_W:2 * STATE_W] = jnp.where(
            row >= 1, pltpu.roll(hi, 1, 0), ci)
        return hr[SUBLANES - 1:SUBLANES, :], hi[SUBLANES - 1:SUBLANES, :]

    zero = jnp.zeros((1, STATE_W), F32)
    lax.fori_loop(0, n_rows // SUBLANES, block, (zero, zero))

    hin = h_ref[...].astype(BF16)
    for ct in range(FOLD_W // MXU_W):
        cols = slice(ct * MXU_W, (ct + 1) * MXU_W)
        kk = (ct + 1) * MXU_W
        y = (jnp.dot(u3_ref[:, :kk], wt_ref[:kk, cols], preferred_element_type=F32)
             + lax.dot_general(hin, wct_ref[cols, :], (((1,), (1,)), ((), ())),
                               preferred_element_type=F32))
        y_ref[2 * ct] = y[:, :LANES].astype(y_ref.dtype)
        y_ref[2 * ct + 1] = y[:, LANES:].astype(y_ref.dtype)


def _s5_weights(a_re, a_im, log_dt, b_re, b_im, c_re, c_im, d_skip):
    g, n, p = N_GROUPS, SSM_STATE, SSM_GROUP
    nt, gl = N_TILES, GROUPS_PER_TILE
    ar, ai = a_re.astype(F32), a_im.astype(F32)
    dt = jnp.exp(log_dt.astype(F32))[:, None]

    def apow(ks):
        k = jnp.asarray(ks, F32)[:, None, None]
        mag, ph = jnp.exp(ar * dt * k), ai * dt * k
        return mag * jnp.cos(ph), mag * jnp.sin(ph)

    pwr, pwi = apow(np.arange(CHUNK + 1))
    abr, abi = pwr[1], pwi[1]
    nr, ni, den = abr - 1.0, abi, ar * ar + ai * ai
    fr, fi = (nr * ar + ni * ai) / den, (ni * ar - nr * ai) / den
    bre = b_re.astype(F32).transpose(0, 2, 1)
    bim = b_im.astype(F32).transpose(0, 2, 1)
    bbr = fr[:, None, :] * bre - fi[:, None, :] * bim
    bbi = fr[:, None, :] * bim + fi[:, None, :] * bre
    cre, cim = c_re.astype(F32), c_im.astype(F32)

    mr = pwr[:CHUNK, :, None, :] * bbr[None] - pwi[:CHUNK, :, None, :] * bbi[None]
    mi = pwr[:CHUNK, :, None, :] * bbi[None] + pwi[:CHUNK, :, None, :] * bbr[None]
    kern = (jnp.einsum('gpn,tgqn->tgpq', cre, mr, precision=HI)
            - jnp.einsum('gpn,tgqn->tgpq', cim, mi, precision=HI))
    kern = kern.at[0].add(jax.vmap(jnp.diag)(d_skip.astype(F32)))
    same_group = np.kron(np.eye(gl, dtype=np.float32), np.ones((p, p), np.float32))
    bd = kern.reshape(CHUNK, nt, gl, p, p).transpose(1, 0, 2, 4, 3).reshape(nt, CHUNK, LANES, p)
    bd = jnp.tile(bd, (1, 1, 1, gl)) * same_group

    def columns(z):
        z = z.reshape(CHUNK, nt, gl, p, n).transpose(1, 0, 2, 3, 4).reshape(nt, FOLD_W, n)
        return jnp.concatenate([z, z], axis=2).astype(BF16)

    car = cre[None] * pwr[1:, :, None, :] - cim[None] * pwi[1:, :, None, :]
    cai = cre[None] * pwi[1:, :, None, :] + cim[None] * pwr[1:, :, None, :]
    cols = [columns(mr[::-1]), columns(mi[::-1]), columns(car), columns(-cai)]

    def tile_lanes(z):
        return z.reshape(z.shape[0], nt, gl * n).transpose(1, 0, 2)

    cyr, cyi = apow(CHUNK * (np.arange(SUBLANES) + 1))
    str_, sti = apow(CHUNK * (1 << np.arange(3)))
    step_rows = jnp.stack([str_, sti], axis=1).reshape(6, g, n)
    sc = jnp.concatenate([tile_lanes(cyr), tile_lanes(cyi), tile_lanes(step_rows),
                          jnp.zeros((nt, 2, STATE_W), F32)], axis=1)
    return cols, bd.astype(BF16), sc


def _group_mask():
    row_g = (np.arange(FOLD_W) // SSM_GROUP) % GROUPS_PER_TILE
    col_h = np.arange(STATE_W) // SSM_STATE
    return (row_g[:, None] == col_h[None, :]).astype(np.float32)


def _s5_core(u, cols, bd, sc):
    _, rows, _ = u.shape
    blk = pl.BlockSpec((CHUNK, rows, LANES), lambda j: (0, 0, j))
    col_spec = pl.BlockSpec((None, FOLD_W, LANES), lambda j: (j, 0, 0))
    mask = jnp.asarray(_group_mask(), BF16)
    return pl.pallas_call(
        _s5_kernel,
        grid=(N_TILES,),
        in_specs=[
            blk, col_spec, col_spec, col_spec, col_spec,
            _resident(mask.shape),
            pl.BlockSpec((None, CHUNK, LANES, LANES), lambda j: (j, 0, 0, 0)),
            pl.BlockSpec((None, 3 * SUBLANES, STATE_W), lambda j: (j, 0, 0)),
        ],
        out_specs=blk,
        out_shape=jax.ShapeDtypeStruct(u.shape, BF16),
        scratch_shapes=[pltpu.VMEM((rows, FOLD_W), BF16),
                        pltpu.VMEM((FOLD_W, FOLD_W), BF16),
                        pltpu.VMEM((FOLD_W, 2 * STATE_W), BF16),
                        pltpu.VMEM((FOLD_W, 2 * STATE_W), BF16),
                        pltpu.VMEM((rows, 2 * STATE_W), F32)],
        compiler_params=_cparams(("parallel",)),
        name="s5_scan",
    )(u, *cols, mask, bd, sc)


def _outproj_kernel(x_ref, a_ref, y_ref, wglu_ref, bglu_ref, wout_ref, g_ref, unperm_ref,
                    h_ref, hn_ref):
    y = jax.nn.gelu(y_ref[...].reshape(TM, SSM_WIDTH).astype(F32))
    gate = jax.nn.sigmoid(
        jnp.dot(y.astype(BF16), wglu_ref[...], preferred_element_type=F32) + bglu_ref[...])
    ssm = (y * gate).astype(BF16)
    attn = jnp.concatenate(
        [jnp.concatenate([a_ref[hh, :, _lane(t)] for t in range(CHUNK)], axis=0)
         for hh in range(N_HEADS)], axis=1)
    mix = jnp.concatenate([attn, ssm], axis=1)
    mix = jnp.dot(unperm_ref[...], mix, preferred_element_type=F32).astype(BF16)
    h = x_ref[...] + jnp.dot(mix, wout_ref[...], preferred_element_type=F32)
    h_ref[...] = h
    ms = jnp.mean(h * h, axis=-1, keepdims=True)
    hn_ref[...] = (h * lax.rsqrt(ms + RMS_EPS) * g_ref[...]).astype(BF16)


def _outproj(x, attn, y, w_glu, b_glu, w_out, g2):
    s = x.shape[0]
    width = MAX_DIL * HEAD_DIM
    unperm = jnp.asarray(_tile_permutation().T, BF16)
    return pl.pallas_call(
        _outproj_kernel,
        grid=(s // TM,),
        in_specs=[
            pl.BlockSpec((TM, D_MODEL), lambda i: (i, 0)),
            pl.BlockSpec((N_HEADS, TM_CHUNKS, width), lambda i: (0, i, 0)),
            pl.BlockSpec((CHUNK, TM_CHUNKS, SSM_WIDTH), lambda i: (0, i, 0)),
            _resident((SSM_WIDTH, SSM_WIDTH)),
            _resident((1, SSM_WIDTH)),
            _resident((D_MODEL, D_MODEL)),
            _resident((1, D_MODEL)),
            _resident((TM, TM)),
        ],
        out_specs=[pl.BlockSpec((TM, D_MODEL), lambda i: (i, 0))] * 2,
        out_shape=[jax.ShapeDtypeStruct((s, D_MODEL), F32),
                   jax.ShapeDtypeStruct((s, D_MODEL), BF16)],
        compiler_params=_cparams(("parallel",)),
        name="outproj",
    )(x, attn, y, w_glu, b_glu, w_out, g2, unperm)


TM_FFN = 1024
TF_FFN = 512
N_F = D_FF // TF_FFN
LOAD_AT = N_F // 2


def _ffn_kernel(n_tiles, hn_ref, h_hbm, wg_ref, wu_ref, wd_ref, g_ref, o_hbm,
                acc_ref, act_a, act_b, sem_in, sem_out):
    k = pl.program_id(0)
    f_prev = lax.rem(k + (N_F - 1), N_F)
    tile = jnp.maximum(k - 1, 0) // N_F
    slot = lax.rem(tile, 2)

    def rows(j):
        return pl.ds(pl.multiple_of(j * TM_FFN, TM_FFN), TM_FFN)

    def copy_in(j, s):
        return pltpu.make_async_copy(h_hbm.at[rows(j), :], acc_ref.at[s], sem_in.at[s])

    def copy_out(j, s):
        return pltpu.make_async_copy(acc_ref.at[s], o_hbm.at[rows(j), :], sem_out.at[s])

    @pl.when(k == 0)
    def _():
        act_b[...] = jnp.zeros_like(act_b)
        acc_ref[0] = jnp.zeros((TM_FFN, D_MODEL), F32)

    @pl.when(f_prev == 0)
    def _():
        copy_in(tile, slot).wait()

    def step(act_prev, act_next):
        part = jnp.dot(act_prev[...], wd_ref[...], preferred_element_type=F32)
        hn = hn_ref[...]
        gate = jnp.dot(hn, wg_ref[...], preferred_element_type=F32)
        up = jnp.dot(hn, wu_ref[...], preferred_element_type=F32)
        act_next[...] = (jax.nn.silu(gate) * up).astype(BF16)
        acc_ref[slot] = acc_ref[slot] + part

    @pl.when(lax.rem(k, 2) == 0)
    def _():
        step(act_b, act_a)

    @pl.when(lax.rem(k, 2) == 1)
    def _():
        step(act_a, act_b)

    @pl.when(k == 0)
    def _():
        copy_in(0, 0).start()

    @pl.when((f_prev == LOAD_AT) & (tile >= 1))
    def _():
        copy_out(tile - 1, 1 - slot).wait()

    @pl.when((f_prev == LOAD_AT) & (tile + 1 < n_tiles))
    def _():
        copy_in(tile + 1, 1 - slot).start()

    @pl.when((f_prev == N_F - 1) & (k > 0))
    def _():
        h = acc_ref[slot]
        ms = jnp.mean(h * h, axis=-1, keepdims=True)
        acc_ref[slot] = h * lax.rsqrt(ms + RMS_EPS) * g_ref[...]
        copy_out(tile, slot).start()

    @pl.when(k == pl.num_programs(0) - 1)
    def _():
        copy_out(tile, slot).wait()


def _ffn(hn, h, w_gate, w_up, w_down, g):
    s = h.shape[0]
    n_i = s // TM_FFN
    return pl.pallas_call(
        functools.partial(_ffn_kernel, n_i),
        grid=(n_i * N_F + 1,),
        in_specs=[
            pl.BlockSpec((TM_FFN, D_MODEL), lambda k: (jnp.minimum(k // N_F, n_i - 1), 0)),
            pl.BlockSpec(memory_space=pl.ANY),
            pl.BlockSpec((D_MODEL, TF_FFN), lambda k: (0, k % N_F)),
            pl.BlockSpec((D_MODEL, TF_FFN), lambda k: (0, k % N_F)),
            pl.BlockSpec((TF_FFN, D_MODEL), lambda k: (jnp.maximum(k - 1, 0) % N_F, 0)),
            _resident((1, D_MODEL)),
        ],
        out_specs=pl.BlockSpec(memory_space=pl.ANY),
        out_shape=jax.ShapeDtypeStruct((s, D_MODEL), F32),
        scratch_shapes=[pltpu.VMEM((2, TM_FFN, D_MODEL), F32),
                        pltpu.VMEM((TM_FFN, TF_FFN), BF16),
                        pltpu.VMEM((TM_FFN, TF_FFN), BF16),
                        pltpu.SemaphoreType.DMA((2,)),
                        pltpu.SemaphoreType.DMA((2,))],
        compiler_params=_cparams(("arbitrary",)),
        name="ffn",
    )(hn, h, w_gate, w_up, w_down, g)


def kernel(x, norm1_g, w_in, a_re, a_im, log_dt, b_re, b_im, c_re, c_im, d_skip, w_glu, b_glu,
           w_out, norm2_g, w_gate, w_up, w_down, final_g):
    b, s, _ = x.shape
    assert b == 1 and s % SUPER == 0 and w_in.shape[0] == 1
    x2 = x[0]
    qkv, u = _inproj(x2, norm1_g[0][None, :], w_in[0].astype(BF16), _rope_tables(s))
    attn = _attention(qkv)
    cols, bd, sc = _s5_weights(a_re[0], a_im[0], log_dt[0], b_re[0], b_im[0], c_re[0],
                               c_im[0], d_skip[0])
    y = _s5_core(u, cols, bd, sc)
    h, hn = _outproj(x2, attn, y, w_glu[0].astype(BF16), b_glu[0][None, :].astype(F32),
                     w_out[0].astype(BF16), norm2_g[0][None, :])
    out = _ffn(hn, h, w_gate[0].astype(BF16), w_up[0].astype(BF16), w_down[0].astype(BF16),
               final_g[None, :])
    return out[None]
```

```python
import functools

import numpy as np
import jax
import jax.numpy as jnp
from jax import lax
from jax.experimental import pallas as pl
from jax.experimental.pallas import tpu as pltpu

F32 = jnp.float32
BF16 = jnp.bfloat16
HI = lax.Precision.HIGHEST

D_MODEL = 2048
ATTN_WIDTH = 1024
SSM_WIDTH = 1024
HEAD_DIM = 128
N_HEADS = ATTN_WIDTH // HEAD_DIM
ROT_DIM = HEAD_DIM // 4
ROPE_THETA = 500000.0
BAND = 128
MAX_DIL = 16
SUPER = BAND * MAX_DIL
SSM_GROUP = 16
N_GROUPS = SSM_WIDTH // SSM_GROUP
SSM_STATE = 64
CHUNK = 16
D_FF = 5632
IN_WIDTH = 3 * ATTN_WIDTH + SSM_WIDTH
RMS_EPS = 1e-6
LANES = 128
SUBLANES = 8

TM = 512
TM_CHUNKS = TM // CHUNK

VMEM_LIMIT = 58 * 1024 * 1024


def _cparams(sem):
    return pltpu.CompilerParams(dimension_semantics=sem, vmem_limit_bytes=VMEM_LIMIT)


def _resident(shape):
    zeros = (0,) * len(shape)
    return pl.BlockSpec(shape, lambda *_: zeros, pipeline_mode=pl.Buffered(1))


def _rows(t):
    return slice(t * TM_CHUNKS, (t + 1) * TM_CHUNKS)


def _lane(r):
    return slice(r * LANES, (r + 1) * LANES)


TN_IN = 512
HEADS_PER_BLK = TN_IN // HEAD_DIM


def _inproj_kernel(x_ref, g_ref, w_ref, rb_ref, ro_ref, rs_ref, perm_ref, qkv_ref, u_ref, hn_ref):
    x = x_ref[...]
    ms = jnp.mean(x * x, axis=-1, keepdims=True)
    hn = (x * lax.rsqrt(ms + RMS_EPS) * g_ref[...]).astype(BF16)
    hn_ref[...] = jnp.dot(perm_ref[...], hn, preferred_element_type=F32).astype(BF16)

    cb, sb = rb_ref[0:1, :], rb_ref[1:2, :]
    co, so = ro_ref[0], ro_ref[1]
    cos, sin = cb * co - sb * so, sb * co + cb * so
    sin_hi, sin_lo = sin * rs_ref[0:1, :], sin * rs_ref[1:2, :]
    for j in range(IN_WIDTH // TN_IN):
        acc = jnp.dot(hn_ref[...], w_ref[:, j * TN_IN:(j + 1) * TN_IN],
                      preferred_element_type=F32)
        col = j * TN_IN
        if col >= 3 * ATTN_WIDTH:
            for t in range(CHUNK):
                u_ref[t, :, col - 3 * ATTN_WIDTH:col - 3 * ATTN_WIDTH + TN_IN] = (
                    acc[_rows(t), :].astype(BF16))
            continue
        for hh in range(HEADS_PER_BLK):
            r = acc[:, hh * HEAD_DIM:(hh + 1) * HEAD_DIM]
            if col < 2 * ATTN_WIDTH:
                r = (r * cos + pltpu.roll(r, ROT_DIM // 2, 1) * sin_hi
                     + pltpu.roll(r, HEAD_DIM - ROT_DIM // 2, 1) * sin_lo)
            if col < ATTN_WIDTH:
                r = r * (HEAD_DIM ** -0.5)
            r = r.astype(BF16)
            head = j * HEADS_PER_BLK + hh
            for t in range(CHUNK):
                qkv_ref[head, :, _lane(t)] = r[_rows(t), :]


def _rope_tables(s):
    half = ROT_DIM // 2
    freq = np.zeros(HEAD_DIM)
    freq[:ROT_DIM] = np.tile(ROPE_THETA ** (-np.arange(0, ROT_DIM, 2) / ROT_DIM), 2)
    base = (np.arange(s // TM) * TM)[:, None] * freq[None, :]
    rho = np.arange(TM)
    off = (CHUNK * (rho % TM_CHUNKS) + rho // TM_CHUNKS)[:, None] * freq[None, :]
    signs = np.zeros((2, HEAD_DIM))
    signs[0, half:ROT_DIM] = 1.0
    signs[1, :half] = -1.0
    as_f32 = lambda a: jnp.asarray(a.astype(np.float32))
    return (as_f32(np.stack([np.cos(base), np.sin(base)], axis=1)),
            as_f32(np.stack([np.cos(off), np.sin(off)])), as_f32(signs))


def _tile_permutation():
    rho = np.arange(TM)
    pos = CHUNK * (rho % TM_CHUNKS) + rho // TM_CHUNKS
    return (pos[:, None] == np.arange(TM)[None, :]).astype(np.float32)


def _inproj(x, g, w_bf16, rope):
    s = x.shape[0]
    width = MAX_DIL * HEAD_DIM
    perm = jnp.asarray(_tile_permutation(), BF16)
    rope_base, rope_off, rope_signs = rope
    return pl.pallas_call(
        _inproj_kernel,
        grid=(s // TM,),
        in_specs=[
            pl.BlockSpec((TM, D_MODEL), lambda i: (i, 0)),
            _resident((1, D_MODEL)),
            _resident((D_MODEL, IN_WIDTH)),
            pl.BlockSpec((None, 2, HEAD_DIM), lambda i: (i, 0, 0)),
            _resident(rope_off.shape),
            _resident(rope_signs.shape),
            _resident((TM, TM)),
        ],
        out_specs=[
            pl.BlockSpec((3 * N_HEADS, TM_CHUNKS, width), lambda i: (0, i, 0)),
            pl.BlockSpec((CHUNK, TM_CHUNKS, SSM_WIDTH), lambda i: (0, i, 0)),
        ],
        out_shape=[
            jax.ShapeDtypeStruct((3 * N_HEADS, s // MAX_DIL, width), BF16),
            jax.ShapeDtypeStruct((CHUNK, s // CHUNK, SSM_WIDTH), BF16),
        ],
        scratch_shapes=[pltpu.VMEM((TM, D_MODEL), BF16)],
        compiler_params=_cparams(("parallel",)),
        name="inproj",
    )(x, g, w_bf16, rope_base, rope_off, rope_signs, perm)


def _band_bias(tile, perm_mod, perm_mul):
    rho = np.arange(tile)
    lat = perm_mul * (rho % perm_mod) + rho // perm_mod
    jq = lat[:, None]
    jk = np.concatenate([lat - tile, lat])[None, :]
    dist = jq - jk
    valid = (dist >= 0) & (dist <= BAND)
    normal = np.where(valid, 0.0, -np.inf).astype(np.float32)
    first = np.where(valid & (jk >= 0), 0.0, -np.inf).astype(np.float32)
    return np.stack([normal, first])


def _attn_tile(q, k, v, bias, old):
    n = k.shape[0]
    s = lax.dot_general(q, k, (((1,), (1,)), ((), ())), preferred_element_type=F32) + bias
    mt = jnp.max(s, axis=-1, keepdims=True)
    v1 = jnp.concatenate([v, jnp.ones((n, LANES), BF16)], axis=1)
    if old is None:
        m_new = jnp.broadcast_to(mt, (q.shape[0], LANES))
    else:
        acc_o, m_o, l_o = old
        m_new = jnp.maximum(m_o, mt)
    p = jnp.exp(s - jnp.concatenate([m_new] * (n // LANES), axis=1))
    pv = jnp.dot(p.astype(BF16), v1, preferred_element_type=F32)
    o, l = pv[:, :HEAD_DIM], pv[:, HEAD_DIM:]
    if old is not None:
        alpha = jnp.exp(m_o - m_new)
        o = alpha * acc_o + o
        l = alpha * l_o + l
    return o, m_new, l


def _attn_kernel(q_ref, kp_ref, kc_ref, vp_ref, vc_ref, b16_ref, b4_ref, b1_ref,
                 o_ref, acc_ref, m_ref, l_ref):
    first = jnp.where(pl.program_id(1) == 0, 1, 0)

    bias = b16_ref[first]
    for r in range(MAX_DIL):
        k = jnp.concatenate([kp_ref[:, _lane(r)], kc_ref[:, _lane(r)]], axis=0)
        v = jnp.concatenate([vp_ref[:, _lane(r)], vc_ref[:, _lane(r)]], axis=0)
        o, m, l = _attn_tile(q_ref[:, _lane(r)], k, v, bias, None)
        acc_ref[:, _lane(r)] = o
        m_ref[:, _lane(r)] = m
        l_ref[:, _lane(r)] = l

    def run_pattern(dil, rows, final):
        n_c = MAX_DIL // dil
        n_b = BAND // rows
        b_ref = b4_ref if dil == 4 else b1_ref
        for b in range(n_b):
            bias = b_ref[first] if b == 0 else b_ref[0]
            cur = slice(b * rows, (b + 1) * rows)
            prev = slice((b - 1) * rows, b * rows) if b > 0 else slice(BAND - rows, BAND)
            for r in range(dil):
                blocks = [_lane(r + dil * c) for c in range(n_c)]

                def gather(ref, rsl):
                    return jnp.concatenate([ref[rsl, bl] for bl in blocks], axis=0)

                q = gather(q_ref, cur)
                k = jnp.concatenate(
                    [gather(kc_ref if b > 0 else kp_ref, prev), gather(kc_ref, cur)], axis=0)
                v = jnp.concatenate(
                    [gather(vc_ref if b > 0 else vp_ref, prev), gather(vc_ref, cur)], axis=0)
                old = (gather(acc_ref, cur), gather(m_ref, cur), gather(l_ref, cur))
                o, m, l = _attn_tile(q, k, v, bias, old)
                for c, bl in enumerate(blocks):
                    piece = slice(c * rows, (c + 1) * rows)
                    if final:
                        o_ref[cur, bl] = (o[piece] / l[piece]).astype(o_ref.dtype)
                    else:
                        acc_ref[cur, bl] = o[piece]
                        m_ref[cur, bl] = m[piece]
                        l_ref[cur, bl] = l[piece]

    run_pattern(4, 32, False)
    run_pattern(1, 16, True)


def _attention(qkv):
    rows, width = qkv.shape[1:]
    b16 = jnp.asarray(_band_bias(BAND, BAND, 1))
    b4 = jnp.asarray(_band_bias(BAND, BAND // 4, 4))
    b1 = jnp.asarray(_band_bias(2 * BAND, MAX_DIL, MAX_DIL))
    blk = (None, BAND, width)

    def spec(base, prev):
        if prev:
            return pl.BlockSpec(blk, lambda h, i: (base + h, jnp.maximum(i - 1, 0), 0))
        return pl.BlockSpec(blk, lambda h, i: (base + h, i, 0))

    return pl.pallas_call(
        _attn_kernel,
        grid=(N_HEADS, rows // BAND),
        in_specs=[spec(0, False), spec(N_HEADS, True), spec(N_HEADS, False),
                  spec(2 * N_HEADS, True), spec(2 * N_HEADS, False),
                  _resident(b16.shape), _resident(b4.shape), _resident(b1.shape)],
        out_specs=pl.BlockSpec(blk, lambda h, i: (h, i, 0)),
        out_shape=jax.ShapeDtypeStruct((N_HEADS, rows, width), BF16),
        scratch_shapes=[pltpu.VMEM((BAND, width), F32)] * 3,
        compiler_params=_cparams(("parallel", "parallel")),
        name="dilated_attn",
    )(qkv, qkv, qkv, qkv, qkv, b16, b4, b1)


GROUPS_PER_TILE = LANES // SSM_GROUP
N_TILES = N_GROUPS // GROUPS_PER_TILE
STATE_W = GROUPS_PER_TILE * SSM_STATE
FOLD_W = CHUNK * LANES
MXU_W = 256


def _shift_down(x, k, row):
    return jnp.where(row >= k, pltpu.roll(x, k, 0), 0.0)


def _s5_kernel(u_ref, wer_ref, wei_ref, wcr_ref, wci_ref, mask_ref, bd_ref, sc_ref, y_ref,
               u3_ref, wt_ref, we_ref, wct_ref, h_ref):
    n_rows = u_ref.shape[1]
    for t in range(CHUNK):
        u3_ref[:, _lane(t)] = u_ref[t]
    mask = mask_ref[...]
    reps = STATE_W // LANES
    for dst, re_ref, im_ref in ((we_ref, wer_ref, wei_ref), (wct_ref, wcr_ref, wci_ref)):
        dst[:, 0:STATE_W] = jnp.concatenate([re_ref[...]] * reps, axis=1) * mask
        dst[:, STATE_W:2 * STATE_W] = jnp.concatenate([im_ref[...]] * reps, axis=1) * mask
    zero_blk = jnp.zeros((LANES, LANES), BF16)
    for tp in range(CHUNK):
        for t in range(tp + 1):
            wt_ref[_lane(t), _lane(tp)] = bd_ref[tp - t]
        if tp % 2 == 0:
            wt_ref[_lane(tp + 1), _lane(tp)] = zero_blk

    h_ref[...] = jnp.dot(u3_ref[...], we_ref[...], preferred_element_type=F32)

    row = lax.broadcasted_iota(jnp.int32, (SUBLANES, STATE_W), 0)

    def block(b, carry):
        cr, ci = carry
        r0 = pl.multiple_of(b * SUBLANES, SUBLANES)
        xr = h_ref[pl.ds(r0, SUBLANES), 0:STATE_W]
        xi = h_ref[pl.ds(r0, SUBLANES), STATE_W:2 * STATE_W]
        for i in range(3):
            kr = sc_ref[16 + 2 * i:17 + 2 * i, :]
            ki = sc_ref[17 + 2 * i:18 + 2 * i, :]
            sr, si = _shift_down(xr, 1 << i, row), _shift_down(xi, 1 << i, row)
            xr, xi = xr + (kr * sr - ki * si), xi + (kr * si + ki * sr)
        pr, pi = sc_ref[0:8, :], sc_ref[8:16, :]
        hr = xr + (pr * cr - pi * ci)
        hi = xi + (pr * ci + pi * cr)
        h_ref[pl.ds(r0, SUBLANES), 0:STATE_W] = jnp.where(row >= 1, pltpu.roll(hr, 1, 0), cr)
        h_ref[pl.ds(r0, SUBLANES), STATE_W:2 * STATE_W] = jnp.where(
            row >= 1, pltpu.roll(hi, 1, 0), ci)
        return hr[SUBLANES - 1:SUBLANES, :], hi[SUBLANES - 1:SUBLANES, :]

    zero = jnp.zeros((1, STATE_W), F32)
    lax.fori_loop(0, n_rows // SUBLANES, block, (zero, zero))

    hin = h_ref[...].astype(BF16)
    for ct in range(FOLD_W // MXU_W):
        cols = slice(ct * MXU_W, (ct + 1) * MXU_W)
        kk = (ct + 1) * MXU_W
        y = (jnp.dot(u3_ref[:, :kk], wt_ref[:kk, cols], preferred_element_type=F32)
             + lax.dot_general(hin, wct_ref[cols, :], (((1,), (1,)), ((), ())),
                               preferred_element_type=F32))
        y_ref[2 * ct] = y[:, :LANES].astype(y_ref.dtype)
        y_ref[2 * ct + 1] = y[:, LANES:].astype(y_ref.dtype)


def _s5_weights(a_re, a_im, log_dt, b_re, b_im, c_re, c_im, d_skip):
    g, n, p = N_GROUPS, SSM_STATE, SSM_GROUP
    nt, gl = N_TILES, GROUPS_PER_TILE
    ar, ai = a_re.astype(F32), a_im.astype(F32)
    dt = jnp.exp(log_dt.astype(F32))[:, None]

    def apow(ks):
        k = jnp.asarray(ks, F32)[:, None, None]
        mag, ph = jnp.exp(ar * dt * k), ai * dt * k
        return mag * jnp.cos(ph), mag * jnp.sin(ph)

    pwr, pwi = apow(np.arange(CHUNK + 1))
    abr, abi = pwr[1], pwi[1]
    nr, ni, den = abr - 1.0, abi, ar * ar + ai * ai
    fr, fi = (nr * ar + ni * ai) / den, (ni * ar - nr * ai) / den
    bre = b_re.astype(F32).transpose(0, 2, 1)
    bim = b_im.astype(F32).transpose(0, 2, 1)
    bbr = fr[:, None, :] * bre - fi[:, None, :] * bim
    bbi = fr[:, None, :] * bim + fi[:, None, :] * bre
    cre, cim = c_re.astype(F32), c_im.astype(F32)

    mr = pwr[:CHUNK, :, None, :] * bbr[None] - pwi[:CHUNK, :, None, :] * bbi[None]
    mi = pwr[:CHUNK, :, None, :] * bbi[None] + pwi[:CHUNK, :, None, :] * bbr[None]
    kern = (jnp.einsum('gpn,tgqn->tgpq', cre, mr, precision=HI)
            - jnp.einsum('gpn,tgqn->tgpq', cim, mi, precision=HI))
    kern = kern.at[0].add(jax.vmap(jnp.diag)(d_skip.astype(F32)))
    same_group = np.kron(np.eye(gl, dtype=np.float32), np.ones((p, p), np.float32))
    bd = kern.reshape(CHUNK, nt, gl, p, p).transpose(1, 0, 2, 4, 3).reshape(nt, CHUNK, LANES, p)
    bd = jnp.tile(bd, (1, 1, 1, gl)) * same_group

    def dup(z):
        return jnp.concatenate([z, z], axis=-1)

    def columns(z):
        z = z.reshape(CHUNK, nt, gl, p, LANES).transpose(1, 0, 2, 3, 4)
        return z.reshape(nt, FOLD_W, LANES).astype(BF16)

    er, ei = dup(pwr[CHUNK - 1::-1])[:, :, None, :], dup(pwi[CHUNK - 1::-1])[:, :, None, :]
    br, bi = dup(bbr)[None], dup(bbi)[None]
    cr, ci = dup(cre)[None], dup(cim)[None]
    ar1, ai1 = dup(pwr[1:])[:, :, None, :], dup(pwi[1:])[:, :, None, :]
    cols = [columns(er * br - ei * bi), columns(er * bi + ei * br),
            columns(cr * ar1 - ci * ai1), columns(-(cr * ai1 + ci * ar1))]

    def tile_lanes(z):
        return z.reshape(z.shape[0], nt, gl * n).transpose(1, 0, 2)

    cyr, cyi = apow(CHUNK * (np.arange(SUBLANES) + 1))
    str_, sti = apow(CHUNK * (1 << np.arange(3)))
    step_rows = jnp.stack([str_, sti], axis=1).reshape(6, g, n)
    sc = jnp.concatenate([tile_lanes(cyr), tile_lanes(cyi), tile_lanes(step_rows),
                          jnp.zeros((nt, 2, STATE_W), F32)], axis=1)
    return cols, bd.astype(BF16), sc


def _group_mask():
    row_g = (np.arange(FOLD_W) // SSM_GROUP) % GROUPS_PER_TILE
    col_h = np.arange(STATE_W) // SSM_STATE
    return (row_g[:, None] == col_h[None, :]).astype(np.float32)


def _s5_core(u, cols, bd, sc):
    _, rows, _ = u.shape
    blk = pl.BlockSpec((CHUNK, rows, LANES), lambda j: (0, 0, j))
    col_spec = pl.BlockSpec((None, FOLD_W, LANES), lambda j: (j, 0, 0))
    mask = jnp.asarray(_group_mask(), BF16)
    return pl.pallas_call(
        _s5_kernel,
        grid=(N_TILES,),
        in_specs=[
            blk, col_spec, col_spec, col_spec, col_spec,
            _resident(mask.shape),
            pl.BlockSpec((None, CHUNK, LANES, LANES), lambda j: (j, 0, 0, 0)),
            pl.BlockSpec((None, 3 * SUBLANES, STATE_W), lambda j: (j, 0, 0)),
        ],
        out_specs=blk,
        out_shape=jax.ShapeDtypeStruct(u.shape, BF16),
        scratch_shapes=[pltpu.VMEM((rows, FOLD_W), BF16),
                        pltpu.VMEM((FOLD_W, FOLD_W), BF16),
                        pltpu.VMEM((FOLD_W, 2 * STATE_W), BF16),
                        pltpu.VMEM((FOLD_W, 2 * STATE_W), BF16),
                        pltpu.VMEM((rows, 2 * STATE_W), F32)],
        compiler_params=_cparams(("parallel",)),
        name="s5_scan",
    )(u, *cols, mask, bd, sc)


def _outproj_kernel(x_ref, a_ref, y_ref, wglu_ref, bglu_ref, wout_ref, g_ref, unperm_ref,
                    h_ref, hn_ref):
    y = jax.nn.gelu(y_ref[...].reshape(TM, SSM_WIDTH).astype(F32))
    gate = jax.nn.sigmoid(
        jnp.dot(y.astype(BF16), wglu_ref[...], preferred_element_type=F32) + bglu_ref[...])
    ssm = (y * gate).astype(BF16)
    attn = jnp.concatenate(
        [jnp.concatenate([a_ref[hh, :, _lane(t)] for t in range(CHUNK)], axis=0)
         for hh in range(N_HEADS)], axis=1)
    mix = jnp.concatenate([attn, ssm], axis=1)
    mix = jnp.dot(unperm_ref[...], mix, preferred_element_type=F32).astype(BF16)
    h = x_ref[...] + jnp.dot(mix, wout_ref[...], preferred_element_type=F32)
    h_ref[...] = h
    ms = jnp.mean(h * h, axis=-1, keepdims=True)
    hn_ref[...] = (h * lax.rsqrt(ms + RMS_EPS) * g_ref[...]).astype(BF16)


def _outproj(x, attn, y, w_glu, b_glu, w_out, g2):
    s = x.shape[0]
    width = MAX_DIL * HEAD_DIM
    unperm = jnp.asarray(_tile_permutation().T, BF16)
    return pl.pallas_call(
        _outproj_kernel,
        grid=(s // TM,),
        in_specs=[
            pl.BlockSpec((TM, D_MODEL), lambda i: (i, 0)),
            pl.BlockSpec((N_HEADS, TM_CHUNKS, width), lambda i: (0, i, 0)),
            pl.BlockSpec((CHUNK, TM_CHUNKS, SSM_WIDTH), lambda i: (0, i, 0)),
            _resident((SSM_WIDTH, SSM_WIDTH)),
            _resident((1, SSM_WIDTH)),
            _resident((D_MODEL, D_MODEL)),
            _resident((1, D_MODEL)),
            _resident((TM, TM)),
        ],
        out_specs=[pl.BlockSpec((TM, D_MODEL), lambda i: (i, 0))] * 2,
        out_shape=[jax.ShapeDtypeStruct((s, D_MODEL), F32),
                   jax.ShapeDtypeStruct((s, D_MODEL), BF16)],
        compiler_params=_cparams(("parallel",)),
        name="outproj",
    )(x, attn, y, w_glu, b_glu, w_out, g2, unperm)


TM_FFN = 1024
TF_FFN = 512
N_F = D_FF // TF_FFN
LOAD_AT = N_F // 2


def _ffn_kernel(n_tiles, hn_ref, h_hbm, wg_ref, wu_ref, wd_ref, g_ref, o_hbm,
                acc_ref, act_a, act_b, sem_in, sem_out):
    k = pl.program_id(0)
    f_prev = lax.rem(k + (N_F - 1), N_F)
    tile = jnp.maximum(k - 1, 0) // N_F
    slot = lax.rem(tile, 2)

    def rows(j):
        return pl.ds(pl.multiple_of(j * TM_FFN, TM_FFN), TM_FFN)

    def copy_in(j, s):
        return pltpu.make_async_copy(h_hbm.at[rows(j), :], acc_ref.at[s], sem_in.at[s])

    def copy_out(j, s):
        return pltpu.make_async_copy(acc_ref.at[s], o_hbm.at[rows(j), :], sem_out.at[s])

    @pl.when(k == 0)
    def _():
        act_b[...] = jnp.zeros_like(act_b)
        acc_ref[0] = jnp.zeros((TM_FFN, D_MODEL), F32)

    @pl.when(f_prev == 0)
    def _():
        copy_in(tile, slot).wait()

    def step(act_prev, act_next):
        part = jnp.dot(act_prev[...], wd_ref[...].astype(BF16), preferred_element_type=F32)
        hn = hn_ref[...]
        gate = jnp.dot(hn, wg_ref[...].astype(BF16), preferred_element_type=F32)
        up = jnp.dot(hn, wu_ref[...].astype(BF16), preferred_element_type=F32)
        act_next[...] = (jax.nn.silu(gate) * up).astype(BF16)
        acc_ref[slot] = acc_ref[slot] + part

    @pl.when(lax.rem(k, 2) == 0)
    def _():
        step(act_b, act_a)

    @pl.when(lax.rem(k, 2) == 1)
    def _():
        step(act_a, act_b)

    @pl.when(k == 0)
    def _():
        copy_in(0, 0).start()

    @pl.when((f_prev == LOAD_AT) & (tile >= 1))
    def _():
        copy_out(tile - 1, 1 - slot).wait()

    @pl.when((f_prev == LOAD_AT) & (tile + 1 < n_tiles))
    def _():
        copy_in(tile + 1, 1 - slot).start()

    @pl.when((f_prev == N_F - 1) & (k > 0))
    def _():
        h = acc_ref[slot]
        ms = jnp.mean(h * h, axis=-1, keepdims=True)
        acc_ref[slot] = h * lax.rsqrt(ms + RMS_EPS) * g_ref[...]
        copy_out(tile, slot).start()

    @pl.when(k == pl.num_programs(0) - 1)
    def _():
        copy_out(tile, slot).wait()


def _ffn(hn, h, w_gate, w_up, w_down, g):
    s = h.shape[0]
    n_i = s // TM_FFN
    return pl.pallas_call(
        functools.partial(_ffn_kernel, n_i),
        grid=(n_i * N_F + 1,),
        in_specs=[
            pl.BlockSpec((TM_FFN, D_MODEL), lambda k: (jnp.minimum(k // N_F, n_i - 1), 0)),
            pl.BlockSpec(memory_space=pl.ANY),
            pl.BlockSpec((D_MODEL, TF_FFN), lambda k: (0, k % N_F)),
            pl.BlockSpec((D_MODEL, TF_FFN), lambda k: (0, k % N_F)),
            pl.BlockSpec((TF_FFN, D_MODEL), lambda k: (jnp.maximum(k - 1, 0) % N_F, 0)),
            _resident((1, D_MODEL)),
        ],
        out_specs=pl.BlockSpec(memory_space=pl.ANY),
        out_shape=jax.ShapeDtypeStruct((s, D_MODEL), F32),
        scratch_shapes=[pltpu.VMEM((2, TM_FFN, D_MODEL), F32),
                        pltpu.VMEM((TM_FFN, TF_FFN), BF16),
                        pltpu.VMEM((TM_FFN, TF_FFN), BF16),
                        pltpu.SemaphoreType.DMA((2,)),
                        pltpu.SemaphoreType.DMA((2,))],
        compiler_params=_cparams(("arbitrary",)),
        name="ffn",
    )(hn, h, w_gate, w_up, w_down, g)


def kernel(x, norm1_g, w_in, a_re, a_im, log_dt, b_re, b_im, c_re, c_im, d_skip, w_glu, b_glu,
           w_out, norm2_g, w_gate, w_up, w_down, final_g):
    b, s, _ = x.shape
    assert b == 1 and s % SUPER == 0 and w_in.shape[0] == 1
    x2 = x[0]
    qkv, u = _inproj(x2, norm1_g[0][None, :], w_in[0].astype(BF16), _rope_tables(s))
    attn = _attention(qkv)
    cols, bd, sc = _s5_weights(a_re[0], a_im[0], log_dt[0], b_re[0], b_im[0], c_re[0],
                               c_im[0], d_skip[0])
    y = _s5_core(u, cols, bd, sc)
    h, hn = _outproj(x2, attn, y, w_glu[0].astype(BF16), b_glu[0][None, :].astype(F32),
                     w_out[0].astype(BF16), norm2_g[0][None, :])
    out = _ffn(hn, h, w_gate[0], w_up[0], w_down[0], final_g[None, :])
    return out[None]
```

```python
import functools

import numpy as np
import jax
import jax.numpy as jnp
from jax import lax
from jax.experimental import pallas as pl
from jax.experimental.pallas import tpu as pltpu

F32 = jnp.float32
BF16 = jnp.bfloat16
HI = lax.Precision.HIGHEST

D_MODEL = 2048
ATTN_WIDTH = 1024
SSM_WIDTH = 1024
HEAD_DIM = 128
N_HEADS = ATTN_WIDTH // HEAD_DIM
ROT_DIM = HEAD_DIM // 4
ROPE_THETA = 500000.0
BAND = 128
MAX_DIL = 16
SUPER = BAND * MAX_DIL
SSM_GROUP = 16
N_GROUPS = SSM_WIDTH // SSM_GROUP
SSM_STATE = 64
CHUNK = 16
D_FF = 5632
IN_WIDTH = 3 * ATTN_WIDTH + SSM_WIDTH
RMS_EPS = 1e-6
LANES = 128
SUBLANES = 8

TM = 512
TM_CHUNKS = TM // CHUNK

VMEM_LIMIT = 58 * 1024 * 1024


def _cparams(sem):
    return pltpu.CompilerParams(dimension_semantics=sem, vmem_limit_bytes=VMEM_LIMIT)


def _resident(shape):
    zeros = (0,) * len(shape)
    return pl.BlockSpec(shape, lambda *_: zeros, pipeline_mode=pl.Buffered(1))


PERM = CHUNK * CHUNK
N_PERM = TM // PERM


def _piece(blk, t):
    start = blk * PERM + t * CHUNK
    return slice(start, start + CHUNK)


def _chunks(blk):
    return slice(blk * CHUNK, (blk + 1) * CHUNK)


def _tile_positions():
    rho = np.arange(TM)
    r = rho % PERM
    return (rho // PERM) * PERM + CHUNK * (r % CHUNK) + r // CHUNK


def _lane(r):
    return slice(r * LANES, (r + 1) * LANES)


TN_IN = 512
HEADS_PER_BLK = TN_IN // HEAD_DIM
Q_SCALE = float(HEAD_DIM ** -0.5 * np.log2(np.e))


def _inproj_kernel(x_ref, g_ref, w_ref, rb_ref, ro_ref, rs_ref, perm_ref, qkv_ref, u_ref, hn_ref):
    x = x_ref[...]
    ms = jnp.mean(x * x, axis=-1, keepdims=True)
    hn = (x * lax.rsqrt(ms + RMS_EPS) * g_ref[...]).astype(BF16)
    for blk in range(N_PERM):
        rows = slice(blk * PERM, (blk + 1) * PERM)
        hn_ref[rows, :] = jnp.dot(perm_ref[...], hn[rows, :],
                                  preferred_element_type=F32).astype(BF16)

    cb, sb = rb_ref[0:1, :], rb_ref[1:2, :]
    co, so = ro_ref[0], ro_ref[1]
    cos, sin = cb * co - sb * so, sb * co + cb * so
    sin_hi, sin_lo = sin * rs_ref[0:1, :], sin * rs_ref[1:2, :]
    for j in range(IN_WIDTH // TN_IN):
        acc = jnp.dot(hn_ref[...], w_ref[:, j * TN_IN:(j + 1) * TN_IN],
                      preferred_element_type=F32)
        col = j * TN_IN
        if col >= 3 * ATTN_WIDTH:
            ucols = slice(col - 3 * ATTN_WIDTH, col - 3 * ATTN_WIDTH + TN_IN)
            for blk in range(N_PERM):
                for t in range(CHUNK):
                    u_ref[t, _chunks(blk), ucols] = acc[_piece(blk, t), :].astype(BF16)
            continue
        for hh in range(HEADS_PER_BLK):
            r = acc[:, hh * HEAD_DIM:(hh + 1) * HEAD_DIM]
            if col < 2 * ATTN_WIDTH:
                r = (r * cos + pltpu.roll(r, ROT_DIM // 2, 1) * sin_hi
                     + pltpu.roll(r, HEAD_DIM - ROT_DIM // 2, 1) * sin_lo)
            if col < ATTN_WIDTH:
                r = r * Q_SCALE
            r = r.astype(BF16)
            head = j * HEADS_PER_BLK + hh
            for blk in range(N_PERM):
                for t in range(CHUNK):
                    qkv_ref[head, _chunks(blk), _lane(t)] = r[_piece(blk, t), :]


def _rope_tables(s):
    half = ROT_DIM // 2
    freq = np.zeros(HEAD_DIM)
    freq[:ROT_DIM] = np.tile(ROPE_THETA ** (-np.arange(0, ROT_DIM, 2) / ROT_DIM), 2)
    base = (np.arange(s // TM) * TM)[:, None] * freq[None, :]
    off = _tile_positions()[:, None] * freq[None, :]
    signs = np.zeros((2, HEAD_DIM))
    signs[0, half:ROT_DIM] = 1.0
    signs[1, :half] = -1.0
    as_f32 = lambda a: jnp.asarray(a.astype(np.float32))
    return (as_f32(np.stack([np.cos(base), np.sin(base)], axis=1)),
            as_f32(np.stack([np.cos(off), np.sin(off)])), as_f32(signs))


def _block_permutation():
    pos = _tile_positions()[:PERM]
    return (pos[:, None] == np.arange(PERM)[None, :]).astype(np.float32)


def _inproj(x, g, w_bf16, rope):
    s = x.shape[0]
    width = MAX_DIL * HEAD_DIM
    perm = jnp.asarray(_block_permutation(), BF16)
    rope_base, rope_off, rope_signs = rope
    return pl.pallas_call(
        _inproj_kernel,
        grid=(s // TM,),
        in_specs=[
            pl.BlockSpec((TM, D_MODEL), lambda i: (i, 0)),
            _resident((1, D_MODEL)),
            _resident((D_MODEL, IN_WIDTH)),
            pl.BlockSpec((None, 2, HEAD_DIM), lambda i: (i, 0, 0)),
            _resident(rope_off.shape),
            _resident(rope_signs.shape),
            _resident((PERM, PERM)),
        ],
        out_specs=[
            pl.BlockSpec((3 * N_HEADS, TM_CHUNKS, width), lambda i: (0, i, 0)),
            pl.BlockSpec((CHUNK, TM_CHUNKS, SSM_WIDTH), lambda i: (0, i, 0)),
        ],
        out_shape=[
            jax.ShapeDtypeStruct((3 * N_HEADS, s // MAX_DIL, width), BF16),
            jax.ShapeDtypeStruct((CHUNK, s // CHUNK, SSM_WIDTH), BF16),
        ],
        scratch_shapes=[pltpu.VMEM((TM, D_MODEL), BF16)],
        compiler_params=_cparams(("parallel",)),
        name="inproj",
    )(x, g, w_bf16, rope_base, rope_off, rope_signs, perm)


def _band_bias(tile, perm_mod, perm_mul):
    rho = np.arange(tile)
    lat = perm_mul * (rho % perm_mod) + rho // perm_mod
    jq = lat[:, None]
    jk = np.concatenate([lat - tile, lat])[None, :]
    dist = jq - jk
    valid = (dist >= 0) & (dist <= BAND)
    normal = np.where(valid, 0.0, -np.inf).astype(np.float32)
    first = np.where(valid & (jk >= 0), 0.0, -np.inf).astype(np.float32)
    return np.stack([normal, first])


def _attn_tile(q, k, v, bias, old):
    n = k.shape[0]
    s = lax.dot_general(q, k, (((1,), (1,)), ((), ())), preferred_element_type=F32) + bias
    mt = jnp.max(s, axis=-1, keepdims=True)
    v1 = jnp.concatenate([v, jnp.ones((n, LANES), BF16)], axis=1)
    if old is None:
        m_new = jnp.broadcast_to(mt, (q.shape[0], LANES))
    else:
        acc_o, m_o, l_o = old
        m_new = jnp.maximum(m_o, mt)
    p = jnp.exp2(s - jnp.concatenate([m_new] * (n // LANES), axis=1))
    pv = jnp.dot(p.astype(BF16), v1, preferred_element_type=F32)
    o, l = pv[:, :HEAD_DIM], pv[:, HEAD_DIM:]
    if old is not None:
        alpha = jnp.exp2(m_o - m_new)
        o = alpha * acc_o + o
        l = alpha * l_o + l
    return o, m_new, l


def _attn_kernel(q_ref, kp_ref, kc_ref, vp_ref, vc_ref, b16_ref, b4_ref, b1_ref,
                 o_ref, acc_ref, m_ref, l_ref):
    first = jnp.where(pl.program_id(1) == 0, 1, 0)

    bias = b16_ref[first]
    for r in range(MAX_DIL):
        k = jnp.concatenate([kp_ref[:, _lane(r)], kc_ref[:, _lane(r)]], axis=0)
        v = jnp.concatenate([vp_ref[:, _lane(r)], vc_ref[:, _lane(r)]], axis=0)
        o, m, l = _attn_tile(q_ref[:, _lane(r)], k, v, bias, None)
        acc_ref[:, _lane(r)] = o
        m_ref[:, _lane(r)] = m
        l_ref[:, _lane(r)] = l

    def run_pattern(dil, rows, final):
        n_c = MAX_DIL // dil
        n_b = BAND // rows
        b_ref = b4_ref if dil == 4 else b1_ref
        for b in range(n_b):
            bias = b_ref[first] if b == 0 else b_ref[0]
            cur = slice(b * rows, (b + 1) * rows)
            prev = slice((b - 1) * rows, b * rows) if b > 0 else slice(BAND - rows, BAND)
            for r in range(dil):
                blocks = [_lane(r + dil * c) for c in range(n_c)]

                def gather(ref, rsl):
                    return jnp.concatenate([ref[rsl, bl] for bl in blocks], axis=0)

                q = gather(q_ref, cur)
                k = jnp.concatenate(
                    [gather(kc_ref if b > 0 else kp_ref, prev), gather(kc_ref, cur)], axis=0)
                v = jnp.concatenate(
                    [gather(vc_ref if b > 0 else vp_ref, prev), gather(vc_ref, cur)], axis=0)
                old = (gather(acc_ref, cur), gather(m_ref, cur), gather(l_ref, cur))
                o, m, l = _attn_tile(q, k, v, bias, old)
                for c, bl in enumerate(blocks):
                    piece = slice(c * rows, (c + 1) * rows)
                    if final:
                        o_ref[cur, bl] = (o[piece] / l[piece]).astype(o_ref.dtype)
                    else:
                        acc_ref[cur, bl] = o[piece]
                        m_ref[cur, bl] = m[piece]
                        l_ref[cur, bl] = l[piece]

    run_pattern(4, 32, False)
    run_pattern(1, 16, True)


def _attention(qkv):
    rows, width = qkv.shape[1:]
    b16 = jnp.asarray(_band_bias(BAND, BAND, 1))
    b4 = jnp.asarray(_band_bias(BAND, BAND // 4, 4))
    b1 = jnp.asarray(_band_bias(2 * BAND, MAX_DIL, MAX_DIL))
    blk = (None, BAND, width)

    def spec(base, prev):
        if prev:
            return pl.BlockSpec(blk, lambda h, i: (base + h, jnp.maximum(i - 1, 0), 0))
        return pl.BlockSpec(blk, lambda h, i: (base + h, i, 0))

    return pl.pallas_call(
        _attn_kernel,
        grid=(N_HEADS, rows // BAND),
        in_specs=[spec(0, False), spec(N_HEADS, True), spec(N_HEADS, False),
                  spec(2 * N_HEADS, True), spec(2 * N_HEADS, False),
                  _resident(b16.shape), _resident(b4.shape), _resident(b1.shape)],
        out_specs=pl.BlockSpec(blk, lambda h, i: (h, i, 0)),
        out_shape=jax.ShapeDtypeStruct((N_HEADS, rows, width), BF16),
        scratch_shapes=[pltpu.VMEM((BAND, width), F32)] * 3,
        compiler_params=_cparams(("parallel", "parallel")),
        name="dilated_attn",
    )(qkv, qkv, qkv, qkv, qkv, b16, b4, b1)


GROUPS_PER_TILE = LANES // SSM_GROUP
N_TILES = N_GROUPS // GROUPS_PER_TILE
STATE_W = GROUPS_PER_TILE * SSM_STATE
FOLD_W = CHUNK * LANES
MXU_W = 256


def _shift_down(x, k, row):
    return jnp.where(row >= k, pltpu.roll(x, k, 0), 0.0)


def _s5_kernel(u_ref, wer_ref, wei_ref, wcr_ref, wci_ref, mask_ref, bd_ref, sc_ref, y_ref,
               u3_ref, wt_ref, we_ref, wct_ref, h_ref):
    n_rows = u_ref.shape[1]
    for t in range(CHUNK):
        u3_ref[:, _lane(t)] = u_ref[t]
    mask = mask_ref[...]
    reps = STATE_W // LANES
    for dst, re_ref, im_ref in ((we_ref, wer_ref, wei_ref), (wct_ref, wcr_ref, wci_ref)):
        dst[:, 0:STATE_W] = jnp.concatenate([re_ref[...]] * reps, axis=1) * mask
        dst[:, STATE_W:2 * STATE_W] = jnp.concatenate([im_ref[...]] * reps, axis=1) * mask
    zero_blk = jnp.zeros((LANES, LANES), BF16)
    for tp in range(CHUNK):
        for t in range(tp + 1):
            wt_ref[_lane(t), _lane(tp)] = bd_ref[tp - t]
        if tp % 2 == 0:
            wt_ref[_lane(tp + 1), _lane(tp)] = zero_blk

    h_ref[...] = jnp.dot(u3_ref[...], we_ref[...], preferred_element_type=F32)

    row = lax.broadcasted_iota(jnp.int32, (SUBLANES, STATE_W), 0)

    def block(b, carry):
        cr, ci = carry
        r0 = pl.multiple_of(b * SUBLANES, SUBLANES)
        xr = h_ref[pl.ds(r0, SUBLANES), 0:STATE_W]
        xi = h_ref[pl.ds(r0, SUBLANES), STATE_W:2 * STATE_W]
        for i in range(3):
            kr = sc_ref[16 + 2 * i:17 + 2 * i, :]
            ki = sc_ref[17 + 2 * i:18 + 2 * i, :]
            sr, si = _shift_down(xr, 1 << i, row), _shift_down(xi, 1 << i, row)
            xr, xi = xr + (kr * sr - ki * si), xi + (kr * si + ki * sr)
        pr, pi = sc_ref[0:8, :], sc_ref[8:16, :]
        hr = xr + (pr * cr - pi * ci)
        hi = xi + (pr * ci + pi * cr)
        h_ref[pl.ds(r0, SUBLANES), 0:STATE_W] = jnp.where(row >= 1, pltpu.roll(hr, 1, 0), cr)
        h_ref[pl.ds(r0, SUBLANES), STATE_W:2 * STATE_W] = jnp.where(
            row >= 1, pltpu.roll(hi, 1, 0), ci)
        return hr[SUBLANES - 1:SUBLANES, :], hi[SUBLANES - 1:SUBLANES, :]

    zero = jnp.zeros((1, STATE_W), F32)
    lax.fori_loop(0, n_rows // SUBLANES, block, (zero, zero))

    hin = h_ref[...].astype(BF16)
    for ct in range(FOLD_W // MXU_W):
        cols = slice(ct * MXU_W, (ct + 1) * MXU_W)
        kk = (ct + 1) * MXU_W
        y = (jnp.dot(u3_ref[:, :kk], wt_ref[:kk, cols], preferred_element_type=F32)
             + lax.dot_general(hin, wct_ref[cols, :], (((1,), (1,)), ((), ())),
                               preferred_element_type=F32))
        y_ref[2 * ct] = y[:, :LANES].astype(y_ref.dtype)
        y_ref[2 * ct + 1] = y[:, LANES:].astype(y_ref.dtype)


def _s5_weights(a_re, a_im, log_dt, b_re, b_im, c_re, c_im, d_skip):
    g, n, p = N_GROUPS, SSM_STATE, SSM_GROUP
    nt, gl = N_TILES, GROUPS_PER_TILE
    ar, ai = a_re.astype(F32), a_im.astype(F32)
    dt = jnp.exp(log_dt.astype(F32))[:, None]

    def apow(ks):
        k = jnp.asarray(ks, F32)[:, None, None]
        mag, ph = jnp.exp(ar * dt * k), ai * dt * k
        return mag * jnp.cos(ph), mag * jnp.sin(ph)

    pwr, pwi = apow(np.arange(CHUNK + 1))
    abr, abi = pwr[1], pwi[1]
    nr, ni, den = abr - 1.0, abi, ar * ar + ai * ai
    fr, fi = (nr * ar + ni * ai) / den, (ni * ar - nr * ai) / den
    bre = b_re.astype(F32).transpose(0, 2, 1)
    bim = b_im.astype(F32).transpose(0, 2, 1)
    bbr = fr[:, None, :] * bre - fi[:, None, :] * bim
    bbi = fr[:, None, :] * bim + fi[:, None, :] * bre
    cre, cim = c_re.astype(F32), c_im.astype(F32)

    mr = pwr[:CHUNK, :, None, :] * bbr[None] - pwi[:CHUNK, :, None, :] * bbi[None]
    mi = pwr[:CHUNK, :, None, :] * bbi[None] + pwi[:CHUNK, :, None, :] * bbr[None]
    kern = (jnp.einsum('gpn,tgqn->tgpq', cre, mr, precision=HI)
            - jnp.einsum('gpn,tgqn->tgpq', cim, mi, precision=HI))
    kern = kern.at[0].add(jax.vmap(jnp.diag)(d_skip.astype(F32)))
    same_group = np.kron(np.eye(gl, dtype=np.float32), np.ones((p, p), np.float32))
    bd = kern.reshape(CHUNK, nt, gl, p, p).transpose(1, 0, 2, 4, 3).reshape(nt, CHUNK, LANES, p)
    bd = jnp.tile(bd, (1, 1, 1, gl)) * same_group

    def dup(z):
        return jnp.concatenate([z, z], axis=-1)

    def columns(z):
        z = z.reshape(CHUNK, nt, gl, p, LANES).transpose(1, 0, 2, 3, 4)
        return z.reshape(nt, FOLD_W, LANES).astype(BF16)

    er, ei = dup(pwr[CHUNK - 1::-1])[:, :, None, :], dup(pwi[CHUNK - 1::-1])[:, :, None, :]
    br, bi = dup(bbr)[None], dup(bbi)[None]
    cr, ci = dup(cre)[None], dup(cim)[None]
    ar1, ai1 = dup(pwr[1:])[:, :, None, :], dup(pwi[1:])[:, :, None, :]
    cols = [columns(er * br - ei * bi), columns(er * bi + ei * br),
            columns(cr * ar1 - ci * ai1), columns(-(cr * ai1 + ci * ar1))]

    def tile_lanes(z):
        return z.reshape(z.shape[0], nt, gl * n).transpose(1, 0, 2)

    cyr, cyi = apow(CHUNK * (np.arange(SUBLANES) + 1))
    str_, sti = apow(CHUNK * (1 << np.arange(3)))
    step_rows = jnp.stack([str_, sti], axis=1).reshape(6, g, n)
    sc = jnp.concatenate([tile_lanes(cyr), tile_lanes(cyi), tile_lanes(step_rows),
                          jnp.zeros((nt, 2, STATE_W), F32)], axis=1)
    return cols, bd.astype(BF16), sc


def _group_mask():
    row_g = (np.arange(FOLD_W) // SSM_GROUP) % GROUPS_PER_TILE
    col_h = np.arange(STATE_W) // SSM_STATE
    return (row_g[:, None] == col_h[None, :]).astype(np.float32)


def _s5_core(u, cols, bd, sc):
    _, rows, _ = u.shape
    blk = pl.BlockSpec((CHUNK, rows, LANES), lambda j: (0, 0, j))
    col_spec = pl.BlockSpec((None, FOLD_W, LANES), lambda j: (j, 0, 0))
    mask = jnp.asarray(_group_mask(), BF16)
    return pl.pallas_call(
        _s5_kernel,
        grid=(N_TILES,),
        in_specs=[
            blk, col_spec, col_spec, col_spec, col_spec,
            _resident(mask.shape),
            pl.BlockSpec((None, CHUNK, LANES, LANES), lambda j: (j, 0, 0, 0)),
            pl.BlockSpec((None, 3 * SUBLANES, STATE_W), lambda j: (j, 0, 0)),
        ],
        out_specs=blk,
        out_shape=jax.ShapeDtypeStruct(u.shape, BF16),
        scratch_shapes=[pltpu.VMEM((rows, FOLD_W), BF16),
                        pltpu.VMEM((FOLD_W, FOLD_W), BF16),
                        pltpu.VMEM((FOLD_W, 2 * STATE_W), BF16),
                        pltpu.VMEM((FOLD_W, 2 * STATE_W), BF16),
                        pltpu.VMEM((rows, 2 * STATE_W), F32)],
        compiler_params=_cparams(("parallel",)),
        name="s5_scan",
    )(u, *cols, mask, bd, sc)


def _outproj_kernel(x_ref, a_ref, y_ref, wglu_ref, bglu_ref, wout_ref, g_ref, unperm_ref,
                    h_ref, hn_ref):
    pieces = [(blk, t) for blk in range(N_PERM) for t in range(CHUNK)]
    y = jnp.concatenate([y_ref[t, _chunks(blk), :] for blk, t in pieces], axis=0)
    y = jax.nn.gelu(y.astype(F32))
    gate = jax.nn.sigmoid(
        jnp.dot(y.astype(BF16), wglu_ref[...], preferred_element_type=F32) + bglu_ref[...])
    ssm = (y * gate).astype(BF16)
    attn = jnp.concatenate(
        [jnp.concatenate([a_ref[hh, _chunks(blk), _lane(t)] for blk, t in pieces], axis=0)
         for hh in range(N_HEADS)], axis=1)
    mix = jnp.concatenate([attn, ssm], axis=1)
    mix = jnp.concatenate(
        [jnp.dot(unperm_ref[...], mix[blk * PERM:(blk + 1) * PERM, :],
                 preferred_element_type=F32).astype(BF16) for blk in range(N_PERM)], axis=0)
    h = x_ref[...] + jnp.dot(mix, wout_ref[...], preferred_element_type=F32)
    h_ref[...] = h
    ms = jnp.mean(h * h, axis=-1, keepdims=True)
    hn_ref[...] = (h * lax.rsqrt(ms + RMS_EPS) * g_ref[...]).astype(BF16)


def _outproj(x, attn, y, w_glu, b_glu, w_out, g2):
    s = x.shape[0]
    width = MAX_DIL * HEAD_DIM
    unperm = jnp.asarray(_block_permutation(), BF16)
    return pl.pallas_call(
        _outproj_kernel,
        grid=(s // TM,),
        in_specs=[
            pl.BlockSpec((TM, D_MODEL), lambda i: (i, 0)),
            pl.BlockSpec((N_HEADS, TM_CHUNKS, width), lambda i: (0, i, 0)),
            pl.BlockSpec((CHUNK, TM_CHUNKS, SSM_WIDTH), lambda i: (0, i, 0)),
            _resident((SSM_WIDTH, SSM_WIDTH)),
            _resident((1, SSM_WIDTH)),
            _resident((D_MODEL, D_MODEL)),
            _resident((1, D_MODEL)),
            _resident((PERM, PERM)),
        ],
        out_specs=[pl.BlockSpec((TM, D_MODEL), lambda i: (i, 0))] * 2,
        out_shape=[jax.ShapeDtypeStruct((s, D_MODEL), F32),
                   jax.ShapeDtypeStruct((s, D_MODEL), BF16)],
        compiler_params=_cparams(("parallel",)),
        name="outproj",
    )(x, attn, y, w_glu, b_glu, w_out, g2, unperm)


TM_FFN = 1024
TF_FFN = 512
N_F = D_FF // TF_FFN
LOAD_AT = N_F // 2


def _ffn_kernel(n_tiles, hn_ref, h_hbm, wg_ref, wu_ref, wd_ref, g_ref, o_hbm,
                acc_ref, act_a, act_b, sem_in, sem_out):
    k = pl.program_id(0)
    f_prev = lax.rem(k + (N_F - 1), N_F)
    tile = jnp.maximum(k - 1, 0) // N_F
    slot = lax.rem(tile, 2)

    def rows(j):
        return pl.ds(pl.multiple_of(j * TM_FFN, TM_FFN), TM_FFN)

    def copy_in(j, s):
        return pltpu.make_async_copy(h_hbm.at[rows(j), :], acc_ref.at[s], sem_in.at[s])

    def copy_out(j, s):
        return pltpu.make_async_copy(acc_ref.at[s], o_hbm.at[rows(j), :], sem_out.at[s])

    @pl.when(k == 0)
    def _():
        act_b[...] = jnp.zeros_like(act_b)
        acc_ref[0] = jnp.zeros((TM_FFN, D_MODEL), F32)

    @pl.when(f_prev == 0)
    def _():
        copy_in(tile, slot).wait()

    def step(act_prev, act_next):
        part = jnp.dot(act_prev[...], wd_ref[...].astype(BF16), preferred_element_type=F32)
        hn = hn_ref[...]
        gate = jnp.dot(hn, wg_ref[...].astype(BF16), preferred_element_type=F32)
        up = jnp.dot(hn, wu_ref[...].astype(BF16), preferred_element_type=F32)
        act_next[...] = (jax.nn.silu(gate) * up).astype(BF16)
        acc_ref[slot] = acc_ref[slot] + part

    @pl.when(lax.rem(k, 2) == 0)
    def _():
        step(act_b, act_a)

    @pl.when(lax.rem(k, 2) == 1)
    def _():
        step(act_a, act_b)

    @pl.when(k == 0)
    def _():
        copy_in(0, 0).start()

    @pl.when((f_prev == LOAD_AT) & (tile >= 1))
    def _():
        copy_out(tile - 1, 1 - slot).wait()

    @pl.when((f_prev == LOAD_AT) & (tile + 1 < n_tiles))
    def _():
        copy_in(tile + 1, 1 - slot).start()

    @pl.when((f_prev == N_F - 1) & (k > 0))
    def _():
        h = acc_ref[slot]
        ms = jnp.mean(h * h, axis=-1, keepdims=True)
        acc_ref[slot] = h * lax.rsqrt(ms + RMS_EPS) * g_ref[...]
        copy_out(tile, slot).start()

    @pl.when(k == pl.num_programs(0) - 1)
    def _():
        copy_out(tile, slot).wait()


def _ffn(hn, h, w_gate, w_up, w_down, g):
    s = h.shape[0]
    n_i = s // TM_FFN
    return pl.pallas_call(
        functools.partial(_ffn_kernel, n_i),
        grid=(n_i * N_F + 1,),
        in_specs=[
            pl.BlockSpec((TM_FFN, D_MODEL), lambda k: (jnp.minimum(k // N_F, n_i - 1), 0)),
            pl.BlockSpec(memory_space=pl.ANY),
            pl.BlockSpec((D_MODEL, TF_FFN), lambda k: (0, k % N_F)),
            pl.BlockSpec((D_MODEL, TF_FFN), lambda k: (0, k % N_F)),
            pl.BlockSpec((TF_FFN, D_MODEL), lambda k: (jnp.maximum(k - 1, 0) % N_F, 0)),
            _resident((1, D_MODEL)),
        ],
        out_specs=pl.BlockSpec(memory_space=pl.ANY),
        out_shape=jax.ShapeDtypeStruct((s, D_MODEL), F32),
        scratch_shapes=[pltpu.VMEM((2, TM_FFN, D_MODEL), F32),
                        pltpu.VMEM((TM_FFN, TF_FFN), BF16),
                        pltpu.VMEM((TM_FFN, TF_FFN), BF16),
                        pltpu.SemaphoreType.DMA((2,)),
                        pltpu.SemaphoreType.DMA((2,))],
        compiler_params=_cparams(("arbitrary",)),
        name="ffn",
    )(hn, h, w_gate, w_up, w_down, g)


def kernel(x, norm1_g, w_in, a_re, a_im, log_dt, b_re, b_im, c_re, c_im, d_skip, w_glu, b_glu,
           w_out, norm2_g, w_gate, w_up, w_down, final_g):
    b, s, _ = x.shape
    assert b == 1 and s % SUPER == 0 and w_in.shape[0] == 1
    x2 = x[0]
    qkv, u = _inproj(x2, norm1_g[0][None, :], w_in[0].astype(BF16), _rope_tables(s))
    attn = _attention(qkv)
    cols, bd, sc = _s5_weights(a_re[0], a_im[0], log_dt[0], b_re[0], b_im[0], c_re[0],
                               c_im[0], d_skip[0])
    y = _s5_core(u, cols, bd, sc)
    h, hn = _outproj(x2, attn, y, w_glu[0].astype(BF16), b_glu[0][None, :].astype(F32),
                     w_out[0].astype(BF16), norm2_g[0][None, :])
    out = _ffn(hn, h, w_gate[0], w_up[0], w_down[0], final_g[None, :])
    return out[None]
```

```python
import functools

import numpy as np
import jax
import jax.numpy as jnp
from jax import lax
from jax.experimental import pallas as pl
from jax.experimental.pallas import tpu as pltpu

F32 = jnp.float32
BF16 = jnp.bfloat16
HI = lax.Precision.HIGHEST

D_MODEL = 2048
ATTN_WIDTH = 1024
SSM_WIDTH = 1024
HEAD_DIM = 128
N_HEADS = ATTN_WIDTH // HEAD_DIM
ROT_DIM = HEAD_DIM // 4
ROPE_THETA = 500000.0
BAND = 128
MAX_DIL = 16
SUPER = BAND * MAX_DIL
SSM_GROUP = 16
N_GROUPS = SSM_WIDTH // SSM_GROUP
SSM_STATE = 64
CHUNK = 16
D_FF = 5632
IN_WIDTH = 3 * ATTN_WIDTH + SSM_WIDTH
RMS_EPS = 1e-6
LANES = 128
SUBLANES = 8

TM = 512
TM_CHUNKS = TM // CHUNK

VMEM_LIMIT = 58 * 1024 * 1024


def _cparams(sem):
    return pltpu.CompilerParams(dimension_semantics=sem, vmem_limit_bytes=VMEM_LIMIT)


def _resident(shape):
    zeros = (0,) * len(shape)
    return pl.BlockSpec(shape, lambda *_: zeros, pipeline_mode=pl.Buffered(1))


PERM = CHUNK * CHUNK
N_PERM = TM // PERM


def _piece(blk, t):
    start = blk * PERM + t * CHUNK
    return slice(start, start + CHUNK)


def _chunks(blk):
    return slice(blk * CHUNK, (blk + 1) * CHUNK)


def _tile_positions():
    rho = np.arange(TM)
    r = rho % PERM
    return (rho // PERM) * PERM + CHUNK * (r % CHUNK) + r // CHUNK


def _lane(r):
    return slice(r * LANES, (r + 1) * LANES)


TN_IN = 512
HEADS_PER_BLK = TN_IN // HEAD_DIM
Q_SCALE = float(HEAD_DIM ** -0.5 * np.log2(np.e))


def _inproj_kernel(x_ref, g_ref, w_ref, rb_ref, ro_ref, rs_ref, perm_ref, qkv_ref, u_ref, hn_ref):
    x = x_ref[...]
    ms = jnp.mean(x * x, axis=-1, keepdims=True)
    hn = (x * lax.rsqrt(ms + RMS_EPS) * g_ref[...]).astype(BF16)
    for blk in range(N_PERM):
        rows = slice(blk * PERM, (blk + 1) * PERM)
        hn_ref[rows, :] = jnp.dot(perm_ref[...], hn[rows, :],
                                  preferred_element_type=F32).astype(BF16)

    cb, sb = rb_ref[0:1, :], rb_ref[1:2, :]
    co, so = ro_ref[0], ro_ref[1]
    cos, sin = cb * co - sb * so, sb * co + cb * so
    sin_hi, sin_lo = sin * rs_ref[0:1, :], sin * rs_ref[1:2, :]
    for j in range(IN_WIDTH // TN_IN):
        acc = jnp.dot(hn_ref[...], w_ref[:, j * TN_IN:(j + 1) * TN_IN],
                      preferred_element_type=F32)
        col = j * TN_IN
        if col >= 3 * ATTN_WIDTH:
            ucols = slice(col - 3 * ATTN_WIDTH, col - 3 * ATTN_WIDTH + TN_IN)
            for blk in range(N_PERM):
                for t in range(CHUNK):
                    u_ref[t, _chunks(blk), ucols] = acc[_piece(blk, t), :].astype(BF16)
            continue
        for hh in range(HEADS_PER_BLK):
            r = acc[:, hh * HEAD_DIM:(hh + 1) * HEAD_DIM]
            if col < 2 * ATTN_WIDTH:
                r = (r * cos + pltpu.roll(r, ROT_DIM // 2, 1) * sin_hi
                     + pltpu.roll(r, HEAD_DIM - ROT_DIM // 2, 1) * sin_lo)
            if col < ATTN_WIDTH:
                r = r * Q_SCALE
            r = r.astype(BF16)
            head = j * HEADS_PER_BLK + hh
            for blk in range(N_PERM):
                for t in range(CHUNK):
                    qkv_ref[head, _chunks(blk), _lane(t)] = r[_piece(blk, t), :]


def _rope_tables(s):
    half = ROT_DIM // 2
    freq = np.zeros(HEAD_DIM)
    freq[:ROT_DIM] = np.tile(ROPE_THETA ** (-np.arange(0, ROT_DIM, 2) / ROT_DIM), 2)
    base = (np.arange(s // TM) * TM)[:, None] * freq[None, :]
    off = _tile_positions()[:, None] * freq[None, :]
    signs = np.zeros((2, HEAD_DIM))
    signs[0, half:ROT_DIM] = 1.0
    signs[1, :half] = -1.0
    as_f32 = lambda a: jnp.asarray(a.astype(np.float32))
    return (as_f32(np.stack([np.cos(base), np.sin(base)], axis=1)),
            as_f32(np.stack([np.cos(off), np.sin(off)])), as_f32(signs))


def _block_permutation():
    pos = _tile_positions()[:PERM]
    return (pos[:, None] == np.arange(PERM)[None, :]).astype(np.float32)


def _inproj(x, g, w_bf16, rope):
    s = x.shape[0]
    width = MAX_DIL * HEAD_DIM
    perm = jnp.asarray(_block_permutation(), BF16)
    rope_base, rope_off, rope_signs = rope
    return pl.pallas_call(
        _inproj_kernel,
        grid=(s // TM,),
        in_specs=[
            pl.BlockSpec((TM, D_MODEL), lambda i: (i, 0)),
            _resident((1, D_MODEL)),
            _resident((D_MODEL, IN_WIDTH)),
            pl.BlockSpec((None, 2, HEAD_DIM), lambda i: (i, 0, 0)),
            _resident(rope_off.shape),
            _resident(rope_signs.shape),
            _resident((PERM, PERM)),
        ],
        out_specs=[
            pl.BlockSpec((3 * N_HEADS, TM_CHUNKS, width), lambda i: (0, i, 0)),
            pl.BlockSpec((CHUNK, TM_CHUNKS, SSM_WIDTH), lambda i: (0, i, 0)),
        ],
        out_shape=[
            jax.ShapeDtypeStruct((3 * N_HEADS, s // MAX_DIL, width), BF16),
            jax.ShapeDtypeStruct((CHUNK, s // CHUNK, SSM_WIDTH), BF16),
        ],
        scratch_shapes=[pltpu.VMEM((TM, D_MODEL), BF16)],
        compiler_params=_cparams(("parallel",)),
        name="inproj",
    )(x, g, w_bf16, rope_base, rope_off, rope_signs, perm)


def _band_bias(tile, perm_mod, perm_mul):
    rho = np.arange(tile)
    lat = perm_mul * (rho % perm_mod) + rho // perm_mod
    jq = lat[:, None]
    jk = np.concatenate([lat - tile, lat])[None, :]
    dist = jq - jk
    valid = (dist >= 0) & (dist <= BAND)
    normal = np.where(valid, 0.0, -np.inf).astype(np.float32)
    first = np.where(valid & (jk >= 0), 0.0, -np.inf).astype(np.float32)
    return np.stack([normal, first])


def _attn_tile(q, k, v, bias, old):
    n = k.shape[0]
    s = lax.dot_general(q, k, (((1,), (1,)), ((), ())), preferred_element_type=F32) + bias
    mt = jnp.max(s, axis=-1, keepdims=True)
    v1 = jnp.concatenate([v, jnp.ones((n, LANES), BF16)], axis=1)
    if old is None:
        m_new = jnp.broadcast_to(mt, (q.shape[0], LANES))
    else:
        acc_o, m_o, l_o = old
        m_new = jnp.maximum(m_o, mt)
    p = jnp.exp2(s - jnp.concatenate([m_new] * (n // LANES), axis=1))
    pv = jnp.dot(p.astype(BF16), v1, preferred_element_type=F32)
    o, l = pv[:, :HEAD_DIM], pv[:, HEAD_DIM:]
    if old is not None:
        alpha = jnp.exp2(m_o - m_new)
        o = alpha * acc_o + o
        l = alpha * l_o + l
    return o, m_new, l


def _attn_kernel(q_ref, kp_ref, kc_ref, vp_ref, vc_ref, b16_ref, b4_ref, b1_ref,
                 o_ref, acc_ref, m_ref, l_ref):
    first = jnp.where(pl.program_id(1) == 0, 1, 0)

    bias = b16_ref[first]
    for r in range(MAX_DIL):
        k = jnp.concatenate([kp_ref[:, _lane(r)], kc_ref[:, _lane(r)]], axis=0)
        v = jnp.concatenate([vp_ref[:, _lane(r)], vc_ref[:, _lane(r)]], axis=0)
        o, m, l = _attn_tile(q_ref[:, _lane(r)], k, v, bias, None)
        acc_ref[:, _lane(r)] = o
        m_ref[:, _lane(r)] = m
        l_ref[:, _lane(r)] = l

    def run_pattern(dil, rows, final):
        n_c = MAX_DIL // dil
        n_b = BAND // rows
        b_ref = b4_ref if dil == 4 else b1_ref
        for b in range(n_b):
            bias = b_ref[first] if b == 0 else b_ref[0]
            cur = slice(b * rows, (b + 1) * rows)
            prev = slice((b - 1) * rows, b * rows) if b > 0 else slice(BAND - rows, BAND)
            for r in range(dil):
                blocks = [_lane(r + dil * c) for c in range(n_c)]

                def gather(ref, rsl):
                    return jnp.concatenate([ref[rsl, bl] for bl in blocks], axis=0)

                q = gather(q_ref, cur)
                k = jnp.concatenate(
                    [gather(kc_ref if b > 0 else kp_ref, prev), gather(kc_ref, cur)], axis=0)
                v = jnp.concatenate(
                    [gather(vc_ref if b > 0 else vp_ref, prev), gather(vc_ref, cur)], axis=0)
                old = (gather(acc_ref, cur), gather(m_ref, cur), gather(l_ref, cur))
                o, m, l = _attn_tile(q, k, v, bias, old)
                for c, bl in enumerate(blocks):
                    piece = slice(c * rows, (c + 1) * rows)
                    if final:
                        o_ref[cur, bl] = (o[piece] / l[piece]).astype(o_ref.dtype)
                    else:
                        acc_ref[cur, bl] = o[piece]
                        m_ref[cur, bl] = m[piece]
                        l_ref[cur, bl] = l[piece]

    run_pattern(4, 32, False)
    run_pattern(1, 16, True)


def _attention(qkv):
    rows, width = qkv.shape[1:]
    b16 = jnp.asarray(_band_bias(BAND, BAND, 1))
    b4 = jnp.asarray(_band_bias(BAND, BAND // 4, 4))
    b1 = jnp.asarray(_band_bias(2 * BAND, MAX_DIL, MAX_DIL))
    blk = (None, BAND, width)

    def spec(base, prev):
        if prev:
            return pl.BlockSpec(blk, lambda h, i: (base + h, jnp.maximum(i - 1, 0), 0))
        return pl.BlockSpec(blk, lambda h, i: (base + h, i, 0))

    return pl.pallas_call(
        _attn_kernel,
        grid=(N_HEADS, rows // BAND),
        in_specs=[spec(0, False), spec(N_HEADS, True), spec(N_HEADS, False),
                  spec(2 * N_HEADS, True), spec(2 * N_HEADS, False),
                  _resident(b16.shape), _resident(b4.shape), _resident(b1.shape)],
        out_specs=pl.BlockSpec(blk, lambda h, i: (h, i, 0)),
        out_shape=jax.ShapeDtypeStruct((N_HEADS, rows, width), BF16),
        scratch_shapes=[pltpu.VMEM((BAND, width), F32)] * 3,
        compiler_params=_cparams(("parallel", "parallel")),
        name="dilated_attn",
    )(qkv, qkv, qkv, qkv, qkv, b16, b4, b1)


GROUPS_PER_TILE = LANES // SSM_GROUP
N_TILES = N_GROUPS // GROUPS_PER_TILE
STATE_W = GROUPS_PER_TILE * SSM_STATE
FOLD_W = CHUNK * LANES
MXU_W = 256


def _shift_down(x, k, row):
    return jnp.where(row >= k, pltpu.roll(x, k, 0), 0.0)


def _s5_kernel(u_ref, wer_ref, wei_ref, wcr_ref, wci_ref, mask_ref, bd_ref, sc_ref, y_ref,
               u3_ref, wt_ref, we_ref, wct_ref, h_ref):
    n_rows = u_ref.shape[1]
    for t in range(CHUNK):
        u3_ref[:, _lane(t)] = u_ref[t]
    mask = mask_ref[...]
    reps = STATE_W // LANES
    for dst, re_ref, im_ref in ((we_ref, wer_ref, wei_ref), (wct_ref, wcr_ref, wci_ref)):
        re = re_ref[...].reshape(FOLD_W, LANES)
        im = im_ref[...].reshape(FOLD_W, LANES)
        dst[:, 0:STATE_W] = jnp.concatenate([re] * reps, axis=1) * mask
        dst[:, STATE_W:2 * STATE_W] = jnp.concatenate([im] * reps, axis=1) * mask
    zero_blk = jnp.zeros((LANES, LANES), BF16)
    for tp in range(CHUNK):
        for t in range(tp + 1):
            wt_ref[_lane(t), _lane(tp)] = bd_ref[tp - t]
        if tp % 2 == 0:
            wt_ref[_lane(tp + 1), _lane(tp)] = zero_blk

    h_ref[...] = jnp.dot(u3_ref[...], we_ref[...], preferred_element_type=F32)

    row = lax.broadcasted_iota(jnp.int32, (SUBLANES, STATE_W), 0)

    def block(b, carry):
        cr, ci = carry
        r0 = pl.multiple_of(b * SUBLANES, SUBLANES)
        xr = h_ref[pl.ds(r0, SUBLANES), 0:STATE_W]
        xi = h_ref[pl.ds(r0, SUBLANES), STATE_W:2 * STATE_W]
        for i in range(3):
            kr = sc_ref[16 + 2 * i:17 + 2 * i, :]
            ki = sc_ref[17 + 2 * i:18 + 2 * i, :]
            sr, si = _shift_down(xr, 1 << i, row), _shift_down(xi, 1 << i, row)
            xr, xi = xr + (kr * sr - ki * si), xi + (kr * si + ki * sr)
        pr, pi = sc_ref[0:8, :], sc_ref[8:16, :]
        hr = xr + (pr * cr - pi * ci)
        hi = xi + (pr * ci + pi * cr)
        h_ref[pl.ds(r0, SUBLANES), 0:STATE_W] = jnp.where(row >= 1, pltpu.roll(hr, 1, 0), cr)
        h_ref[pl.ds(r0, SUBLANES), STATE_W:2 * STATE_W] = jnp.where(
            row >= 1, pltpu.roll(hi, 1, 0), ci)
        return hr[SUBLANES - 1:SUBLANES, :], hi[SUBLANES - 1:SUBLANES, :]

    zero = jnp.zeros((1, STATE_W), F32)
    lax.fori_loop(0, n_rows // SUBLANES, block, (zero, zero))

    hin = h_ref[...].astype(BF16)
    for ct in range(FOLD_W // MXU_W):
        cols = slice(ct * MXU_W, (ct + 1) * MXU_W)
        kk = (ct + 1) * MXU_W
        y = (jnp.dot(u3_ref[:, :kk], wt_ref[:kk, cols], preferred_element_type=F32)
             + lax.dot_general(hin, wct_ref[cols, :], (((1,), (1,)), ((), ())),
                               preferred_element_type=F32))
        y_ref[2 * ct] = y[:, :LANES].astype(y_ref.dtype)
        y_ref[2 * ct + 1] = y[:, LANES:].astype(y_ref.dtype)


def _s5_weights(a_re, a_im, log_dt, b_re, b_im, c_re, c_im, d_skip):
    g, n, p = N_GROUPS, SSM_STATE, SSM_GROUP
    nt, gl = N_TILES, GROUPS_PER_TILE
    ar, ai = a_re.astype(F32), a_im.astype(F32)
    dt = jnp.exp(log_dt.astype(F32))[:, None]

    def apow(ks):
        k = jnp.asarray(ks, F32)[:, None, None]
        mag, ph = jnp.exp(ar * dt * k), ai * dt * k
        return mag * jnp.cos(ph), mag * jnp.sin(ph)

    pwr, pwi = apow(np.arange(CHUNK + 1))
    abr, abi = pwr[1], pwi[1]
    nr, ni, den = abr - 1.0, abi, ar * ar + ai * ai
    fr, fi = (nr * ar + ni * ai) / den, (ni * ar - nr * ai) / den
    bre = b_re.astype(F32).transpose(0, 2, 1)
    bim = b_im.astype(F32).transpose(0, 2, 1)
    bbr = fr[:, None, :] * bre - fi[:, None, :] * bim
    bbi = fr[:, None, :] * bim + fi[:, None, :] * bre
    cre, cim = c_re.astype(F32), c_im.astype(F32)

    pr0, pi0 = pwr[:CHUNK].transpose(1, 0, 2)[:, :, None, :], pwi[:CHUNK].transpose(1, 0, 2)[:, :, None, :]
    ca0 = jnp.concatenate([cre[:, None] * pr0 - cim[:, None] * pi0,
                           cre[:, None] * pi0 + cim[:, None] * pr0], axis=-1)
    bb2 = jnp.concatenate([bbr, -bbi], axis=-1)
    kern = jnp.einsum('gqk,gak->gqa', bb2, ca0.reshape(g, CHUNK * p, 2 * n), precision=HI)
    kern = kern.reshape(g, p, CHUNK, p)
    kern = kern.at[:, :, 0, :].add(jnp.eye(p, dtype=F32)[None] * d_skip.astype(F32)[:, None, :])
    same_group = np.kron(np.eye(gl, dtype=np.float32), np.ones((p, p), np.float32))
    bd = kern.reshape(nt, gl, p, CHUNK, p).transpose(0, 3, 1, 2, 4).reshape(nt, CHUNK, LANES, p)
    bd = jnp.tile(bd, (1, 1, 1, gl)) * same_group

    def dup(z):
        return jnp.concatenate([z, z], axis=-1)

    def columns(z):
        return z.reshape(CHUNK, nt, LANES, LANES).astype(BF16)

    er, ei = dup(pwr[CHUNK - 1::-1])[:, :, None, :], dup(pwi[CHUNK - 1::-1])[:, :, None, :]
    br, bi = dup(bbr)[None], dup(bbi)[None]
    cr, ci = dup(cre)[None], dup(cim)[None]
    ar1, ai1 = dup(pwr[1:])[:, :, None, :], dup(pwi[1:])[:, :, None, :]
    cols = [columns(er * br - ei * bi), columns(er * bi + ei * br),
            columns(cr * ar1 - ci * ai1), columns(-(cr * ai1 + ci * ar1))]

    def tile_lanes(z):
        return z.reshape(z.shape[0], nt, gl * n).transpose(1, 0, 2)

    cyr, cyi = apow(CHUNK * (np.arange(SUBLANES) + 1))
    str_, sti = apow(CHUNK * (1 << np.arange(3)))
    step_rows = jnp.stack([str_, sti], axis=1).reshape(6, g, n)
    sc = jnp.concatenate([tile_lanes(cyr), tile_lanes(cyi), tile_lanes(step_rows),
                          jnp.zeros((nt, 2, STATE_W), F32)], axis=1)
    return cols, bd.astype(BF16), sc


def _group_mask():
    row_g = (np.arange(FOLD_W) // SSM_GROUP) % GROUPS_PER_TILE
    col_h = np.arange(STATE_W) // SSM_STATE
    return (row_g[:, None] == col_h[None, :]).astype(np.float32)


def _s5_core(u, cols, bd, sc):
    _, rows, _ = u.shape
    blk = pl.BlockSpec((CHUNK, rows, LANES), lambda j: (0, 0, j))
    col_spec = pl.BlockSpec((CHUNK, None, LANES, LANES), lambda j: (0, j, 0, 0))
    mask = jnp.asarray(_group_mask(), BF16)
    return pl.pallas_call(
        _s5_kernel,
        grid=(N_TILES,),
        in_specs=[
            blk, col_spec, col_spec, col_spec, col_spec,
            _resident(mask.shape),
            pl.BlockSpec((None, CHUNK, LANES, LANES), lambda j: (j, 0, 0, 0)),
            pl.BlockSpec((None, 3 * SUBLANES, STATE_W), lambda j: (j, 0, 0)),
        ],
        out_specs=blk,
        out_shape=jax.ShapeDtypeStruct(u.shape, BF16),
        scratch_shapes=[pltpu.VMEM((rows, FOLD_W), BF16),
                        pltpu.VMEM((FOLD_W, FOLD_W), BF16),
                        pltpu.VMEM((FOLD_W, 2 * STATE_W), BF16),
                        pltpu.VMEM((FOLD_W, 2 * STATE_W), BF16),
                        pltpu.VMEM((rows, 2 * STATE_W), F32)],
        compiler_params=_cparams(("parallel",)),
        name="s5_scan",
    )(u, *cols, mask, bd, sc)


def _outproj_kernel(x_ref, a_ref, y_ref, wglu_ref, bglu_ref, wout_ref, g_ref, unperm_ref,
                    h_ref, hn_ref):
    for blk in range(N_PERM):
        rows = slice(blk * PERM, (blk + 1) * PERM)
        y = jnp.concatenate([y_ref[t, _chunks(blk), :] for t in range(CHUNK)], axis=0)
        y = jax.nn.gelu(y.astype(F32))
        gate = jax.nn.sigmoid(
            jnp.dot(y.astype(BF16), wglu_ref[...], preferred_element_type=F32) + bglu_ref[...])
        ssm = (y * gate).astype(BF16)
        attn = jnp.concatenate(
            [jnp.concatenate([a_ref[hh, _chunks(blk), _lane(t)] for t in range(CHUNK)], axis=0)
             for hh in range(N_HEADS)], axis=1)
        mix = jnp.concatenate([attn, ssm], axis=1)
        mix = jnp.dot(unperm_ref[...], mix, preferred_element_type=F32).astype(BF16)
        h = x_ref[rows, :] + jnp.dot(mix, wout_ref[...], preferred_element_type=F32)
        h_ref[rows, :] = h
        ms = jnp.mean(h * h, axis=-1, keepdims=True)
        hn_ref[rows, :] = (h * lax.rsqrt(ms + RMS_EPS) * g_ref[...]).astype(BF16)


def _outproj(x, attn, y, w_glu, b_glu, w_out, g2):
    s = x.shape[0]
    width = MAX_DIL * HEAD_DIM
    unperm = jnp.asarray(_block_permutation(), BF16)
    return pl.pallas_call(
        _outproj_kernel,
        grid=(s // TM,),
        in_specs=[
            pl.BlockSpec((TM, D_MODEL), lambda i: (i, 0)),
            pl.BlockSpec((N_HEADS, TM_CHUNKS, width), lambda i: (0, i, 0)),
            pl.BlockSpec((CHUNK, TM_CHUNKS, SSM_WIDTH), lambda i: (0, i, 0)),
            _resident((SSM_WIDTH, SSM_WIDTH)),
            _resident((1, SSM_WIDTH)),
            _resident((D_MODEL, D_MODEL)),
            _resident((1, D_MODEL)),
            _resident((PERM, PERM)),
        ],
        out_specs=[pl.BlockSpec((TM, D_MODEL), lambda i: (i, 0))] * 2,
        out_shape=[jax.ShapeDtypeStruct((s, D_MODEL), F32),
                   jax.ShapeDtypeStruct((s, D_MODEL), BF16)],
        compiler_params=_cparams(("parallel",)),
        name="outproj",
    )(x, attn, y, w_glu, b_glu, w_out, g2, unperm)


TM_FFN = 1024
TF_FFN = 512
N_F = D_FF // TF_FFN
LOAD_AT = N_F // 2


def _ffn_kernel(n_tiles, hn_ref, h_hbm, wg_ref, wu_ref, wd_ref, g_ref, o_hbm,
                acc_ref, act_a, act_b, sem_in, sem_out):
    k = pl.program_id(0)
    f_prev = lax.rem(k + (N_F - 1), N_F)
    tile = jnp.maximum(k - 1, 0) // N_F
    slot = lax.rem(tile, 2)

    def rows(j):
        return pl.ds(pl.multiple_of(j * TM_FFN, TM_FFN), TM_FFN)

    def copy_in(j, s):
        return pltpu.make_async_copy(h_hbm.at[rows(j), :], acc_ref.at[s], sem_in.at[s])

    def copy_out(j, s):
        return pltpu.make_async_copy(acc_ref.at[s], o_hbm.at[rows(j), :], sem_out.at[s])

    @pl.when(k == 0)
    def _():
        act_b[...] = jnp.zeros_like(act_b)
        acc_ref[0] = jnp.zeros((TM_FFN, D_MODEL), F32)

    @pl.when(f_prev == 0)
    def _():
        copy_in(tile, slot).wait()

    def step(act_prev, act_next):
        part = jnp.dot(act_prev[...], wd_ref[...].astype(BF16), preferred_element_type=F32)
        hn = hn_ref[...]
        gate = jnp.dot(hn, wg_ref[...].astype(BF16), preferred_element_type=F32)
        up = jnp.dot(hn, wu_ref[...].astype(BF16), preferred_element_type=F32)
        act_next[...] = (jax.nn.silu(gate) * up).astype(BF16)
        acc_ref[slot] = acc_ref[slot] + part

    @pl.when(lax.rem(k, 2) == 0)
    def _():
        step(act_b, act_a)

    @pl.when(lax.rem(k, 2) == 1)
    def _():
        step(act_a, act_b)

    @pl.when(k == 0)
    def _():
        copy_in(0, 0).start()

    @pl.when((f_prev == LOAD_AT) & (tile >= 1))
    def _():
        copy_out(tile - 1, 1 - slot).wait()

    @pl.when((f_prev == LOAD_AT) & (tile + 1 < n_tiles))
    def _():
        copy_in(tile + 1, 1 - slot).start()

    @pl.when((f_prev == N_F - 1) & (k > 0))
    def _():
        h = acc_ref[slot]
        ms = jnp.mean(h * h, axis=-1, keepdims=True)
        acc_ref[slot] = h * lax.rsqrt(ms + RMS_EPS) * g_ref[...]
        copy_out(tile, slot).start()

    @pl.when(k == pl.num_programs(0) - 1)
    def _():
        copy_out(tile, slot).wait()


def _ffn(hn, h, w_gate, w_up, w_down, g):
    s = h.shape[0]
    n_i = s // TM_FFN
    return pl.pallas_call(
        functools.partial(_ffn_kernel, n_i),
        grid=(n_i * N_F + 1,),
        in_specs=[
            pl.BlockSpec((TM_FFN, D_MODEL), lambda k: (jnp.minimum(k // N_F, n_i - 1), 0)),
            pl.BlockSpec(memory_space=pl.ANY),
            pl.BlockSpec((D_MODEL, TF_FFN), lambda k: (0, k % N_F)),
            pl.BlockSpec((D_MODEL, TF_FFN), lambda k: (0, k % N_F)),
            pl.BlockSpec((TF_FFN, D_MODEL), lambda k: (jnp.maximum(k - 1, 0) % N_F, 0)),
            _resident((1, D_MODEL)),
        ],
        out_specs=pl.BlockSpec(memory_space=pl.ANY),
        out_shape=jax.ShapeDtypeStruct((s, D_MODEL), F32),
        scratch_shapes=[pltpu.VMEM((2, TM_FFN, D_MODEL), F32),
                        pltpu.VMEM((TM_FFN, TF_FFN), BF16),
                        pltpu.VMEM((TM_FFN, TF_FFN), BF16),
                        pltpu.SemaphoreType.DMA((2,)),
                        pltpu.SemaphoreType.DMA((2,))],
        compiler_params=_cparams(("arbitrary",)),
        name="ffn",
    )(hn, h, w_gate, w_up, w_down, g)


def kernel(x, norm1_g, w_in, a_re, a_im, log_dt, b_re, b_im, c_re, c_im, d_skip, w_glu, b_glu,
           w_out, norm2_g, w_gate, w_up, w_down, final_g):
    b, s, _ = x.shape
    assert b == 1 and s % SUPER == 0 and w_in.shape[0] == 1
    x2 = x[0]
    qkv, u = _inproj(x2, norm1_g[0][None, :], w_in[0].astype(BF16), _rope_tables(s))
    attn = _attention(qkv)
    cols, bd, sc = _s5_weights(a_re[0], a_im[0], log_dt[0], b_re[0], b_im[0], c_re[0],
                               c_im[0], d_skip[0])
    y = _s5_core(u, cols, bd, sc)
    h, hn = _outproj(x2, attn, y, w_glu[0].astype(BF16), b_glu[0][None, :].astype(F32),
                     w_out[0].astype(BF16), norm2_g[0][None, :])
    out = _ffn(hn, h, w_gate[0], w_up[0], w_down[0], final_g[None, :])
    return out[None]
```

```python
import functools

import numpy as np
import jax
import jax.numpy as jnp
from jax import lax
from jax.experimental import pallas as pl
from jax.experimental.pallas import tpu as pltpu

F32 = jnp.float32
BF16 = jnp.bfloat16
HI = lax.Precision.HIGHEST

D_MODEL = 2048
ATTN_WIDTH = 1024
SSM_WIDTH = 1024
HEAD_DIM = 128
N_HEADS = ATTN_WIDTH // HEAD_DIM
ROT_DIM = HEAD_DIM // 4
ROPE_THETA = 500000.0
BAND = 128
MAX_DIL = 16
SUPER = BAND * MAX_DIL
SSM_GROUP = 16
N_GROUPS = SSM_WIDTH // SSM_GROUP
SSM_STATE = 64
CHUNK = 16
D_FF = 5632
IN_WIDTH = 3 * ATTN_WIDTH + SSM_WIDTH
RMS_EPS = 1e-6
LANES = 128
SUBLANES = 8

TM = 512
TM_CHUNKS = TM // CHUNK

VMEM_LIMIT = 58 * 1024 * 1024


def _cparams(sem):
    return pltpu.CompilerParams(dimension_semantics=sem, vmem_limit_bytes=VMEM_LIMIT)


def _resident(shape):
    zeros = (0,) * len(shape)
    return pl.BlockSpec(shape, lambda *_: zeros, pipeline_mode=pl.Buffered(1))


PERM = CHUNK * CHUNK
N_PERM = TM // PERM


def _piece(blk, t):
    start = blk * PERM + t * CHUNK
    return slice(start, start + CHUNK)


def _chunks(blk):
    return slice(blk * CHUNK, (blk + 1) * CHUNK)


def _tile_positions():
    rho = np.arange(TM)
    r = rho % PERM
    return (rho // PERM) * PERM + CHUNK * (r % CHUNK) + r // CHUNK


def _lane(r):
    return slice(r * LANES, (r + 1) * LANES)


SEGS = LANES // SSM_GROUP


def _transpose_segments(src, seg):
    x = list(src)
    d = SEGS // 2
    while d:
        upper = (seg & d) != 0
        for i in range(SEGS):
            if i & d:
                continue
            a, b = x[i], x[i + d]
            x[i] = jnp.where(upper, pltpu.roll(b, d * SSM_GROUP, 1), a)
            x[i + d] = jnp.where(upper, b, pltpu.roll(a, LANES - d * SSM_GROUP, 1))
        d //= 2
    return x


TN_IN = 512
HEADS_PER_BLK = TN_IN // HEAD_DIM
Q_SCALE = float(HEAD_DIM ** -0.5 * np.log2(np.e))


def _inproj_kernel(x_ref, g_ref, w_ref, rb_ref, ro_ref, rs_ref, perm_ref, qkv_ref, u_ref, hn_ref):
    x = x_ref[...]
    ms = jnp.mean(x * x, axis=-1, keepdims=True)
    hn = (x * lax.rsqrt(ms + RMS_EPS) * g_ref[...]).astype(BF16)
    for blk in range(N_PERM):
        rows = slice(blk * PERM, (blk + 1) * PERM)
        hn_ref[rows, :] = jnp.dot(perm_ref[...], hn[rows, :],
                                  preferred_element_type=F32).astype(BF16)

    cb, sb = rb_ref[0:1, :], rb_ref[1:2, :]
    co, so = ro_ref[0], ro_ref[1]
    cos, sin = cb * co - sb * so, sb * co + cb * so
    sin_hi, sin_lo = sin * rs_ref[0:1, :], sin * rs_ref[1:2, :]
    n_blk, n_qkv = IN_WIDTH // TN_IN, 3 * ATTN_WIDTH // TN_IN
    for j in list(range(n_qkv, n_blk)) + list(range(n_qkv)):
        acc = jnp.dot(hn_ref[...], w_ref[:, j * TN_IN:(j + 1) * TN_IN],
                      preferred_element_type=F32)
        col = j * TN_IN
        if col >= 3 * ATTN_WIDTH:
            seg = lax.broadcasted_iota(jnp.int32, (CHUNK, LANES), 1) // SSM_GROUP
            for lt in range(TN_IN // LANES):
                tile = (col - 3 * ATTN_WIDTH) // LANES + lt
                for blk in range(N_PERM):
                    for th in range(CHUNK // SEGS):
                        src = [acc[_piece(blk, th * SEGS + k), _lane(lt)] for k in range(SEGS)]
                        for gl, folded in enumerate(_transpose_segments(src, seg)):
                            u_ref[tile * SEGS + gl, _chunks(blk), _lane(th)] = folded.astype(BF16)
            continue
        for hh in range(HEADS_PER_BLK):
            r = acc[:, hh * HEAD_DIM:(hh + 1) * HEAD_DIM]
            if col < 2 * ATTN_WIDTH:
                r = (r * cos + pltpu.roll(r, ROT_DIM // 2, 1) * sin_hi
                     + pltpu.roll(r, HEAD_DIM - ROT_DIM // 2, 1) * sin_lo)
            if col < ATTN_WIDTH:
                r = r * Q_SCALE
            r = r.astype(BF16)
            head = j * HEADS_PER_BLK + hh
            for blk in range(N_PERM):
                for t in range(CHUNK):
                    qkv_ref[head, _chunks(blk), _lane(t)] = r[_piece(blk, t), :]


def _rope_tables(s):
    half = ROT_DIM // 2
    freq = np.zeros(HEAD_DIM)
    freq[:ROT_DIM] = np.tile(ROPE_THETA ** (-np.arange(0, ROT_DIM, 2) / ROT_DIM), 2)
    base = (np.arange(s // TM) * TM)[:, None] * freq[None, :]
    off = _tile_positions()[:, None] * freq[None, :]
    signs = np.zeros((2, HEAD_DIM))
    signs[0, half:ROT_DIM] = 1.0
    signs[1, :half] = -1.0
    as_f32 = lambda a: jnp.asarray(a.astype(np.float32))
    return (as_f32(np.stack([np.cos(base), np.sin(base)], axis=1)),
            as_f32(np.stack([np.cos(off), np.sin(off)])), as_f32(signs))


def _block_permutation():
    pos = _tile_positions()[:PERM]
    return (pos[:, None] == np.arange(PERM)[None, :]).astype(np.float32)


def _inproj(x, g, w_bf16, rope):
    s = x.shape[0]
    width = MAX_DIL * HEAD_DIM
    perm = jnp.asarray(_block_permutation(), BF16)
    rope_base, rope_off, rope_signs = rope
    return pl.pallas_call(
        _inproj_kernel,
        grid=(s // TM,),
        in_specs=[
            pl.BlockSpec((TM, D_MODEL), lambda i: (i, 0)),
            _resident((1, D_MODEL)),
            _resident((D_MODEL, IN_WIDTH)),
            pl.BlockSpec((None, 2, HEAD_DIM), lambda i: (i, 0, 0)),
            _resident(rope_off.shape),
            _resident(rope_signs.shape),
            _resident((PERM, PERM)),
        ],
        out_specs=[
            pl.BlockSpec((3 * N_HEADS, TM_CHUNKS, width), lambda i: (0, i, 0)),
            pl.BlockSpec((N_GROUPS, TM_CHUNKS, CHUNK * SSM_GROUP), lambda i: (0, i, 0)),
        ],
        out_shape=[
            jax.ShapeDtypeStruct((3 * N_HEADS, s // MAX_DIL, width), BF16),
            jax.ShapeDtypeStruct((N_GROUPS, s // CHUNK, CHUNK * SSM_GROUP), BF16),
        ],
        scratch_shapes=[pltpu.VMEM((TM, D_MODEL), BF16)],
        compiler_params=_cparams(("parallel",)),
        name="inproj",
    )(x, g, w_bf16, rope_base, rope_off, rope_signs, perm)


def _band_bias(tile, perm_mod, perm_mul):
    rho = np.arange(tile)
    lat = perm_mul * (rho % perm_mod) + rho // perm_mod
    jq = lat[:, None]
    jk = np.concatenate([lat - tile, lat])[None, :]
    dist = jq - jk
    valid = (dist >= 0) & (dist <= BAND)
    normal = np.where(valid, 0.0, -np.inf).astype(np.float32)
    first = np.where(valid & (jk >= 0), 0.0, -np.inf).astype(np.float32)
    return np.stack([normal, first])


def _attn_tile(q, k, v, bias, old):
    n = k.shape[0]
    s = lax.dot_general(q, k, (((1,), (1,)), ((), ())), preferred_element_type=F32) + bias
    mt = jnp.max(s, axis=-1, keepdims=True)
    v1 = jnp.concatenate([v, jnp.ones((n, LANES), BF16)], axis=1)
    if old is None:
        m_new = jnp.broadcast_to(mt, (q.shape[0], LANES))
    else:
        acc_o, m_o, l_o = old
        m_new = jnp.maximum(m_o, mt)
    p = jnp.exp2(s - jnp.concatenate([m_new] * (n // LANES), axis=1))
    pv = jnp.dot(p.astype(BF16), v1, preferred_element_type=F32)
    o, l = pv[:, :HEAD_DIM], pv[:, HEAD_DIM:]
    if old is not None:
        alpha = jnp.exp2(m_o - m_new)
        o = alpha * acc_o + o
        l = alpha * l_o + l
    return o, m_new, l


def _attn_kernel(q_ref, kp_ref, kc_ref, vp_ref, vc_ref, b16_ref, b4_ref, b1_ref,
                 o_ref, acc_ref, m_ref, l_ref):
    first = jnp.where(pl.program_id(1) == 0, 1, 0)

    bias = b16_ref[first]
    for r in range(MAX_DIL):
        k = jnp.concatenate([kp_ref[:, _lane(r)], kc_ref[:, _lane(r)]], axis=0)
        v = jnp.concatenate([vp_ref[:, _lane(r)], vc_ref[:, _lane(r)]], axis=0)
        o, m, l = _attn_tile(q_ref[:, _lane(r)], k, v, bias, None)
        acc_ref[:, _lane(r)] = o
        m_ref[:, _lane(r)] = m
        l_ref[:, _lane(r)] = l

    def run_pattern(dil, rows, final):
        n_c = MAX_DIL // dil
        n_b = BAND // rows
        b_ref = b4_ref if dil == 4 else b1_ref
        for b in range(n_b):
            bias = b_ref[first] if b == 0 else b_ref[0]
            cur = slice(b * rows, (b + 1) * rows)
            prev = slice((b - 1) * rows, b * rows) if b > 0 else slice(BAND - rows, BAND)
            for r in range(dil):
                blocks = [_lane(r + dil * c) for c in range(n_c)]

                def gather(ref, rsl):
                    return jnp.concatenate([ref[rsl, bl] for bl in blocks], axis=0)

                q = gather(q_ref, cur)
                k = jnp.concatenate(
                    [gather(kc_ref if b > 0 else kp_ref, prev), gather(kc_ref, cur)], axis=0)
                v = jnp.concatenate(
                    [gather(vc_ref if b > 0 else vp_ref, prev), gather(vc_ref, cur)], axis=0)
                old = (gather(acc_ref, cur), gather(m_ref, cur), gather(l_ref, cur))
                o, m, l = _attn_tile(q, k, v, bias, old)
                for c, bl in enumerate(blocks):
                    piece = slice(c * rows, (c + 1) * rows)
                    if final:
                        o_ref[cur, bl] = (o[piece] / l[piece]).astype(o_ref.dtype)
                    else:
                        acc_ref[cur, bl] = o[piece]
                        m_ref[cur, bl] = m[piece]
                        l_ref[cur, bl] = l[piece]

    run_pattern(4, 32, False)
    run_pattern(1, 16, True)


def _attention(qkv):
    rows, width = qkv.shape[1:]
    b16 = jnp.asarray(_band_bias(BAND, BAND, 1))
    b4 = jnp.asarray(_band_bias(BAND, BAND // 4, 4))
    b1 = jnp.asarray(_band_bias(2 * BAND, MAX_DIL, MAX_DIL))
    blk = (None, BAND, width)

    def spec(base, prev):
        if prev:
            return pl.BlockSpec(blk, lambda h, i: (base + h, jnp.maximum(i - 1, 0), 0))
        return pl.BlockSpec(blk, lambda h, i: (base + h, i, 0))

    return pl.pallas_call(
        _attn_kernel,
        grid=(N_HEADS, rows // BAND),
        in_specs=[spec(0, False), spec(N_HEADS, True), spec(N_HEADS, False),
                  spec(2 * N_HEADS, True), spec(2 * N_HEADS, False),
                  _resident(b16.shape), _resident(b4.shape), _resident(b1.shape)],
        out_specs=pl.BlockSpec(blk, lambda h, i: (h, i, 0)),
        out_shape=jax.ShapeDtypeStruct((N_HEADS, rows, width), BF16),
        scratch_shapes=[pltpu.VMEM((BAND, width), F32)] * 3,
        compiler_params=_cparams(("parallel", "parallel")),
        name="dilated_attn",
    )(qkv, qkv, qkv, qkv, qkv, b16, b4, b1)


GROUPS_PER_TILE = SEGS
N_TILES = N_GROUPS // GROUPS_PER_TILE
PAIRS = GROUPS_PER_TILE // 2
STATE_W = GROUPS_PER_TILE * SSM_STATE
FOLD = CHUNK * SSM_GROUP
PAIR_STATE = 2 * SSM_STATE


def _shift_down(x, k, row):
    return jnp.where(row >= k, pltpu.roll(x, k, 0), 0.0)


def _s5_kernel(u_ref, we_ref, wy_ref, sc_ref, y_ref, h_ref):
    n_rows = u_ref.shape[1]
    for k in range(PAIRS):
        e = jnp.dot(jnp.concatenate([u_ref[2 * k], u_ref[2 * k + 1]], axis=1), we_ref[k],
                    preferred_element_type=F32)
        h_ref[:, _lane(k)] = e[:, :PAIR_STATE]
        h_ref[:, STATE_W + k * PAIR_STATE:STATE_W + (k + 1) * PAIR_STATE] = e[:, PAIR_STATE:]

    row = lax.broadcasted_iota(jnp.int32, (SUBLANES, STATE_W), 0)

    def block(b, carry):
        cr, ci = carry
        r0 = pl.multiple_of(b * SUBLANES, SUBLANES)
        xr = h_ref[pl.ds(r0, SUBLANES), 0:STATE_W]
        xi = h_ref[pl.ds(r0, SUBLANES), STATE_W:2 * STATE_W]
        for i in range(3):
            kr = sc_ref[16 + 2 * i:17 + 2 * i, :]
            ki = sc_ref[17 + 2 * i:18 + 2 * i, :]
            sr, si = _shift_down(xr, 1 << i, row), _shift_down(xi, 1 << i, row)
            xr, xi = xr + (kr * sr - ki * si), xi + (kr * si + ki * sr)
        pr, pi = sc_ref[0:8, :], sc_ref[8:16, :]
        hr = xr + (pr * cr - pi * ci)
        hi = xi + (pr * ci + pi * cr)
        h_ref[pl.ds(r0, SUBLANES), 0:STATE_W] = jnp.where(row >= 1, pltpu.roll(hr, 1, 0), cr)
        h_ref[pl.ds(r0, SUBLANES), STATE_W:2 * STATE_W] = jnp.where(
            row >= 1, pltpu.roll(hi, 1, 0), ci)
        return hr[SUBLANES - 1:SUBLANES, :], hi[SUBLANES - 1:SUBLANES, :]

    zero = jnp.zeros((1, STATE_W), F32)
    lax.fori_loop(0, n_rows // SUBLANES, block, (zero, zero))

    for k in range(PAIRS):
        hin = jnp.concatenate(
            [h_ref[:, _lane(k)],
             h_ref[:, STATE_W + k * PAIR_STATE:STATE_W + (k + 1) * PAIR_STATE]],
            axis=1).astype(BF16)
        for s in range(2):
            g = 2 * k + s
            y = jnp.dot(jnp.concatenate([u_ref[g], hin], axis=1), wy_ref[k, s],
                        preferred_element_type=F32)
            y_ref[g] = y.astype(y_ref.dtype)


def _s5_weights(a_re, a_im, log_dt, b_re, b_im, c_re, c_im, d_skip):
    g, n, p = N_GROUPS, SSM_STATE, SSM_GROUP
    nt, gl = N_TILES, GROUPS_PER_TILE
    ar, ai = a_re.astype(F32), a_im.astype(F32)
    dt = jnp.exp(log_dt.astype(F32))[:, None]

    def apow(ks):
        k = jnp.asarray(ks, F32)[:, None, None]
        mag, ph = jnp.exp(ar * dt * k), ai * dt * k
        return mag * jnp.cos(ph), mag * jnp.sin(ph)

    pwr, pwi = apow(np.arange(CHUNK + 1))
    abr, abi = pwr[1], pwi[1]
    nr, ni, den = abr - 1.0, abi, ar * ar + ai * ai
    fr, fi = (nr * ar + ni * ai) / den, (ni * ar - nr * ai) / den
    bre = b_re.astype(F32).transpose(0, 2, 1)
    bim = b_im.astype(F32).transpose(0, 2, 1)
    bbr = fr[:, None, :] * bre - fi[:, None, :] * bim
    bbi = fr[:, None, :] * bim + fi[:, None, :] * bre
    cre, cim = c_re.astype(F32), c_im.astype(F32)

    pr0, pi0 = pwr[:CHUNK].transpose(1, 0, 2)[:, :, None, :], pwi[:CHUNK].transpose(1, 0, 2)[:, :, None, :]
    ca0 = jnp.concatenate([cre[:, None] * pr0 - cim[:, None] * pi0,
                           cre[:, None] * pi0 + cim[:, None] * pr0], axis=-1)
    bb2 = jnp.concatenate([bbr, -bbi], axis=-1)
    kern = jnp.einsum('gqk,gak->gqa', bb2, ca0.reshape(g, CHUNK * p, 2 * n), precision=HI)
    kern = kern.reshape(g, p, CHUNK, p)
    kern = kern.at[:, :, 0, :].add(jnp.eye(p, dtype=F32)[None] * d_skip.astype(F32)[:, None, :])
    lag = np.arange(CHUNK)[None, :] - np.arange(CHUNK)[:, None]
    toe = jnp.where((lag >= 0)[None, None, :, :, None],
                    kern[:, :, np.clip(lag, 0, CHUNK - 1), :], 0.0)
    toe = toe.transpose(0, 2, 1, 3, 4).reshape(g, FOLD, FOLD)

    er = pwr[CHUNK - 1::-1].transpose(1, 0, 2)[:, :, None, :]
    ei = pwi[CHUNK - 1::-1].transpose(1, 0, 2)[:, :, None, :]
    wbr = (er * bbr[:, None] - ei * bbi[:, None]).reshape(g, FOLD, n)
    wbi = (er * bbi[:, None] + ei * bbr[:, None]).reshape(g, FOLD, n)
    ar1 = pwr[1:].transpose(1, 0, 2)[:, :, None, :]
    ai1 = pwi[1:].transpose(1, 0, 2)[:, :, None, :]
    car = (cre[:, None] * ar1 - cim[:, None] * ai1).reshape(g, FOLD, n).transpose(0, 2, 1)
    cai = (cre[:, None] * ai1 + cim[:, None] * ar1).reshape(g, FOLD, n).transpose(0, 2, 1)

    gp = g // 2
    pair = lambda z: z.reshape((gp, 2) + z.shape[1:])
    z_e, z_c = jnp.zeros((gp, FOLD, n), F32), jnp.zeros((gp, n, FOLD), F32)
    wbr, wbi, car, cai, toe = pair(wbr), pair(wbi), pair(car), pair(cai), pair(toe)
    we = jnp.concatenate([
        jnp.concatenate([wbr[:, 0], z_e, wbi[:, 0], z_e], axis=2),
        jnp.concatenate([z_e, wbr[:, 1], z_e, wbi[:, 1]], axis=2)], axis=1)
    wy = jnp.stack([
        jnp.concatenate([toe[:, 0], car[:, 0], z_c, -cai[:, 0], z_c], axis=1),
        jnp.concatenate([toe[:, 1], z_c, car[:, 1], z_c, -cai[:, 1]], axis=1)],
        axis=1)

    def tile_lanes(z):
        return z.reshape(z.shape[0], nt, gl * n).transpose(1, 0, 2)

    cyr, cyi = apow(CHUNK * (np.arange(SUBLANES) + 1))
    str_, sti = apow(CHUNK * (1 << np.arange(3)))
    step_rows = jnp.stack([str_, sti], axis=1).reshape(6, g, n)
    sc = jnp.concatenate([tile_lanes(cyr), tile_lanes(cyi), tile_lanes(step_rows),
                          jnp.zeros((nt, 2, STATE_W), F32)], axis=1)
    return we.astype(BF16), wy.astype(BF16), sc


def _s5_core(u2, we, wy, sc):
    _, rows, _ = u2.shape
    blk = pl.BlockSpec((GROUPS_PER_TILE, rows, FOLD), lambda j: (j, 0, 0))
    return pl.pallas_call(
        _s5_kernel,
        grid=(N_TILES,),
        in_specs=[
            blk,
            pl.BlockSpec((PAIRS, 2 * FOLD, 2 * PAIR_STATE), lambda j: (j, 0, 0)),
            pl.BlockSpec((PAIRS, 2, FOLD + 2 * PAIR_STATE, FOLD), lambda j: (j, 0, 0, 0)),
            pl.BlockSpec((None, 3 * SUBLANES, STATE_W), lambda j: (j, 0, 0)),
        ],
        out_specs=blk,
        out_shape=jax.ShapeDtypeStruct(u2.shape, BF16),
        scratch_shapes=[pltpu.VMEM((rows, 2 * STATE_W), F32)],
        compiler_params=_cparams(("parallel",)),
        name="s5_scan",
    )(u2, we, wy, sc)


def _outproj_kernel(x_ref, a_ref, y_ref, wglu_ref, bglu_ref, wout_ref, g_ref, unperm_ref,
                    h_ref, hn_ref):
    seg = lax.broadcasted_iota(jnp.int32, (CHUNK, LANES), 1) // SSM_GROUP
    for blk in range(N_PERM):
        rows = slice(blk * PERM, (blk + 1) * PERM)
        y_t = [[] for _ in range(CHUNK)]
        for tile in range(N_GROUPS // SEGS):
            for th in range(CHUNK // SEGS):
                src = [y_ref[tile * SEGS + gl, _chunks(blk), _lane(th)].astype(F32)
                       for gl in range(SEGS)]
                for k, unfolded in enumerate(_transpose_segments(src, seg)):
                    y_t[th * SEGS + k].append(unfolded)
        y = jnp.concatenate([jnp.concatenate(parts, axis=1) for parts in y_t], axis=0)
        y = jax.nn.gelu(y)
        gate = jax.nn.sigmoid(
            jnp.dot(y.astype(BF16), wglu_ref[...], preferred_element_type=F32) + bglu_ref[...])
        ssm = (y * gate).astype(BF16)
        attn = jnp.concatenate(
            [jnp.concatenate([a_ref[hh, _chunks(blk), _lane(t)] for t in range(CHUNK)], axis=0)
             for hh in range(N_HEADS)], axis=1)
        mix = jnp.concatenate([attn, ssm], axis=1)
        mix = jnp.dot(unperm_ref[...], mix, preferred_element_type=F32).astype(BF16)
        h = x_ref[rows, :] + jnp.dot(mix, wout_ref[...], preferred_element_type=F32)
        h_ref[rows, :] = h
        ms = jnp.mean(h * h, axis=-1, keepdims=True)
        hn_ref[rows, :] = (h * lax.rsqrt(ms + RMS_EPS) * g_ref[...]).astype(BF16)


def _outproj(x, attn, y, w_glu, b_glu, w_out, g2):
    s = x.shape[0]
    width = MAX_DIL * HEAD_DIM
    unperm = jnp.asarray(_block_permutation(), BF16)
    return pl.pallas_call(
        _outproj_kernel,
        grid=(s // TM,),
        in_specs=[
            pl.BlockSpec((TM, D_MODEL), lambda i: (i, 0)),
            pl.BlockSpec((N_HEADS, TM_CHUNKS, width), lambda i: (0, i, 0)),
            pl.BlockSpec((N_GROUPS, TM_CHUNKS, FOLD), lambda i: (0, i, 0)),
            _resident((SSM_WIDTH, SSM_WIDTH)),
            _resident((1, SSM_WIDTH)),
            _resident((D_MODEL, D_MODEL)),
            _resident((1, D_MODEL)),
            _resident((PERM, PERM)),
        ],
        out_specs=[pl.BlockSpec((TM, D_MODEL), lambda i: (i, 0))] * 2,
        out_shape=[jax.ShapeDtypeStruct((s, D_MODEL), F32),
                   jax.ShapeDtypeStruct((s, D_MODEL), BF16)],
        compiler_params=_cparams(("parallel",)),
        name="outproj",
    )(x, attn, y, w_glu, b_glu, w_out, g2, unperm)


TM_FFN = 1024
TF_FFN = 512
N_F = D_FF // TF_FFN
LOAD_AT = N_F // 2


def _ffn_kernel(n_tiles, hn_ref, h_hbm, wg_ref, wu_ref, wd_ref, g_ref, o_hbm,
                acc_ref, act_a, act_b, sem_in, sem_out):
    k = pl.program_id(0)
    f_prev = lax.rem(k + (N_F - 1), N_F)
    tile = jnp.maximum(k - 1, 0) // N_F
    slot = lax.rem(tile, 2)

    def rows(j):
        return pl.ds(pl.multiple_of(j * TM_FFN, TM_FFN), TM_FFN)

    def copy_in(j, s):
        return pltpu.make_async_copy(h_hbm.at[rows(j), :], acc_ref.at[s], sem_in.at[s])

    def copy_out(j, s):
        return pltpu.make_async_copy(acc_ref.at[s], o_hbm.at[rows(j), :], sem_out.at[s])

    @pl.when(k == 0)
    def _():
        act_b[...] = jnp.zeros_like(act_b)
        acc_ref[0] = jnp.zeros((TM_FFN, D_MODEL), F32)

    @pl.when(f_prev == 0)
    def _():
        copy_in(tile, slot).wait()

    def step(act_prev, act_next):
        part = jnp.dot(act_prev[...], wd_ref[...].astype(BF16), preferred_element_type=F32)
        hn = hn_ref[...]
        gate = jnp.dot(hn, wg_ref[...].astype(BF16), preferred_element_type=F32)
        up = jnp.dot(hn, wu_ref[...].astype(BF16), preferred_element_type=F32)
        act_next[...] = (jax.nn.silu(gate) * up).astype(BF16)
        acc_ref[slot] = acc_ref[slot] + part

    @pl.when(lax.rem(k, 2) == 0)
    def _():
        step(act_b, act_a)

    @pl.when(lax.rem(k, 2) == 1)
    def _():
        step(act_a, act_b)

    @pl.when(k == 0)
    def _():
        copy_in(0, 0).start()

    @pl.when((f_prev == LOAD_AT) & (tile >= 1))
    def _():
        copy_out(tile - 1, 1 - slot).wait()

    @pl.when((f_prev == LOAD_AT) & (tile + 1 < n_tiles))
    def _():
        copy_in(tile + 1, 1 - slot).start()

    @pl.when((f_prev == N_F - 1) & (k > 0))
    def _():
        h = acc_ref[slot]
        ms = jnp.mean(h * h, axis=-1, keepdims=True)
        acc_ref[slot] = h * lax.rsqrt(ms + RMS_EPS) * g_ref[...]
        copy_out(tile, slot).start()

    @pl.when(k == pl.num_programs(0) - 1)
    def _():
        copy_out(tile, slot).wait()


def _ffn(hn, h, w_gate, w_up, w_down, g):
    s = h.shape[0]
    n_i = s // TM_FFN
    return pl.pallas_call(
        functools.partial(_ffn_kernel, n_i),
        grid=(n_i * N_F + 1,),
        in_specs=[
            pl.BlockSpec((TM_FFN, D_MODEL), lambda k: (jnp.minimum(k // N_F, n_i - 1), 0)),
            pl.BlockSpec(memory_space=pl.ANY),
            pl.BlockSpec((D_MODEL, TF_FFN), lambda k: (0, k % N_F)),
            pl.BlockSpec((D_MODEL, TF_FFN), lambda k: (0, k % N_F)),
            pl.BlockSpec((TF_FFN, D_MODEL), lambda k: (jnp.maximum(k - 1, 0) % N_F, 0)),
            _resident((1, D_MODEL)),
        ],
        out_specs=pl.BlockSpec(memory_space=pl.ANY),
        out_shape=jax.ShapeDtypeStruct((s, D_MODEL), F32),
        scratch_shapes=[pltpu.VMEM((2, TM_FFN, D_MODEL), F32),
                        pltpu.VMEM((TM_FFN, TF_FFN), BF16),
                        pltpu.VMEM((TM_FFN, TF_FFN), BF16),
                        pltpu.SemaphoreType.DMA((2,)),
                        pltpu.SemaphoreType.DMA((2,))],
        compiler_params=_cparams(("arbitrary",)),
        name="ffn",
    )(hn, h, w_gate, w_up, w_down, g)


def kernel(x, norm1_g, w_in, a_re, a_im, log_dt, b_re, b_im, c_re, c_im, d_skip, w_glu, b_glu,
           w_out, norm2_g, w_gate, w_up, w_down, final_g):
    b, s, _ = x.shape
    assert b == 1 and s % SUPER == 0 and w_in.shape[0] == 1
    x2 = x[0]
    qkv, u = _inproj(x2, norm1_g[0][None, :], w_in[0].astype(BF16), _rope_tables(s))
    attn = _attention(qkv)
    we, wy, sc = _s5_weights(a_re[0], a_im[0], log_dt[0], b_re[0], b_im[0], c_re[0],
                             c_im[0], d_skip[0])
    y = _s5_core(u, we, wy, sc)
    h, hn = _outproj(x2, attn, y, w_glu[0].astype(BF16), b_glu[0][None, :].astype(F32),
                     w_out[0].astype(BF16), norm2_g[0][None, :])
    out = _ffn(hn, h, w_gate[0], w_up[0], w_down[0], final_g[None, :])
    return out[None]
```

```python
import functools

import numpy as np
import jax
import jax.numpy as jnp
from jax import lax
from jax.experimental import pallas as pl
from jax.experimental.pallas import tpu as pltpu

F32 = jnp.float32
BF16 = jnp.bfloat16
HI = lax.Precision.HIGHEST

D_MODEL = 2048
ATTN_WIDTH = 1024
SSM_WIDTH = 1024
HEAD_DIM = 128
N_HEADS = ATTN_WIDTH // HEAD_DIM
ROT_DIM = HEAD_DIM // 4
ROPE_THETA = 500000.0
BAND = 128
MAX_DIL = 16
SUPER = BAND * MAX_DIL
SSM_GROUP = 16
N_GROUPS = SSM_WIDTH // SSM_GROUP
SSM_STATE = 64
CHUNK = 16
D_FF = 5632
IN_WIDTH = 3 * ATTN_WIDTH + SSM_WIDTH
RMS_EPS = 1e-6
LANES = 128
SUBLANES = 8

TM = 512
TM_CHUNKS = TM // CHUNK

VMEM_LIMIT = 58 * 1024 * 1024


def _cparams(sem):
    return pltpu.CompilerParams(dimension_semantics=sem, vmem_limit_bytes=VMEM_LIMIT)


def _resident(shape):
    zeros = (0,) * len(shape)
    return pl.BlockSpec(shape, lambda *_: zeros, pipeline_mode=pl.Buffered(1))


PERM = CHUNK * CHUNK
N_PERM = TM // PERM


def _piece(blk, t):
    start = blk * PERM + t * CHUNK
    return slice(start, start + CHUNK)


def _chunks(blk):
    return slice(blk * CHUNK, (blk + 1) * CHUNK)


def _tile_positions():
    rho = np.arange(TM)
    r = rho % PERM
    return (rho // PERM) * PERM + CHUNK * (r % CHUNK) + r // CHUNK


def _lane(r):
    return slice(r * LANES, (r + 1) * LANES)


SEGS = LANES // SSM_GROUP


def _transpose_segments(src, seg):
    x = list(src)
    d = SEGS // 2
    while d:
        upper = (seg & d) != 0
        for i in range(SEGS):
            if i & d:
                continue
            a, b = x[i], x[i + d]
            x[i] = jnp.where(upper, pltpu.roll(b, d * SSM_GROUP, 1), a)
            x[i + d] = jnp.where(upper, b, pltpu.roll(a, LANES - d * SSM_GROUP, 1))
        d //= 2
    return x


TN_IN = 512
HEADS_PER_BLK = TN_IN // HEAD_DIM
Q_SCALE = float(HEAD_DIM ** -0.5 * np.log2(np.e))


def _inproj_kernel(x_ref, g_ref, w_ref, rb_ref, ro_ref, rs_ref, perm_ref, qkv_ref, u_ref, hn_ref):
    x = x_ref[...]
    ms = jnp.mean(x * x, axis=-1, keepdims=True)
    hn = (x * lax.rsqrt(ms + RMS_EPS) * g_ref[...]).astype(BF16)
    for blk in range(N_PERM):
        rows = slice(blk * PERM, (blk + 1) * PERM)
        hn_ref[rows, :] = jnp.dot(perm_ref[...], hn[rows, :],
                                  preferred_element_type=F32).astype(BF16)

    cb, sb = rb_ref[0:1, :], rb_ref[1:2, :]
    co, so = ro_ref[0], ro_ref[1]
    cos, sin = cb * co - sb * so, sb * co + cb * so
    sin_hi, sin_lo = sin * rs_ref[0:1, :], sin * rs_ref[1:2, :]
    n_blk, n_qkv = IN_WIDTH // TN_IN, 3 * ATTN_WIDTH // TN_IN
    for j in list(range(n_qkv, n_blk)) + list(range(n_qkv)):
        acc = jnp.dot(hn_ref[...], w_ref[:, j * TN_IN:(j + 1) * TN_IN],
                      preferred_element_type=F32)
        col = j * TN_IN
        if col >= 3 * ATTN_WIDTH:
            seg = lax.broadcasted_iota(jnp.int32, (CHUNK, LANES), 1) // SSM_GROUP
            for lt in range(TN_IN // LANES):
                tile = (col - 3 * ATTN_WIDTH) // LANES + lt
                for blk in range(N_PERM):
                    for th in range(CHUNK // SEGS):
                        src = [acc[_piece(blk, th * SEGS + k), _lane(lt)] for k in range(SEGS)]
                        for gl, folded in enumerate(_transpose_segments(src, seg)):
                            u_ref[tile * SEGS + gl, _chunks(blk), _lane(th)] = folded.astype(BF16)
            continue
        for hh in range(HEADS_PER_BLK):
            r = acc[:, hh * HEAD_DIM:(hh + 1) * HEAD_DIM]
            if col < 2 * ATTN_WIDTH:
                r = (r * cos + pltpu.roll(r, ROT_DIM // 2, 1) * sin_hi
                     + pltpu.roll(r, HEAD_DIM - ROT_DIM // 2, 1) * sin_lo)
            if col < ATTN_WIDTH:
                r = r * Q_SCALE
            r = r.astype(BF16)
            head = j * HEADS_PER_BLK + hh
            for blk in range(N_PERM):
                for t in range(CHUNK):
                    qkv_ref[head, _chunks(blk), _lane(t)] = r[_piece(blk, t), :]


def _rope_tables(s):
    half = ROT_DIM // 2
    freq = np.zeros(HEAD_DIM)
    freq[:ROT_DIM] = np.tile(ROPE_THETA ** (-np.arange(0, ROT_DIM, 2) / ROT_DIM), 2)
    base = (np.arange(s // TM) * TM)[:, None] * freq[None, :]
    off = _tile_positions()[:, None] * freq[None, :]
    signs = np.zeros((2, HEAD_DIM))
    signs[0, half:ROT_DIM] = 1.0
    signs[1, :half] = -1.0
    as_f32 = lambda a: jnp.asarray(a.astype(np.float32))
    return (as_f32(np.stack([np.cos(base), np.sin(base)], axis=1)),
            as_f32(np.stack([np.cos(off), np.sin(off)])), as_f32(signs))


def _block_permutation():
    pos = _tile_positions()[:PERM]
    return (pos[:, None] == np.arange(PERM)[None, :]).astype(np.float32)


def _inproj(x, g, w_bf16, rope):
    s = x.shape[0]
    width = MAX_DIL * HEAD_DIM
    perm = jnp.asarray(_block_permutation(), BF16)
    rope_base, rope_off, rope_signs = rope
    return pl.pallas_call(
        _inproj_kernel,
        grid=(s // TM,),
        in_specs=[
            pl.BlockSpec((TM, D_MODEL), lambda i: (i, 0)),
            _resident((1, D_MODEL)),
            _resident((D_MODEL, IN_WIDTH)),
            pl.BlockSpec((None, 2, HEAD_DIM), lambda i: (i, 0, 0)),
            _resident(rope_off.shape),
            _resident(rope_signs.shape),
            _resident((PERM, PERM)),
        ],
        out_specs=[
            pl.BlockSpec((3 * N_HEADS, TM_CHUNKS, width), lambda i: (0, i, 0)),
            pl.BlockSpec((N_GROUPS, TM_CHUNKS, CHUNK * SSM_GROUP), lambda i: (0, i, 0)),
        ],
        out_shape=[
            jax.ShapeDtypeStruct((3 * N_HEADS, s // MAX_DIL, width), BF16),
            jax.ShapeDtypeStruct((N_GROUPS, s // CHUNK, CHUNK * SSM_GROUP), BF16),
        ],
        scratch_shapes=[pltpu.VMEM((TM, D_MODEL), BF16)],
        compiler_params=_cparams(("parallel",)),
        name="inproj",
    )(x, g, w_bf16, rope_base, rope_off, rope_signs, perm)


def _band_bias(tile, perm_mod, perm_mul):
    rho = np.arange(tile)
    lat = perm_mul * (rho % perm_mod) + rho // perm_mod
    jq = lat[:, None]
    jk = np.concatenate([lat - tile, lat])[None, :]
    dist = jq - jk
    valid = (dist >= 0) & (dist <= BAND)
    normal = np.where(valid, 0.0, -np.inf).astype(np.float32)
    first = np.where(valid & (jk >= 0), 0.0, -np.inf).astype(np.float32)
    return np.stack([normal, first])


def _attn_tile(q, k, v, bias, old):
    n = k.shape[0]
    s = lax.dot_general(q, k, (((1,), (1,)), ((), ())), preferred_element_type=F32) + bias
    mt = jnp.max(s, axis=-1, keepdims=True)
    v1 = jnp.concatenate([v, jnp.ones((n, LANES), BF16)], axis=1)
    if old is None:
        m_new = jnp.broadcast_to(mt, (q.shape[0], LANES))
    else:
        acc_o, m_o, l_o = old
        m_new = jnp.maximum(m_o, mt)
    p = jnp.exp2(s - jnp.concatenate([m_new] * (n // LANES), axis=1))
    pv = jnp.dot(p.astype(BF16), v1, preferred_element_type=F32)
    o, l = pv[:, :HEAD_DIM], pv[:, HEAD_DIM:]
    if old is not None:
        alpha = jnp.exp2(m_o - m_new)
        o = alpha * acc_o + o
        l = alpha * l_o + l
    return o, m_new, l


def _attn_kernel(q_ref, kp_ref, kc_ref, vp_ref, vc_ref, b16_ref, b4_ref, b1_ref,
                 o_ref, acc_ref, m_ref, l_ref):
    first = jnp.where(pl.program_id(1) == 0, 1, 0)

    bias = b16_ref[first]
    for r in range(MAX_DIL):
        k = jnp.concatenate([kp_ref[:, _lane(r)], kc_ref[:, _lane(r)]], axis=0)
        v = jnp.concatenate([vp_ref[:, _lane(r)], vc_ref[:, _lane(r)]], axis=0)
        o, m, l = _attn_tile(q_ref[:, _lane(r)], k, v, bias, None)
        acc_ref[:, _lane(r)] = o
        m_ref[:, _lane(r)] = m
        l_ref[:, _lane(r)] = l

    def run_pattern(dil, rows, final):
        n_c = MAX_DIL // dil
        n_b = BAND // rows
        b_ref = b4_ref if dil == 4 else b1_ref
        for b in range(n_b):
            bias = b_ref[first] if b == 0 else b_ref[0]
            cur = slice(b * rows, (b + 1) * rows)
            prev = slice((b - 1) * rows, b * rows) if b > 0 else slice(BAND - rows, BAND)
            for r in range(dil):
                blocks = [_lane(r + dil * c) for c in range(n_c)]

                def gather(ref, rsl):
                    return jnp.concatenate([ref[rsl, bl] for bl in blocks], axis=0)

                q = gather(q_ref, cur)
                k = jnp.concatenate(
                    [gather(kc_ref if b > 0 else kp_ref, prev), gather(kc_ref, cur)], axis=0)
                v = jnp.concatenate(
                    [gather(vc_ref if b > 0 else vp_ref, prev), gather(vc_ref, cur)], axis=0)
                old = (gather(acc_ref, cur), gather(m_ref, cur), gather(l_ref, cur))
                o, m, l = _attn_tile(q, k, v, bias, old)
                for c, bl in enumerate(blocks):
                    piece = slice(c * rows, (c + 1) * rows)
                    if final:
                        o_ref[cur, bl] = (o[piece] / l[piece]).astype(o_ref.dtype)
                    else:
                        acc_ref[cur, bl] = o[piece]
                        m_ref[cur, bl] = m[piece]
                        l_ref[cur, bl] = l[piece]

    run_pattern(4, 32, False)
    run_pattern(1, 16, True)


def _attention(qkv):
    rows, width = qkv.shape[1:]
    b16 = jnp.asarray(_band_bias(BAND, BAND, 1))
    b4 = jnp.asarray(_band_bias(BAND, BAND // 4, 4))
    b1 = jnp.asarray(_band_bias(2 * BAND, MAX_DIL, MAX_DIL))
    blk = (None, BAND, width)

    def spec(base, prev):
        if prev:
            return pl.BlockSpec(blk, lambda h, i: (base + h, jnp.maximum(i - 1, 0), 0))
        return pl.BlockSpec(blk, lambda h, i: (base + h, i, 0))

    return pl.pallas_call(
        _attn_kernel,
        grid=(N_HEADS, rows // BAND),
        in_specs=[spec(0, False), spec(N_HEADS, True), spec(N_HEADS, False),
                  spec(2 * N_HEADS, True), spec(2 * N_HEADS, False),
                  _resident(b16.shape), _resident(b4.shape), _resident(b1.shape)],
        out_specs=pl.BlockSpec(blk, lambda h, i: (h, i, 0)),
        out_shape=jax.ShapeDtypeStruct((N_HEADS, rows, width), BF16),
        scratch_shapes=[pltpu.VMEM((BAND, width), F32)] * 3,
        compiler_params=_cparams(("parallel", "parallel")),
        name="dilated_attn",
    )(qkv, qkv, qkv, qkv, qkv, b16, b4, b1)


GROUPS_PER_TILE = SEGS
N_TILES = N_GROUPS // GROUPS_PER_TILE
PAIRS = GROUPS_PER_TILE // 2
STATE_W = GROUPS_PER_TILE * SSM_STATE
FOLD = CHUNK * SSM_GROUP
PAIR_STATE = 2 * SSM_STATE


def _shift_down(x, k, row):
    return jnp.where(row >= k, pltpu.roll(x, k, 0), 0.0)


def _s5_kernel(u_ref, toe_ref, wb_ref, wc_ref, sc_ref, y_ref, h_ref):
    n_rows = u_ref.shape[1]
    for k in range(PAIRS):
        e = (jnp.dot(u_ref[2 * k], wb_ref[2 * k], preferred_element_type=F32)
             + jnp.dot(u_ref[2 * k + 1], wb_ref[2 * k + 1], preferred_element_type=F32))
        h_ref[:, _lane(k)] = e[:, :PAIR_STATE]
        h_ref[:, STATE_W + k * PAIR_STATE:STATE_W + (k + 1) * PAIR_STATE] = e[:, PAIR_STATE:]

    row = lax.broadcasted_iota(jnp.int32, (SUBLANES, STATE_W), 0)

    def block(b, carry):
        cr, ci = carry
        r0 = pl.multiple_of(b * SUBLANES, SUBLANES)
        xr = h_ref[pl.ds(r0, SUBLANES), 0:STATE_W]
        xi = h_ref[pl.ds(r0, SUBLANES), STATE_W:2 * STATE_W]
        for i in range(3):
            kr = sc_ref[16 + 2 * i:17 + 2 * i, :]
            ki = sc_ref[17 + 2 * i:18 + 2 * i, :]
            sr, si = _shift_down(xr, 1 << i, row), _shift_down(xi, 1 << i, row)
            xr, xi = xr + (kr * sr - ki * si), xi + (kr * si + ki * sr)
        pr, pi = sc_ref[0:8, :], sc_ref[8:16, :]
        hr = xr + (pr * cr - pi * ci)
        hi = xi + (pr * ci + pi * cr)
        h_ref[pl.ds(r0, SUBLANES), 0:STATE_W] = jnp.where(row >= 1, pltpu.roll(hr, 1, 0), cr)
        h_ref[pl.ds(r0, SUBLANES), STATE_W:2 * STATE_W] = jnp.where(
            row >= 1, pltpu.roll(hi, 1, 0), ci)
        return hr[SUBLANES - 1:SUBLANES, :], hi[SUBLANES - 1:SUBLANES, :]

    zero = jnp.zeros((1, STATE_W), F32)
    lax.fori_loop(0, n_rows // SUBLANES, block, (zero, zero))

    for k in range(PAIRS):
        hin = jnp.concatenate(
            [h_ref[:, _lane(k)],
             h_ref[:, STATE_W + k * PAIR_STATE:STATE_W + (k + 1) * PAIR_STATE]],
            axis=1).astype(BF16)
        for s in range(2):
            g = 2 * k + s
            y = (jnp.dot(u_ref[g], toe_ref[g], preferred_element_type=F32)
                 + jnp.dot(hin, wc_ref[g], preferred_element_type=F32))
            y_ref[g] = y.astype(y_ref.dtype)


def _s5_weights(a_re, a_im, log_dt, b_re, b_im, c_re, c_im, d_skip):
    g, n, p = N_GROUPS, SSM_STATE, SSM_GROUP
    nt, gl = N_TILES, GROUPS_PER_TILE
    ar, ai = a_re.astype(F32), a_im.astype(F32)
    dt = jnp.exp(log_dt.astype(F32))[:, None]

    def apow(ks):
        k = jnp.asarray(ks, F32)[:, None, None]
        mag, ph = jnp.exp(ar * dt * k), ai * dt * k
        return mag * jnp.cos(ph), mag * jnp.sin(ph)

    pwr, pwi = apow(np.arange(CHUNK + 1))
    abr, abi = pwr[1], pwi[1]
    nr, ni, den = abr - 1.0, abi, ar * ar + ai * ai
    fr, fi = (nr * ar + ni * ai) / den, (ni * ar - nr * ai) / den
    bre = b_re.astype(F32).transpose(0, 2, 1)
    bim = b_im.astype(F32).transpose(0, 2, 1)
    bbr = fr[:, None, :] * bre - fi[:, None, :] * bim
    bbi = fr[:, None, :] * bim + fi[:, None, :] * bre
    cre, cim = c_re.astype(F32), c_im.astype(F32)

    pr0, pi0 = pwr[:CHUNK].transpose(1, 0, 2)[:, :, None, :], pwi[:CHUNK].transpose(1, 0, 2)[:, :, None, :]
    ca0 = jnp.concatenate([cre[:, None] * pr0 - cim[:, None] * pi0,
                           cre[:, None] * pi0 + cim[:, None] * pr0], axis=-1)
    bb2 = jnp.concatenate([bbr, -bbi], axis=-1)
    kern = jnp.einsum('gqk,gak->gqa', bb2, ca0.reshape(g, CHUNK * p, 2 * n), precision=HI)
    skip = jnp.eye(p, dtype=F32)[None] * d_skip.astype(F32)[:, None, :]
    kern = kern + jnp.pad(skip, ((0, 0), (0, 0), (0, FOLD - p)))
    toe = jnp.stack([jnp.pad(kern[:, :, :FOLD - p * t], ((0, 0), (0, 0), (p * t, 0)))
                     for t in range(CHUNK)], axis=1).reshape(g, FOLD, FOLD)

    def dup(z):
        return jnp.concatenate([z, z], axis=-1)

    odd = np.arange(g) % 2 == 1
    own = jnp.asarray((np.arange(LANES) >= n)[None, :] == odd[:, None])
    er = dup(pwr[CHUNK - 1::-1]).transpose(1, 0, 2)[:, :, None, :]
    ei = dup(pwi[CHUNK - 1::-1]).transpose(1, 0, 2)[:, :, None, :]
    br, bi, own4 = dup(bbr)[:, None], dup(bbi)[:, None], own[:, None, None, :]
    wb = jnp.concatenate([jnp.where(own4, er * br - ei * bi, 0.0),
                          jnp.where(own4, er * bi + ei * br, 0.0)], axis=-1)
    wb = wb.reshape(g, FOLD, 2 * LANES)
    cre_f = jnp.tile(cre.transpose(0, 2, 1), (1, 1, CHUNK))
    cim_f = jnp.tile(cim.transpose(0, 2, 1), (1, 1, CHUNK))
    ar1 = jnp.repeat(pwr[1:].transpose(1, 2, 0), p, axis=2)
    ai1 = jnp.repeat(pwi[1:].transpose(1, 2, 0), p, axis=2)
    car, cai = cre_f * ar1 - cim_f * ai1, cre_f * ai1 + cim_f * ar1
    zero = jnp.zeros_like(car)
    wc = jnp.where(jnp.asarray(odd)[:, None, None],
                   jnp.concatenate([zero, car, zero, -cai], axis=1),
                   jnp.concatenate([car, zero, -cai, zero], axis=1))

    def tile_lanes(z):
        return z.reshape(z.shape[0], nt, gl * n).transpose(1, 0, 2)

    cyr, cyi = apow(CHUNK * (np.arange(SUBLANES) + 1))
    str_, sti = apow(CHUNK * (1 << np.arange(3)))
    step_rows = jnp.stack([str_, sti], axis=1).reshape(6, g, n)
    sc = jnp.concatenate([tile_lanes(cyr), tile_lanes(cyi), tile_lanes(step_rows),
                          jnp.zeros((nt, 2, STATE_W), F32)], axis=1)
    return (toe.astype(BF16), wb.astype(BF16), wc.astype(BF16)), sc


def _s5_core(u2, weights, sc):
    _, rows, _ = u2.shape
    blk = pl.BlockSpec((GROUPS_PER_TILE, rows, FOLD), lambda j: (j, 0, 0))
    w_spec = pl.BlockSpec((GROUPS_PER_TILE, FOLD, FOLD), lambda j: (j, 0, 0))
    return pl.pallas_call(
        _s5_kernel,
        grid=(N_TILES,),
        in_specs=[
            blk, w_spec, w_spec, w_spec,
            pl.BlockSpec((None, 3 * SUBLANES, STATE_W), lambda j: (j, 0, 0)),
        ],
        out_specs=blk,
        out_shape=jax.ShapeDtypeStruct(u2.shape, BF16),
        scratch_shapes=[pltpu.VMEM((rows, 2 * STATE_W), F32)],
        compiler_params=_cparams(("parallel",)),
        name="s5_scan",
    )(u2, *weights, sc)


def _outproj_kernel(x_ref, a_ref, y_ref, wglu_ref, bglu_ref, wout_ref, g_ref, unperm_ref,
                    h_ref, hn_ref):
    seg = lax.broadcasted_iota(jnp.int32, (CHUNK, LANES), 1) // SSM_GROUP
    for blk in range(N_PERM):
        rows = slice(blk * PERM, (blk + 1) * PERM)
        y_t = [[] for _ in range(CHUNK)]
        for tile in range(N_GROUPS // SEGS):
            for th in range(CHUNK // SEGS):
                src = [y_ref[tile * SEGS + gl, _chunks(blk), _lane(th)].astype(F32)
                       for gl in range(SEGS)]
                for k, unfolded in enumerate(_transpose_segments(src, seg)):
                    y_t[th * SEGS + k].append(unfolded)
        y = jnp.concatenate([jnp.concatenate(parts, axis=1) for parts in y_t], axis=0)
        y = jax.nn.gelu(y)
        gate = jax.nn.sigmoid(
            jnp.dot(y.astype(BF16), wglu_ref[...], preferred_element_type=F32) + bglu_ref[...])
        ssm = (y * gate).astype(BF16)
        attn = jnp.concatenate(
            [jnp.concatenate([a_ref[hh, _chunks(blk), _lane(t)] for t in range(CHUNK)], axis=0)
             for hh in range(N_HEADS)], axis=1)
        mix = jnp.concatenate([attn, ssm], axis=1)
        mix = jnp.dot(unperm_ref[...], mix, preferred_element_type=F32).astype(BF16)
        h = x_ref[rows, :] + jnp.dot(mix, wout_ref[...], preferred_element_type=F32)
        h_ref[rows, :] = h
        ms = jnp.mean(h * h, axis=-1, keepdims=True)
        hn_ref[rows, :] = (h * lax.rsqrt(ms + RMS_EPS) * g_ref[...]).astype(BF16)


def _outproj(x, attn, y, w_glu, b_glu, w_out, g2):
    s = x.shape[0]
    width = MAX_DIL * HEAD_DIM
    unperm = jnp.asarray(_block_permutation(), BF16)
    return pl.pallas_call(
        _outproj_kernel,
        grid=(s // TM,),
        in_specs=[
            pl.BlockSpec((TM, D_MODEL), lambda i: (i, 0)),
            pl.BlockSpec((N_HEADS, TM_CHUNKS, width), lambda i: (0, i, 0)),
            pl.BlockSpec((N_GROUPS, TM_CHUNKS, FOLD), lambda i: (0, i, 0)),
            _resident((SSM_WIDTH, SSM_WIDTH)),
            _resident((1, SSM_WIDTH)),
            _resident((D_MODEL, D_MODEL)),
            _resident((1, D_MODEL)),
            _resident((PERM, PERM)),
        ],
        out_specs=[pl.BlockSpec((TM, D_MODEL), lambda i: (i, 0))] * 2,
        out_shape=[jax.ShapeDtypeStruct((s, D_MODEL), F32),
                   jax.ShapeDtypeStruct((s, D_MODEL), BF16)],
        compiler_params=_cparams(("parallel",)),
        name="outproj",
    )(x, attn, y, w_glu, b_glu, w_out, g2, unperm)


TM_FFN = 1024
TF_FFN = 512
N_F = D_FF // TF_FFN
LOAD_AT = N_F // 2


def _ffn_kernel(n_tiles, hn_ref, h_hbm, wg_ref, wu_ref, wd_ref, g_ref, o_hbm,
                acc_ref, act_a, act_b, sem_in, sem_out):
    k = pl.program_id(0)
    f_prev = lax.rem(k + (N_F - 1), N_F)
    tile = jnp.maximum(k - 1, 0) // N_F
    slot = lax.rem(tile, 2)

    def rows(j):
        return pl.ds(pl.multiple_of(j * TM_FFN, TM_FFN), TM_FFN)

    def copy_in(j, s):
        return pltpu.make_async_copy(h_hbm.at[rows(j), :], acc_ref.at[s], sem_in.at[s])

    def copy_out(j, s):
        return pltpu.make_async_copy(acc_ref.at[s], o_hbm.at[rows(j), :], sem_out.at[s])

    @pl.when(k == 0)
    def _():
        act_b[...] = jnp.zeros_like(act_b)
        acc_ref[0] = jnp.zeros((TM_FFN, D_MODEL), F32)

    @pl.when(f_prev == 0)
    def _():
        copy_in(tile, slot).wait()

    def step(act_prev, act_next):
        part = jnp.dot(act_prev[...], wd_ref[...].astype(BF16), preferred_element_type=F32)
        hn = hn_ref[...]
        gate = jnp.dot(hn, wg_ref[...].astype(BF16), preferred_element_type=F32)
        up = jnp.dot(hn, wu_ref[...].astype(BF16), preferred_element_type=F32)
        act_next[...] = (jax.nn.silu(gate) * up).astype(BF16)
        acc_ref[slot] = acc_ref[slot] + part

    @pl.when(lax.rem(k, 2) == 0)
    def _():
        step(act_b, act_a)

    @pl.when(lax.rem(k, 2) == 1)
    def _():
        step(act_a, act_b)

    @pl.when(k == 0)
    def _():
        copy_in(0, 0).start()

    @pl.when((f_prev == LOAD_AT) & (tile >= 1))
    def _():
        copy_out(tile - 1, 1 - slot).wait()

    @pl.when((f_prev == LOAD_AT) & (tile + 1 < n_tiles))
    def _():
        copy_in(tile + 1, 1 - slot).start()

    @pl.when((f_prev == N_F - 1) & (k > 0))
    def _():
        h = acc_ref[slot]
        ms = jnp.mean(h * h, axis=-1, keepdims=True)
        acc_ref[slot] = h * lax.rsqrt(ms + RMS_EPS) * g_ref[...]
        copy_out(tile, slot).start()

    @pl.when(k == pl.num_programs(0) - 1)
    def _():
        copy_out(tile, slot).wait()


def _ffn(hn, h, w_gate, w_up, w_down, g):
    s = h.shape[0]
    n_i = s // TM_FFN
    return pl.pallas_call(
        functools.partial(_ffn_kernel, n_i),
        grid=(n_i * N_F + 1,),
        in_specs=[
            pl.BlockSpec((TM_FFN, D_MODEL), lambda k: (jnp.minimum(k // N_F, n_i - 1), 0)),
            pl.BlockSpec(memory_space=pl.ANY),
            pl.BlockSpec((D_MODEL, TF_FFN), lambda k: (0, k % N_F)),
            pl.BlockSpec((D_MODEL, TF_FFN), lambda k: (0, k % N_F)),
            pl.BlockSpec((TF_FFN, D_MODEL), lambda k: (jnp.maximum(k - 1, 0) % N_F, 0)),
            _resident((1, D_MODEL)),
        ],
        out_specs=pl.BlockSpec(memory_space=pl.ANY),
        out_shape=jax.ShapeDtypeStruct((s, D_MODEL), F32),
        scratch_shapes=[pltpu.VMEM((2, TM_FFN, D_MODEL), F32),
                        pltpu.VMEM((TM_FFN, TF_FFN), BF16),
                        pltpu.VMEM((TM_FFN, TF_FFN), BF16),
                        pltpu.SemaphoreType.DMA((2,)),
                        pltpu.SemaphoreType.DMA((2,))],
        compiler_params=_cparams(("arbitrary",)),
        name="ffn",
    )(hn, h, w_gate, w_up, w_down, g)


def kernel(x, norm1_g, w_in, a_re, a_im, log_dt, b_re, b_im, c_re, c_im, d_skip, w_glu, b_glu,
           w_out, norm2_g, w_gate, w_up, w_down, final_g):
    b, s, _ = x.shape
    assert b == 1 and s % SUPER == 0 and w_in.shape[0] == 1
    x2 = x[0]
    qkv, u = _inproj(x2, norm1_g[0][None, :], w_in[0].astype(BF16), _rope_tables(s))
    attn = _attention(qkv)
    s5_w, sc = _s5_weights(a_re[0], a_im[0], log_dt[0], b_re[0], b_im[0], c_re[0], c_im[0],
                           d_skip[0])
    y = _s5_core(u, s5_w, sc)
    h, hn = _outproj(x2, attn, y, w_glu[0].astype(BF16), b_glu[0][None, :].astype(F32),
                     w_out[0].astype(BF16), norm2_g[0][None, :])
    out = _ffn(hn, h, w_gate[0], w_up[0], w_down[0], final_g[None, :])
    return out[None]
```

```python
import functools

import numpy as np
import jax
import jax.numpy as jnp
from jax import lax
from jax.experimental import pallas as pl
from jax.experimental.pallas import tpu as pltpu

F32 = jnp.float32
BF16 = jnp.bfloat16
HI = lax.Precision.HIGHEST

D_MODEL = 2048
ATTN_WIDTH = 1024
SSM_WIDTH = 1024
HEAD_DIM = 128
N_HEADS = ATTN_WIDTH // HEAD_DIM
ROT_DIM = HEAD_DIM // 4
ROPE_THETA = 500000.0
BAND = 128
MAX_DIL = 16
SUPER = BAND * MAX_DIL
SSM_GROUP = 16
N_GROUPS = SSM_WIDTH // SSM_GROUP
SSM_STATE = 64
CHUNK = 16
D_FF = 5632
IN_WIDTH = 3 * ATTN_WIDTH + SSM_WIDTH
RMS_EPS = 1e-6
LANES = 128
SUBLANES = 8

TM = 512
TM_CHUNKS = TM // CHUNK

VMEM_LIMIT = 58 * 1024 * 1024


def _cparams(sem):
    return pltpu.CompilerParams(dimension_semantics=sem, vmem_limit_bytes=VMEM_LIMIT)


def _resident(shape):
    zeros = (0,) * len(shape)
    return pl.BlockSpec(shape, lambda *_: zeros, pipeline_mode=pl.Buffered(1))


PERM = CHUNK * CHUNK
N_PERM = TM // PERM


def _piece(blk, t):
    start = blk * PERM + t * CHUNK
    return slice(start, start + CHUNK)


def _chunks(blk):
    return slice(blk * CHUNK, (blk + 1) * CHUNK)


def _tile_positions():
    rho = np.arange(TM)
    r = rho % PERM
    return (rho // PERM) * PERM + CHUNK * (r % CHUNK) + r // CHUNK


def _lane(r):
    return slice(r * LANES, (r + 1) * LANES)


SEGS = LANES // SSM_GROUP


def _transpose_segments(src, seg):
    x = list(src)
    d = SEGS // 2
    while d:
        upper = (seg & d) != 0
        for i in range(SEGS):
            if i & d:
                continue
            a, b = x[i], x[i + d]
            x[i] = jnp.where(upper, pltpu.roll(b, d * SSM_GROUP, 1), a)
            x[i + d] = jnp.where(upper, b, pltpu.roll(a, LANES - d * SSM_GROUP, 1))
        d //= 2
    return x


TN_IN = 512
HEADS_PER_BLK = TN_IN // HEAD_DIM
Q_SCALE = float(HEAD_DIM ** -0.5 * np.log2(np.e))


def _inproj_kernel(x_ref, g_ref, w_ref, rb_ref, ro_ref, rs_ref, perm_ref, qkv_ref, u_ref, hn_ref):
    x = x_ref[...]
    ms = jnp.mean(x * x, axis=-1, keepdims=True)
    hn = (x * lax.rsqrt(ms + RMS_EPS) * g_ref[...]).astype(BF16)
    for blk in range(N_PERM):
        rows = slice(blk * PERM, (blk + 1) * PERM)
        hn_ref[rows, :] = jnp.dot(perm_ref[...], hn[rows, :],
                                  preferred_element_type=F32).astype(BF16)

    cb, sb = rb_ref[0:1, :], rb_ref[1:2, :]
    co, so = ro_ref[0], ro_ref[1]
    cos, sin = cb * co - sb * so, sb * co + cb * so
    sin_hi, sin_lo = sin * rs_ref[0:1, :], sin * rs_ref[1:2, :]
    n_blk, n_qkv = IN_WIDTH // TN_IN, 3 * ATTN_WIDTH // TN_IN
    for j in list(range(n_qkv, n_blk)) + list(range(n_qkv)):
        acc = jnp.dot(hn_ref[...], w_ref[:, j * TN_IN:(j + 1) * TN_IN],
                      preferred_element_type=F32)
        col = j * TN_IN
        if col >= 3 * ATTN_WIDTH:
            seg = lax.broadcasted_iota(jnp.int32, (CHUNK, LANES), 1) // SSM_GROUP
            for lt in range(TN_IN // LANES):
                tile = (col - 3 * ATTN_WIDTH) // LANES + lt
                for blk in range(N_PERM):
                    for th in range(CHUNK // SEGS):
                        src = [acc[_piece(blk, th * SEGS + k), _lane(lt)] for k in range(SEGS)]
                        for gl, folded in enumerate(_transpose_segments(src, seg)):
                            u_ref[tile * SEGS + gl, _chunks(blk), _lane(th)] = folded.astype(BF16)
            continue
        for hh in range(HEADS_PER_BLK):
            r = acc[:, hh * HEAD_DIM:(hh + 1) * HEAD_DIM]
            if col < 2 * ATTN_WIDTH:
                r = (r * cos + pltpu.roll(r, ROT_DIM // 2, 1) * sin_hi
                     + pltpu.roll(r, HEAD_DIM - ROT_DIM // 2, 1) * sin_lo)
            if col < ATTN_WIDTH:
                r = r * Q_SCALE
            r = r.astype(BF16)
            head = j * HEADS_PER_BLK + hh
            for blk in range(N_PERM):
                for t in range(CHUNK):
                    qkv_ref[head, _chunks(blk), _lane(t)] = r[_piece(blk, t), :]


def _rope_tables(s):
    half = ROT_DIM // 2
    freq = np.zeros(HEAD_DIM)
    freq[:ROT_DIM] = np.tile(ROPE_THETA ** (-np.arange(0, ROT_DIM, 2) / ROT_DIM), 2)
    base = (np.arange(s // TM) * TM)[:, None] * freq[None, :]
    off = _tile_positions()[:, None] * freq[None, :]
    signs = np.zeros((2, HEAD_DIM))
    signs[0, half:ROT_DIM] = 1.0
    signs[1, :half] = -1.0
    as_f32 = lambda a: jnp.asarray(a.astype(np.float32))
    return (as_f32(np.stack([np.cos(base), np.sin(base)], axis=1)),
            as_f32(np.stack([np.cos(off), np.sin(off)])), as_f32(signs))


def _block_permutation():
    pos = _tile_positions()[:PERM]
    return (pos[:, None] == np.arange(PERM)[None, :]).astype(np.float32)


def _inproj(x, g, w_bf16, rope):
    s = x.shape[0]
    width = MAX_DIL * HEAD_DIM
    perm = jnp.asarray(_block_permutation(), BF16)
    rope_base, rope_off, rope_signs = rope
    return pl.pallas_call(
        _inproj_kernel,
        grid=(s // TM,),
        in_specs=[
            pl.BlockSpec((TM, D_MODEL), lambda i: (i, 0)),
            _resident((1, D_MODEL)),
            _resident((D_MODEL, IN_WIDTH)),
            pl.BlockSpec((None, 2, HEAD_DIM), lambda i: (i, 0, 0)),
            _resident(rope_off.shape),
            _resident(rope_signs.shape),
            _resident((PERM, PERM)),
        ],
        out_specs=[
            pl.BlockSpec((3 * N_HEADS, TM_CHUNKS, width), lambda i: (0, i, 0)),
            pl.BlockSpec((N_GROUPS, TM_CHUNKS, CHUNK * SSM_GROUP), lambda i: (0, i, 0)),
        ],
        out_shape=[
            jax.ShapeDtypeStruct((3 * N_HEADS, s // MAX_DIL, width), BF16),
            jax.ShapeDtypeStruct((N_GROUPS, s // CHUNK, CHUNK * SSM_GROUP), BF16),
        ],
        scratch_shapes=[pltpu.VMEM((TM, D_MODEL), BF16)],
        compiler_params=_cparams(("parallel",)),
        name="inproj",
    )(x, g, w_bf16, rope_base, rope_off, rope_signs, perm)


def _band_bias(tile, perm_mod, perm_mul):
    rho = np.arange(tile)
    lat = perm_mul * (rho % perm_mod) + rho // perm_mod
    jq = lat[:, None]
    jk = np.concatenate([lat - tile, lat])[None, :]
    dist = jq - jk
    valid = (dist >= 0) & (dist <= BAND)
    normal = np.where(valid, 0.0, -np.inf).astype(np.float32)
    first = np.where(valid & (jk >= 0), 0.0, -np.inf).astype(np.float32)
    return np.stack([normal, first])


def _attn_tile(q, k, v, bias, old):
    n = k.shape[0]
    s = lax.dot_general(q, k, (((1,), (1,)), ((), ())), preferred_element_type=F32) + bias
    mt = jnp.max(s, axis=-1, keepdims=True)
    v1 = jnp.concatenate([v, jnp.ones((n, LANES), BF16)], axis=1)
    if old is None:
        m_new = jnp.broadcast_to(mt, (q.shape[0], LANES))
    else:
        acc_o, m_o, l_o = old
        m_new = jnp.maximum(m_o, mt)
    p = jnp.exp2(s - jnp.concatenate([m_new] * (n // LANES), axis=1))
    pv = jnp.dot(p.astype(BF16), v1, preferred_element_type=F32)
    o, l = pv[:, :HEAD_DIM], pv[:, HEAD_DIM:]
    if old is not None:
        alpha = jnp.exp2(m_o - m_new)
        o = alpha * acc_o + o
        l = alpha * l_o + l
    return o, m_new, l


def _attn_kernel(q_ref, kp_ref, kc_ref, vp_ref, vc_ref, b16_ref, b4_ref, b1_ref,
                 o_ref, acc_ref, m_ref, l_ref):
    first = jnp.where(pl.program_id(1) == 0, 1, 0)

    bias = b16_ref[first]
    for r in range(MAX_DIL):
        k = jnp.concatenate([kp_ref[:, _lane(r)], kc_ref[:, _lane(r)]], axis=0)
        v = jnp.concatenate([vp_ref[:, _lane(r)], vc_ref[:, _lane(r)]], axis=0)
        o, m, l = _attn_tile(q_ref[:, _lane(r)], k, v, bias, None)
        acc_ref[:, _lane(r)] = o
        m_ref[:, _lane(r)] = m
        l_ref[:, _lane(r)] = l

    def run_pattern(dil, rows, final):
        n_c = MAX_DIL // dil
        n_b = BAND // rows
        b_ref = b4_ref if dil == 4 else b1_ref
        for b in range(n_b):
            bias = b_ref[first] if b == 0 else b_ref[0]
            cur = slice(b * rows, (b + 1) * rows)
            prev = slice((b - 1) * rows, b * rows) if b > 0 else slice(BAND - rows, BAND)
            for r in range(dil):
                blocks = [_lane(r + dil * c) for c in range(n_c)]

                def gather(ref, rsl):
                    return jnp.concatenate([ref[rsl, bl] for bl in blocks], axis=0)

                k = jnp.concatenate(
                    [gather(kc_ref if b > 0 else kp_ref, prev), gather(kc_ref, cur)], axis=0)
                v = jnp.concatenate(
                    [gather(vc_ref if b > 0 else vp_ref, prev), gather(vc_ref, cur)], axis=0)
                per = BAND // rows
                for q0 in range(0, n_c, per):
                    qblocks = blocks[q0:q0 + per]

                    def qgather(ref):
                        return jnp.concatenate([ref[cur, bl] for bl in qblocks], axis=0)

                    old = (qgather(acc_ref), qgather(m_ref), qgather(l_ref))
                    o, m, l = _attn_tile(qgather(q_ref), k, v,
                                         bias[q0 * rows:(q0 + per) * rows, :], old)
                    for c, bl in enumerate(qblocks):
                        piece = slice(c * rows, (c + 1) * rows)
                        if final:
                            o_ref[cur, bl] = (o[piece] / l[piece]).astype(o_ref.dtype)
                        else:
                            acc_ref[cur, bl] = o[piece]
                            m_ref[cur, bl] = m[piece]
                            l_ref[cur, bl] = l[piece]

    run_pattern(4, 32, False)
    run_pattern(1, 16, True)


def _attention(qkv):
    rows, width = qkv.shape[1:]
    b16 = jnp.asarray(_band_bias(BAND, BAND, 1))
    b4 = jnp.asarray(_band_bias(BAND, BAND // 4, 4))
    b1 = jnp.asarray(_band_bias(2 * BAND, MAX_DIL, MAX_DIL))
    blk = (None, BAND, width)

    def spec(base, prev):
        if prev:
            return pl.BlockSpec(blk, lambda h, i: (base + h, jnp.maximum(i - 1, 0), 0))
        return pl.BlockSpec(blk, lambda h, i: (base + h, i, 0))

    return pl.pallas_call(
        _attn_kernel,
        grid=(N_HEADS, rows // BAND),
        in_specs=[spec(0, False), spec(N_HEADS, True), spec(N_HEADS, False),
                  spec(2 * N_HEADS, True), spec(2 * N_HEADS, False),
                  _resident(b16.shape), _resident(b4.shape), _resident(b1.shape)],
        out_specs=pl.BlockSpec(blk, lambda h, i: (h, i, 0)),
        out_shape=jax.ShapeDtypeStruct((N_HEADS, rows, width), BF16),
        scratch_shapes=[pltpu.VMEM((BAND, width), F32)] * 3,
        compiler_params=_cparams(("parallel", "parallel")),
        name="dilated_attn",
    )(qkv, qkv, qkv, qkv, qkv, b16, b4, b1)


GROUPS_PER_TILE = SEGS
N_TILES = N_GROUPS // GROUPS_PER_TILE
PAIRS = GROUPS_PER_TILE // 2
STATE_W = GROUPS_PER_TILE * SSM_STATE
FOLD = CHUNK * SSM_GROUP
PAIR_STATE = 2 * SSM_STATE


def _shift_down(x, k, row):
    return jnp.where(row >= k, pltpu.roll(x, k, 0), 0.0)


def _s5_kernel(u_ref, toe_ref, wb_ref, wc_ref, sc_ref, y_ref, h_ref):
    n_rows = u_ref.shape[1]
    for k in range(PAIRS):
        e = (jnp.dot(u_ref[2 * k], wb_ref[2 * k], preferred_element_type=F32)
             + jnp.dot(u_ref[2 * k + 1], wb_ref[2 * k + 1], preferred_element_type=F32))
        h_ref[:, _lane(k)] = e[:, :PAIR_STATE]
        h_ref[:, STATE_W + k * PAIR_STATE:STATE_W + (k + 1) * PAIR_STATE] = e[:, PAIR_STATE:]

    row = lax.broadcasted_iota(jnp.int32, (SUBLANES, STATE_W), 0)

    def block(b, carry):
        cr, ci = carry
        r0 = pl.multiple_of(b * SUBLANES, SUBLANES)
        xr = h_ref[pl.ds(r0, SUBLANES), 0:STATE_W]
        xi = h_ref[pl.ds(r0, SUBLANES), STATE_W:2 * STATE_W]
        for i in range(3):
            kr = sc_ref[16 + 2 * i:17 + 2 * i, :]
            ki = sc_ref[17 + 2 * i:18 + 2 * i, :]
            sr, si = _shift_down(xr, 1 << i, row), _shift_down(xi, 1 << i, row)
            xr, xi = xr + (kr * sr - ki * si), xi + (kr * si + ki * sr)
        pr, pi = sc_ref[0:8, :], sc_ref[8:16, :]
        hr = xr + (pr * cr - pi * ci)
        hi = xi + (pr * ci + pi * cr)
        h_ref[pl.ds(r0, SUBLANES), 0:STATE_W] = jnp.where(row >= 1, pltpu.roll(hr, 1, 0), cr)
        h_ref[pl.ds(r0, SUBLANES), STATE_W:2 * STATE_W] = jnp.where(
            row >= 1, pltpu.roll(hi, 1, 0), ci)
        return hr[SUBLANES - 1:SUBLANES, :], hi[SUBLANES - 1:SUBLANES, :]

    zero = jnp.zeros((1, STATE_W), F32)
    lax.fori_loop(0, n_rows // SUBLANES, block, (zero, zero))

    for k in range(PAIRS):
        hin = jnp.concatenate(
            [h_ref[:, _lane(k)],
             h_ref[:, STATE_W + k * PAIR_STATE:STATE_W + (k + 1) * PAIR_STATE]],
            axis=1).astype(BF16)
        for s in range(2):
            g = 2 * k + s
            y = (jnp.dot(u_ref[g], toe_ref[g], preferred_element_type=F32)
                 + jnp.dot(hin, wc_ref[g], preferred_element_type=F32))
            y_ref[g] = y.astype(y_ref.dtype)


def _s5_weights(a_re, a_im, log_dt, b_re, b_im, c_re, c_im, d_skip):
    g, n, p = N_GROUPS, SSM_STATE, SSM_GROUP
    nt, gl = N_TILES, GROUPS_PER_TILE
    ar, ai = a_re.astype(F32), a_im.astype(F32)
    dt = jnp.exp(log_dt.astype(F32))[:, None]

    def apow(ks):
        k = jnp.asarray(ks, F32)[:, None, None]
        mag, ph = jnp.exp(ar * dt * k), ai * dt * k
        return mag * jnp.cos(ph), mag * jnp.sin(ph)

    pwr, pwi = apow(np.arange(CHUNK + 1))
    abr, abi = pwr[1], pwi[1]
    nr, ni, den = abr - 1.0, abi, ar * ar + ai * ai
    fr, fi = (nr * ar + ni * ai) / den, (ni * ar - nr * ai) / den
    bre = b_re.astype(F32).transpose(0, 2, 1)
    bim = b_im.astype(F32).transpose(0, 2, 1)
    bbr = fr[:, None, :] * bre - fi[:, None, :] * bim
    bbi = fr[:, None, :] * bim + fi[:, None, :] * bre
    cre, cim = c_re.astype(F32), c_im.astype(F32)

    pr0, pi0 = pwr[:CHUNK].transpose(1, 0, 2)[:, :, None, :], pwi[:CHUNK].transpose(1, 0, 2)[:, :, None, :]
    ca0 = jnp.concatenate([cre[:, None] * pr0 - cim[:, None] * pi0,
                           cre[:, None] * pi0 + cim[:, None] * pr0], axis=-1)
    bb2 = jnp.concatenate([bbr, -bbi], axis=-1)
    kern = jnp.einsum('gqk,gak->gqa', bb2, ca0.reshape(g, CHUNK * p, 2 * n), precision=HI)
    skip = jnp.eye(p, dtype=F32)[None] * d_skip.astype(F32)[:, None, :]
    kern = kern + jnp.pad(skip, ((0, 0), (0, 0), (0, FOLD - p)))
    toe = jnp.stack([jnp.pad(kern[:, :, :FOLD - p * t], ((0, 0), (0, 0), (p * t, 0)))
                     for t in range(CHUNK)], axis=1).reshape(g, FOLD, FOLD)

    def dup(z):
        return jnp.concatenate([z, z], axis=-1)

    odd = np.arange(g) % 2 == 1
    own = jnp.asarray((np.arange(LANES) >= n)[None, :] == odd[:, None])
    er = dup(pwr[CHUNK - 1::-1]).transpose(1, 0, 2)[:, :, None, :]
    ei = dup(pwi[CHUNK - 1::-1]).transpose(1, 0, 2)[:, :, None, :]
    br, bi, own4 = dup(bbr)[:, None], dup(bbi)[:, None], own[:, None, None, :]
    wb = jnp.concatenate([jnp.where(own4, er * br - ei * bi, 0.0),
                          jnp.where(own4, er * bi + ei * br, 0.0)], axis=-1)
    wb = wb.reshape(g, FOLD, 2 * LANES)
    cre_f = jnp.tile(cre.transpose(0, 2, 1), (1, 1, CHUNK))
    cim_f = jnp.tile(cim.transpose(0, 2, 1), (1, 1, CHUNK))
    ar1 = jnp.repeat(pwr[1:].transpose(1, 2, 0), p, axis=2)
    ai1 = jnp.repeat(pwi[1:].transpose(1, 2, 0), p, axis=2)
    car, cai = cre_f * ar1 - cim_f * ai1, cre_f * ai1 + cim_f * ar1
    zero = jnp.zeros_like(car)
    wc = jnp.where(jnp.asarray(odd)[:, None, None],
                   jnp.concatenate([zero, car, zero, -cai], axis=1),
                   jnp.concatenate([car, zero, -cai, zero], axis=1))

    def tile_lanes(z):
        return z.reshape(z.shape[0], nt, gl * n).transpose(1, 0, 2)

    cyr, cyi = apow(CHUNK * (np.arange(SUBLANES) + 1))
    str_, sti = apow(CHUNK * (1 << np.arange(3)))
    step_rows = jnp.stack([str_, sti], axis=1).reshape(6, g, n)
    sc = jnp.concatenate([tile_lanes(cyr), tile_lanes(cyi), tile_lanes(step_rows),
                          jnp.zeros((nt, 2, STATE_W), F32)], axis=1)
    return (toe.astype(BF16), wb.astype(BF16), wc.astype(BF16)), sc


def _s5_core(u2, weights, sc):
    _, rows, _ = u2.shape
    blk = pl.BlockSpec((GROUPS_PER_TILE, rows, FOLD), lambda j: (j, 0, 0))
    w_spec = pl.BlockSpec((GROUPS_PER_TILE, FOLD, FOLD), lambda j: (j, 0, 0))
    return pl.pallas_call(
        _s5_kernel,
        grid=(N_TILES,),
        in_specs=[
            blk, w_spec, w_spec, w_spec,
            pl.BlockSpec((None, 3 * SUBLANES, STATE_W), lambda j: (j, 0, 0)),
        ],
        out_specs=blk,
        out_shape=jax.ShapeDtypeStruct(u2.shape, BF16),
        scratch_shapes=[pltpu.VMEM((rows, 2 * STATE_W), F32)],
        compiler_params=_cparams(("parallel",)),
        name="s5_scan",
    )(u2, *weights, sc)


def _outproj_kernel(x_ref, a_ref, y_ref, wglu_ref, bglu_ref, wout_ref, g_ref, unperm_ref,
                    h_ref, hn_ref):
    seg = lax.broadcasted_iota(jnp.int32, (CHUNK, LANES), 1) // SSM_GROUP
    for blk in range(N_PERM):
        rows = slice(blk * PERM, (blk + 1) * PERM)
        y_t = [[] for _ in range(CHUNK)]
        for tile in range(N_GROUPS // SEGS):
            for th in range(CHUNK // SEGS):
                src = [y_ref[tile * SEGS + gl, _chunks(blk), _lane(th)].astype(F32)
                       for gl in range(SEGS)]
                for k, unfolded in enumerate(_transpose_segments(src, seg)):
                    y_t[th * SEGS + k].append(unfolded)
        y = jnp.concatenate([jnp.concatenate(parts, axis=1) for parts in y_t], axis=0)
        y = jax.nn.gelu(y)
        gate = jax.nn.sigmoid(
            jnp.dot(y.astype(BF16), wglu_ref[...], preferred_element_type=F32) + bglu_ref[...])
        ssm = (y * gate).astype(BF16)
        attn = jnp.concatenate(
            [jnp.concatenate([a_ref[hh, _chunks(blk), _lane(t)] for t in range(CHUNK)], axis=0)
             for hh in range(N_HEADS)], axis=1)
        mix = jnp.concatenate([attn, ssm], axis=1)
        mix = jnp.dot(unperm_ref[...], mix, preferred_element_type=F32).astype(BF16)
        h = x_ref[rows, :] + jnp.dot(mix, wout_ref[...], preferred_element_type=F32)
        h_ref[rows, :] = h
        ms = jnp.mean(h * h, axis=-1, keepdims=True)
        hn_ref[rows, :] = (h * lax.rsqrt(ms + RMS_EPS) * g_ref[...]).astype(BF16)


def _outproj(x, attn, y, w_glu, b_glu, w_out, g2):
    s = x.shape[0]
    width = MAX_DIL * HEAD_DIM
    unperm = jnp.asarray(_block_permutation(), BF16)
    return pl.pallas_call(
        _outproj_kernel,
        grid=(s // TM,),
        in_specs=[
            pl.BlockSpec((TM, D_MODEL), lambda i: (i, 0)),
            pl.BlockSpec((N_HEADS, TM_CHUNKS, width), lambda i: (0, i, 0)),
            pl.BlockSpec((N_GROUPS, TM_CHUNKS, FOLD), lambda i: (0, i, 0)),
            _resident((SSM_WIDTH, SSM_WIDTH)),
            _resident((1, SSM_WIDTH)),
            _resident((D_MODEL, D_MODEL)),
            _resident((1, D_MODEL)),
            _resident((PERM, PERM)),
        ],
        out_specs=[pl.BlockSpec((TM, D_MODEL), lambda i: (i, 0))] * 2,
        out_shape=[jax.ShapeDtypeStruct((s, D_MODEL), F32),
                   jax.ShapeDtypeStruct((s, D_MODEL), BF16)],
        compiler_params=_cparams(("parallel",)),
        name="outproj",
    )(x, attn, y, w_glu, b_glu, w_out, g2, unperm)


TM_FFN = 1024
TF_FFN = 512
N_F = D_FF // TF_FFN
LOAD_AT = N_F // 2


def _ffn_kernel(n_tiles, hn_ref, h_hbm, wg_ref, wu_ref, wd_ref, g_ref, o_hbm,
                acc_ref, act_a, act_b, sem_in, sem_out):
    k = pl.program_id(0)
    f_prev = lax.rem(k + (N_F - 1), N_F)
    tile = jnp.maximum(k - 1, 0) // N_F
    slot = lax.rem(tile, 2)

    def rows(j):
        return pl.ds(pl.multiple_of(j * TM_FFN, TM_FFN), TM_FFN)

    def copy_in(j, s):
        return pltpu.make_async_copy(h_hbm.at[rows(j), :], acc_ref.at[s], sem_in.at[s])

    def copy_out(j, s):
        return pltpu.make_async_copy(acc_ref.at[s], o_hbm.at[rows(j), :], sem_out.at[s])

    @pl.when(k == 0)
    def _():
        act_b[...] = jnp.zeros_like(act_b)
        acc_ref[0] = jnp.zeros((TM_FFN, D_MODEL), F32)

    @pl.when(f_prev == 0)
    def _():
        copy_in(tile, slot).wait()

    def step(act_prev, act_next):
        part = jnp.dot(act_prev[...], wd_ref[...].astype(BF16), preferred_element_type=F32)
        hn = hn_ref[...]
        gate = jnp.dot(hn, wg_ref[...].astype(BF16), preferred_element_type=F32)
        up = jnp.dot(hn, wu_ref[...].astype(BF16), preferred_element_type=F32)
        act_next[...] = (jax.nn.silu(gate) * up).astype(BF16)
        acc_ref[slot] = acc_ref[slot] + part

    @pl.when(lax.rem(k, 2) == 0)
    def _():
        step(act_b, act_a)

    @pl.when(lax.rem(k, 2) == 1)
    def _():
        step(act_a, act_b)

    @pl.when(k == 0)
    def _():
        copy_in(0, 0).start()

    @pl.when((f_prev == LOAD_AT) & (tile >= 1))
    def _():
        copy_out(tile - 1, 1 - slot).wait()

    @pl.when((f_prev == LOAD_AT) & (tile + 1 < n_tiles))
    def _():
        copy_in(tile + 1, 1 - slot).start()

    @pl.when((f_prev == N_F - 1) & (k > 0))
    def _():
        h = acc_ref[slot]
        ms = jnp.mean(h * h, axis=-1, keepdims=True)
        acc_ref[slot] = h * lax.rsqrt(ms + RMS_EPS) * g_ref[...]
        copy_out(tile, slot).start()

    @pl.when(k == pl.num_programs(0) - 1)
    def _():
        copy_out(tile, slot).wait()


def _ffn(hn, h, w_gate, w_up, w_down, g):
    s = h.shape[0]
    n_i = s // TM_FFN
    return pl.pallas_call(
        functools.partial(_ffn_kernel, n_i),
        grid=(n_i * N_F + 1,),
        in_specs=[
            pl.BlockSpec((TM_FFN, D_MODEL), lambda k: (jnp.minimum(k // N_F, n_i - 1), 0)),
            pl.BlockSpec(memory_space=pl.ANY),
            pl.BlockSpec((D_MODEL, TF_FFN), lambda k: (0, k % N_F)),
            pl.BlockSpec((D_MODEL, TF_FFN), lambda k: (0, k % N_F)),
            pl.BlockSpec((TF_FFN, D_MODEL), lambda k: (jnp.maximum(k - 1, 0) % N_F, 0)),
            _resident((1, D_MODEL)),
        ],
        out_specs=pl.BlockSpec(memory_space=pl.ANY),
        out_shape=jax.ShapeDtypeStruct((s, D_MODEL), F32),
        scratch_shapes=[pltpu.VMEM((2, TM_FFN, D_MODEL), F32),
                        pltpu.VMEM((TM_FFN, TF_FFN), BF16),
                        pltpu.VMEM((TM_FFN, TF_FFN), BF16),
                        pltpu.SemaphoreType.DMA((2,)),
                        pltpu.SemaphoreType.DMA((2,))],
        compiler_params=_cparams(("arbitrary",)),
        name="ffn",
    )(hn, h, w_gate, w_up, w_down, g)


def kernel(x, norm1_g, w_in, a_re, a_im, log_dt, b_re, b_im, c_re, c_im, d_skip, w_glu, b_glu,
           w_out, norm2_g, w_gate, w_up, w_down, final_g):
    b, s, _ = x.shape
    assert b == 1 and s % SUPER == 0 and w_in.shape[0] == 1
    x2 = x[0]
    qkv, u = _inproj(x2, norm1_g[0][None, :], w_in[0].astype(BF16), _rope_tables(s))
    attn = _attention(qkv)
    s5_w, sc = _s5_weights(a_re[0], a_im[0], log_dt[0], b_re[0], b_im[0], c_re[0], c_im[0],
                           d_skip[0])
    y = _s5_core(u, s5_w, sc)
    h, hn = _outproj(x2, attn, y, w_glu[0].astype(BF16), b_glu[0][None, :].astype(F32),
                     w_out[0].astype(BF16), norm2_g[0][None, :])
    out = _ffn(hn, h, w_gate[0], w_up[0], w_down[0], final_g[None, :])
    return out[None]
```

```python
import functools

import numpy as np
import jax
import jax.numpy as jnp
from jax import lax
from jax.experimental import pallas as pl
from jax.experimental.pallas import tpu as pltpu

F32 = jnp.float32
BF16 = jnp.bfloat16
HI = lax.Precision.HIGHEST

D_MODEL = 2048
ATTN_WIDTH = 1024
SSM_WIDTH = 1024
HEAD_DIM = 128
N_HEADS = ATTN_WIDTH // HEAD_DIM
ROT_DIM = HEAD_DIM // 4
ROPE_THETA = 500000.0
BAND = 128
MAX_DIL = 16
SUPER = BAND * MAX_DIL
SSM_GROUP = 16
N_GROUPS = SSM_WIDTH // SSM_GROUP
SSM_STATE = 64
CHUNK = 16
D_FF = 5632
IN_WIDTH = 3 * ATTN_WIDTH + SSM_WIDTH
RMS_EPS = 1e-6
LANES = 128
SUBLANES = 8

TM = 512
TM_CHUNKS = TM // CHUNK

VMEM_LIMIT = 58 * 1024 * 1024


def _cparams(sem):
    return pltpu.CompilerParams(dimension_semantics=sem, vmem_limit_bytes=VMEM_LIMIT)


def _resident(shape):
    zeros = (0,) * len(shape)
    return pl.BlockSpec(shape, lambda *_: zeros, pipeline_mode=pl.Buffered(1))


PERM = CHUNK * CHUNK
N_PERM = TM // PERM


def _piece(blk, t):
    start = blk * PERM + t * CHUNK
    return slice(start, start + CHUNK)


def _chunks(blk):
    return slice(blk * CHUNK, (blk + 1) * CHUNK)


def _tile_positions():
    rho = np.arange(TM)
    r = rho % PERM
    return (rho // PERM) * PERM + CHUNK * (r % CHUNK) + r // CHUNK


def _lane(r):
    return slice(r * LANES, (r + 1) * LANES)


SEGS = LANES // SSM_GROUP


def _transpose_segments(src, seg):
    x = list(src)
    d = SEGS // 2
    while d:
        upper = (seg & d) != 0
        for i in range(SEGS):
            if i & d:
                continue
            a, b = x[i], x[i + d]
            x[i] = jnp.where(upper, pltpu.roll(b, d * SSM_GROUP, 1), a)
            x[i + d] = jnp.where(upper, b, pltpu.roll(a, LANES - d * SSM_GROUP, 1))
        d //= 2
    return x


TN_IN = 512
HEADS_PER_BLK = TN_IN // HEAD_DIM
Q_SCALE = float(HEAD_DIM ** -0.5 * np.log2(np.e))


def _inproj_kernel(x_ref, g_ref, w_ref, rb_ref, ro_ref, rs_ref, perm_ref, qkv_ref, u_ref, hn_ref):
    x = x_ref[...]
    ms = jnp.mean(x * x, axis=-1, keepdims=True)
    hn = (x * lax.rsqrt(ms + RMS_EPS) * g_ref[...]).astype(BF16)
    for blk in range(N_PERM):
        rows = slice(blk * PERM, (blk + 1) * PERM)
        hn_ref[rows, :] = jnp.dot(perm_ref[...], hn[rows, :],
                                  preferred_element_type=F32).astype(BF16)

    cb, sb = rb_ref[0:1, :], rb_ref[1:2, :]
    co, so = ro_ref[0], ro_ref[1]
    cos, sin = cb * co - sb * so, sb * co + cb * so
    sin_hi, sin_lo = sin * rs_ref[0:1, :], sin * rs_ref[1:2, :]
    n_blk, n_qkv = IN_WIDTH // TN_IN, 3 * ATTN_WIDTH // TN_IN
    for j in list(range(n_qkv, n_blk)) + list(range(n_qkv)):
        acc = jnp.dot(hn_ref[...], w_ref[:, j * TN_IN:(j + 1) * TN_IN],
                      preferred_element_type=F32)
        col = j * TN_IN
        if col >= 3 * ATTN_WIDTH:
            seg = lax.broadcasted_iota(jnp.int32, (CHUNK, LANES), 1) // SSM_GROUP
            for lt in range(TN_IN // LANES):
                tile = (col - 3 * ATTN_WIDTH) // LANES + lt
                for blk in range(N_PERM):
                    for th in range(CHUNK // SEGS):
                        src = [acc[_piece(blk, th * SEGS + k), _lane(lt)] for k in range(SEGS)]
                        for gl, folded in enumerate(_transpose_segments(src, seg)):
                            u_ref[tile * SEGS + gl, _chunks(blk), _lane(th)] = folded.astype(BF16)
            continue
        for hh in range(HEADS_PER_BLK):
            r = acc[:, hh * HEAD_DIM:(hh + 1) * HEAD_DIM]
            if col < 2 * ATTN_WIDTH:
                r = (r * cos + pltpu.roll(r, ROT_DIM // 2, 1) * sin_hi
                     + pltpu.roll(r, HEAD_DIM - ROT_DIM // 2, 1) * sin_lo)
            if col < ATTN_WIDTH:
                r = r * Q_SCALE
            r = r.astype(BF16)
            head = j * HEADS_PER_BLK + hh
            for blk in range(N_PERM):
                for t in range(CHUNK):
                    qkv_ref[head, _chunks(blk), _lane(t)] = r[_piece(blk, t), :]


def _rope_tables(s):
    half = ROT_DIM // 2
    freq = np.zeros(HEAD_DIM)
    freq[:ROT_DIM] = np.tile(ROPE_THETA ** (-np.arange(0, ROT_DIM, 2) / ROT_DIM), 2)
    base = (np.arange(s // TM) * TM)[:, None] * freq[None, :]
    off = _tile_positions()[:, None] * freq[None, :]
    signs = np.zeros((2, HEAD_DIM))
    signs[0, half:ROT_DIM] = 1.0
    signs[1, :half] = -1.0
    as_f32 = lambda a: jnp.asarray(a.astype(np.float32))
    return (as_f32(np.stack([np.cos(base), np.sin(base)], axis=1)),
            as_f32(np.stack([np.cos(off), np.sin(off)])), as_f32(signs))


def _block_permutation():
    pos = _tile_positions()[:PERM]
    return (pos[:, None] == np.arange(PERM)[None, :]).astype(np.float32)


def _inproj(x, g, w_bf16, rope):
    s = x.shape[0]
    width = MAX_DIL * HEAD_DIM
    perm = jnp.asarray(_block_permutation(), BF16)
    rope_base, rope_off, rope_signs = rope
    return pl.pallas_call(
        _inproj_kernel,
        grid=(s // TM,),
        in_specs=[
            pl.BlockSpec((TM, D_MODEL), lambda i: (i, 0)),
            _resident((1, D_MODEL)),
            _resident((D_MODEL, IN_WIDTH)),
            pl.BlockSpec((None, 2, HEAD_DIM), lambda i: (i, 0, 0)),
            _resident(rope_off.shape),
            _resident(rope_signs.shape),
            _resident((PERM, PERM)),
        ],
        out_specs=[
            pl.BlockSpec((3 * N_HEADS, TM_CHUNKS, width), lambda i: (0, i, 0)),
            pl.BlockSpec((N_GROUPS, TM_CHUNKS, CHUNK * SSM_GROUP), lambda i: (0, i, 0)),
        ],
        out_shape=[
            jax.ShapeDtypeStruct((3 * N_HEADS, s // MAX_DIL, width), BF16),
            jax.ShapeDtypeStruct((N_GROUPS, s // CHUNK, CHUNK * SSM_GROUP), BF16),
        ],
        scratch_shapes=[pltpu.VMEM((TM, D_MODEL), BF16)],
        compiler_params=_cparams(("parallel",)),
        name="inproj",
    )(x, g, w_bf16, rope_base, rope_off, rope_signs, perm)


def _band_bias(tile, perm_mod, perm_mul):
    rho = np.arange(tile)
    lat = perm_mul * (rho % perm_mod) + rho // perm_mod
    jq = lat[:, None]
    jk = np.concatenate([lat - tile, lat])[None, :]
    dist = jq - jk
    valid = (dist >= 0) & (dist <= BAND)
    normal = np.where(valid, 0.0, -np.inf).astype(np.float32)
    first = np.where(valid & (jk >= 0), 0.0, -np.inf).astype(np.float32)
    return np.stack([normal, first])


def _attn_tile(q, k, v, bias, old):
    n = k.shape[0]
    s = lax.dot_general(q, k, (((1,), (1,)), ((), ())), preferred_element_type=F32) + bias
    mt = jnp.max(s, axis=-1, keepdims=True)
    v1 = jnp.concatenate([v, jnp.ones((n, LANES), BF16)], axis=1)
    if old is None:
        m_new = jnp.broadcast_to(mt, (q.shape[0], LANES))
    else:
        acc_o, m_o, l_o = old
        m_new = jnp.maximum(m_o, mt)
    p = jnp.exp2(s - jnp.concatenate([m_new] * (n // LANES), axis=1))
    pv = jnp.dot(p.astype(BF16), v1, preferred_element_type=F32)
    o, l = pv[:, :HEAD_DIM], pv[:, HEAD_DIM:]
    if old is not None:
        alpha = jnp.exp2(m_o - m_new)
        o = alpha * acc_o + o
        l = alpha * l_o + l
    return o, m_new, l


def _attn_kernel(q_ref, kp_ref, kc_ref, vp_ref, vc_ref, b16_ref, b4_ref, b1_ref,
                 o_ref, acc_ref, m_ref, l_ref):
    first = jnp.where(pl.program_id(1) == 0, 1, 0)

    bias = b16_ref[first]
    for r in range(MAX_DIL):
        k = jnp.concatenate([kp_ref[:, _lane(r)], kc_ref[:, _lane(r)]], axis=0)
        v = jnp.concatenate([vp_ref[:, _lane(r)], vc_ref[:, _lane(r)]], axis=0)
        o, m, l = _attn_tile(q_ref[:, _lane(r)], k, v, bias, None)
        acc_ref[:, _lane(r)] = o
        m_ref[:, _lane(r)] = m
        l_ref[:, _lane(r)] = l

    def run_pattern(dil, rows, final):
        n_c = MAX_DIL // dil
        n_b = BAND // rows
        b_ref = b4_ref if dil == 4 else b1_ref
        for b in range(n_b):
            bias = b_ref[first] if b == 0 else b_ref[0]
            cur = slice(b * rows, (b + 1) * rows)
            prev = slice((b - 1) * rows, b * rows) if b > 0 else slice(BAND - rows, BAND)
            for r in range(dil):
                blocks = [_lane(r + dil * c) for c in range(n_c)]

                def gather(ref, rsl):
                    return jnp.concatenate([ref[rsl, bl] for bl in blocks], axis=0)

                k = jnp.concatenate(
                    [gather(kc_ref if b > 0 else kp_ref, prev), gather(kc_ref, cur)], axis=0)
                v = jnp.concatenate(
                    [gather(vc_ref if b > 0 else vp_ref, prev), gather(vc_ref, cur)], axis=0)
                per = BAND // rows
                for q0 in range(0, n_c, per):
                    qblocks = blocks[q0:q0 + per]

                    def qgather(ref):
                        return jnp.concatenate([ref[cur, bl] for bl in qblocks], axis=0)

                    old = (qgather(acc_ref), qgather(m_ref), qgather(l_ref))
                    o, m, l = _attn_tile(qgather(q_ref), k, v,
                                         bias[q0 * rows:(q0 + per) * rows, :], old)
                    for c, bl in enumerate(qblocks):
                        piece = slice(c * rows, (c + 1) * rows)
                        if final:
                            o_ref[cur, bl] = (o[piece] / l[piece]).astype(o_ref.dtype)
                        else:
                            acc_ref[cur, bl] = o[piece]
                            m_ref[cur, bl] = m[piece]
                            l_ref[cur, bl] = l[piece]

    run_pattern(4, 32, False)
    run_pattern(1, 16, True)


def _attention(qkv):
    rows, width = qkv.shape[1:]
    b16 = jnp.asarray(_band_bias(BAND, BAND, 1))
    b4 = jnp.asarray(_band_bias(BAND, BAND // 4, 4))
    b1 = jnp.asarray(_band_bias(2 * BAND, MAX_DIL, MAX_DIL))
    blk = (None, BAND, width)

    def spec(base, prev):
        if prev:
            return pl.BlockSpec(blk, lambda h, i: (base + h, jnp.maximum(i - 1, 0), 0))
        return pl.BlockSpec(blk, lambda h, i: (base + h, i, 0))

    return pl.pallas_call(
        _attn_kernel,
        grid=(N_HEADS, rows // BAND),
        in_specs=[spec(0, False), spec(N_HEADS, True), spec(N_HEADS, False),
                  spec(2 * N_HEADS, True), spec(2 * N_HEADS, False),
                  _resident(b16.shape), _resident(b4.shape), _resident(b1.shape)],
        out_specs=pl.BlockSpec(blk, lambda h, i: (h, i, 0)),
        out_shape=jax.ShapeDtypeStruct((N_HEADS, rows, width), BF16),
        scratch_shapes=[pltpu.VMEM((BAND, width), F32)] * 3,
        compiler_params=_cparams(("parallel", "parallel")),
        name="dilated_attn",
    )(qkv, qkv, qkv, qkv, qkv, b16, b4, b1)


GROUPS_PER_TILE = SEGS
N_TILES = N_GROUPS // GROUPS_PER_TILE
PAIRS = GROUPS_PER_TILE // 2
STATE_W = GROUPS_PER_TILE * SSM_STATE
FOLD = CHUNK * SSM_GROUP
PAIR_STATE = 2 * SSM_STATE


def _shift_down(x, k, row):
    return jnp.where(row >= k, pltpu.roll(x, k, 0), 0.0)


def _s5_kernel(u_ref, kern_ref, wb_ref, wc_ref, sc_ref, y_ref, h_ref, toe_ref):
    n_rows = u_ref.shape[1]
    lane = lax.broadcasted_iota(jnp.int32, (SSM_GROUP, LANES), 1)
    for g in range(GROUPS_PER_TILE):
        lo, hi = kern_ref[g, :, 0:LANES], kern_ref[g, :, LANES:FOLD]
        for t in range(CHUNK):
            s = (SSM_GROUP * t) % LANES
            rlo = pltpu.roll(lo, s, 1) if s else lo
            if t < SEGS:
                rhi = pltpu.roll(hi, s, 1) if s else hi
                blk = [jnp.where(lane >= s, rlo, 0.0), jnp.where(lane >= s, rhi, rlo)]
            else:
                blk = [jnp.zeros_like(lo), jnp.where(lane >= s, rlo, 0.0)]
            toe_ref[g, t * SSM_GROUP:(t + 1) * SSM_GROUP, :] = jnp.concatenate(
                blk, axis=1).astype(BF16)
    for k in range(PAIRS):
        e = (jnp.dot(u_ref[2 * k], wb_ref[2 * k], preferred_element_type=F32)
             + jnp.dot(u_ref[2 * k + 1], wb_ref[2 * k + 1], preferred_element_type=F32))
        h_ref[:, _lane(k)] = e[:, :PAIR_STATE]
        h_ref[:, STATE_W + k * PAIR_STATE:STATE_W + (k + 1) * PAIR_STATE] = e[:, PAIR_STATE:]

    row = lax.broadcasted_iota(jnp.int32, (SUBLANES, STATE_W), 0)

    def block(b, carry):
        cr, ci = carry
        r0 = pl.multiple_of(b * SUBLANES, SUBLANES)
        xr = h_ref[pl.ds(r0, SUBLANES), 0:STATE_W]
        xi = h_ref[pl.ds(r0, SUBLANES), STATE_W:2 * STATE_W]
        for i in range(3):
            kr = sc_ref[16 + 2 * i:17 + 2 * i, :]
            ki = sc_ref[17 + 2 * i:18 + 2 * i, :]
            sr, si = _shift_down(xr, 1 << i, row), _shift_down(xi, 1 << i, row)
            xr, xi = xr + (kr * sr - ki * si), xi + (kr * si + ki * sr)
        pr, pi = sc_ref[0:8, :], sc_ref[8:16, :]
        hr = xr + (pr * cr - pi * ci)
        hi = xi + (pr * ci + pi * cr)
        h_ref[pl.ds(r0, SUBLANES), 0:STATE_W] = jnp.where(row >= 1, pltpu.roll(hr, 1, 0), cr)
        h_ref[pl.ds(r0, SUBLANES), STATE_W:2 * STATE_W] = jnp.where(
            row >= 1, pltpu.roll(hi, 1, 0), ci)
        return hr[SUBLANES - 1:SUBLANES, :], hi[SUBLANES - 1:SUBLANES, :]

    zero = jnp.zeros((1, STATE_W), F32)
    lax.fori_loop(0, n_rows // SUBLANES, block, (zero, zero))

    for k in range(PAIRS):
        hin = jnp.concatenate(
            [h_ref[:, _lane(k)],
             h_ref[:, STATE_W + k * PAIR_STATE:STATE_W + (k + 1) * PAIR_STATE]],
            axis=1).astype(BF16)
        for s in range(2):
            g = 2 * k + s
            y = (jnp.dot(u_ref[g], toe_ref[g], preferred_element_type=F32)
                 + jnp.dot(hin, wc_ref[g], preferred_element_type=F32))
            y_ref[g] = y.astype(y_ref.dtype)


def _s5_weights(a_re, a_im, log_dt, b_re, b_im, c_re, c_im, d_skip):
    g, n, p = N_GROUPS, SSM_STATE, SSM_GROUP
    nt, gl = N_TILES, GROUPS_PER_TILE
    ar, ai = a_re.astype(F32), a_im.astype(F32)
    dt = jnp.exp(log_dt.astype(F32))[:, None]

    def apow(ks):
        k = jnp.asarray(ks, F32)[:, None, None]
        mag, ph = jnp.exp(ar * dt * k), ai * dt * k
        return mag * jnp.cos(ph), mag * jnp.sin(ph)

    pwr, pwi = apow(np.arange(CHUNK + 1))
    abr, abi = pwr[1], pwi[1]
    nr, ni, den = abr - 1.0, abi, ar * ar + ai * ai
    fr, fi = (nr * ar + ni * ai) / den, (ni * ar - nr * ai) / den
    bre = b_re.astype(F32).transpose(0, 2, 1)
    bim = b_im.astype(F32).transpose(0, 2, 1)
    bbr = fr[:, None, :] * bre - fi[:, None, :] * bim
    bbi = fr[:, None, :] * bim + fi[:, None, :] * bre
    cre, cim = c_re.astype(F32), c_im.astype(F32)

    pr0, pi0 = pwr[:CHUNK].transpose(1, 0, 2)[:, :, None, :], pwi[:CHUNK].transpose(1, 0, 2)[:, :, None, :]
    ca0 = jnp.concatenate([cre[:, None] * pr0 - cim[:, None] * pi0,
                           cre[:, None] * pi0 + cim[:, None] * pr0], axis=-1)
    bb2 = jnp.concatenate([bbr, -bbi], axis=-1)
    kern = jnp.einsum('gqk,gak->gqa', bb2, ca0.reshape(g, CHUNK * p, 2 * n), precision=HI)
    skip = jnp.eye(p, dtype=F32)[None] * d_skip.astype(F32)[:, None, :]
    kern = kern + jnp.pad(skip, ((0, 0), (0, 0), (0, FOLD - p)))

    def dup(z):
        return jnp.concatenate([z, z], axis=-1)

    odd = np.arange(g) % 2 == 1
    own = jnp.asarray((np.arange(LANES) >= n)[None, :] == odd[:, None])
    er = dup(pwr[CHUNK - 1::-1]).transpose(1, 0, 2)[:, :, None, :]
    ei = dup(pwi[CHUNK - 1::-1]).transpose(1, 0, 2)[:, :, None, :]
    br, bi, own4 = dup(bbr)[:, None], dup(bbi)[:, None], own[:, None, None, :]
    wb = jnp.concatenate([jnp.where(own4, er * br - ei * bi, 0.0),
                          jnp.where(own4, er * bi + ei * br, 0.0)], axis=-1)
    wb = wb.reshape(g, FOLD, 2 * LANES)
    cre_f = jnp.tile(cre.transpose(0, 2, 1), (1, 1, CHUNK))
    cim_f = jnp.tile(cim.transpose(0, 2, 1), (1, 1, CHUNK))
    ar1 = jnp.repeat(pwr[1:].transpose(1, 2, 0), p, axis=2)
    ai1 = jnp.repeat(pwi[1:].transpose(1, 2, 0), p, axis=2)
    car, cai = cre_f * ar1 - cim_f * ai1, cre_f * ai1 + cim_f * ar1
    zero = jnp.zeros_like(car)
    wc = jnp.where(jnp.asarray(odd)[:, None, None],
                   jnp.concatenate([zero, car, zero, -cai], axis=1),
                   jnp.concatenate([car, zero, -cai, zero], axis=1))

    def tile_lanes(z):
        return z.reshape(z.shape[0], nt, gl * n).transpose(1, 0, 2)

    cyr, cyi = apow(CHUNK * (np.arange(SUBLANES) + 1))
    str_, sti = apow(CHUNK * (1 << np.arange(3)))
    step_rows = jnp.stack([str_, sti], axis=1).reshape(6, g, n)
    sc = jnp.concatenate([tile_lanes(cyr), tile_lanes(cyi), tile_lanes(step_rows),
                          jnp.zeros((nt, 2, STATE_W), F32)], axis=1)
    return (kern, wb.astype(BF16), wc.astype(BF16)), sc


def _s5_core(u2, weights, sc):
    _, rows, _ = u2.shape
    blk = pl.BlockSpec((GROUPS_PER_TILE, rows, FOLD), lambda j: (j, 0, 0))
    w_spec = pl.BlockSpec((GROUPS_PER_TILE, FOLD, FOLD), lambda j: (j, 0, 0))
    return pl.pallas_call(
        _s5_kernel,
        grid=(N_TILES,),
        in_specs=[
            blk,
            pl.BlockSpec((GROUPS_PER_TILE, SSM_GROUP, FOLD), lambda j: (j, 0, 0)),
            w_spec, w_spec,
            pl.BlockSpec((None, 3 * SUBLANES, STATE_W), lambda j: (j, 0, 0)),
        ],
        out_specs=blk,
        out_shape=jax.ShapeDtypeStruct(u2.shape, BF16),
        scratch_shapes=[pltpu.VMEM((rows, 2 * STATE_W), F32),
                        pltpu.VMEM((GROUPS_PER_TILE, FOLD, FOLD), BF16)],
        compiler_params=_cparams(("parallel",)),
        name="s5_scan",
    )(u2, *weights, sc)


def _outproj_kernel(x_ref, a_ref, y_ref, wglu_ref, bglu_ref, wout_ref, g_ref, unperm_ref,
                    h_ref, hn_ref):
    seg = lax.broadcasted_iota(jnp.int32, (CHUNK, LANES), 1) // SSM_GROUP
    for blk in range(N_PERM):
        rows = slice(blk * PERM, (blk + 1) * PERM)
        y_t = [[] for _ in range(CHUNK)]
        for tile in range(N_GROUPS // SEGS):
            for th in range(CHUNK // SEGS):
                src = [y_ref[tile * SEGS + gl, _chunks(blk), _lane(th)].astype(F32)
                       for gl in range(SEGS)]
                for k, unfolded in enumerate(_transpose_segments(src, seg)):
                    y_t[th * SEGS + k].append(unfolded)
        y = jnp.concatenate([jnp.concatenate(parts, axis=1) for parts in y_t], axis=0)
        y = jax.nn.gelu(y)
        gate = jax.nn.sigmoid(
            jnp.dot(y.astype(BF16), wglu_ref[...], preferred_element_type=F32) + bglu_ref[...])
        ssm = (y * gate).astype(BF16)
        attn = jnp.concatenate(
            [jnp.concatenate([a_ref[hh, _chunks(blk), _lane(t)] for t in range(CHUNK)], axis=0)
             for hh in range(N_HEADS)], axis=1)
        mix = jnp.concatenate([attn, ssm], axis=1)
        mix = jnp.dot(unperm_ref[...], mix, preferred_element_type=F32).astype(BF16)
        h = x_ref[rows, :] + jnp.dot(mix, wout_ref[...], preferred_element_type=F32)
        h_ref[rows, :] = h
        ms = jnp.mean(h * h, axis=-1, keepdims=True)
        hn_ref[rows, :] = (h * lax.rsqrt(ms + RMS_EPS) * g_ref[...]).astype(BF16)


def _outproj(x, attn, y, w_glu, b_glu, w_out, g2):
    s = x.shape[0]
    width = MAX_DIL * HEAD_DIM
    unperm = jnp.asarray(_block_permutation(), BF16)
    return pl.pallas_call(
        _outproj_kernel,
        grid=(s // TM,),
        in_specs=[
            pl.BlockSpec((TM, D_MODEL), lambda i: (i, 0)),
            pl.BlockSpec((N_HEADS, TM_CHUNKS, width), lambda i: (0, i, 0)),
            pl.BlockSpec((N_GROUPS, TM_CHUNKS, FOLD), lambda i: (0, i, 0)),
            _resident((SSM_WIDTH, SSM_WIDTH)),
            _resident((1, SSM_WIDTH)),
            _resident((D_MODEL, D_MODEL)),
            _resident((1, D_MODEL)),
            _resident((PERM, PERM)),
        ],
        out_specs=[pl.BlockSpec((TM, D_MODEL), lambda i: (i, 0))] * 2,
        out_shape=[jax.ShapeDtypeStruct((s, D_MODEL), F32),
                   jax.ShapeDtypeStruct((s, D_MODEL), BF16)],
        compiler_params=_cparams(("parallel",)),
        name="outproj",
    )(x, attn, y, w_glu, b_glu, w_out, g2, unperm)


TM_FFN = 1024
TF_FFN = 512
N_F = D_FF // TF_FFN
LOAD_AT = N_F // 2


def _ffn_kernel(n_tiles, hn_ref, h_hbm, wg_ref, wu_ref, wd_ref, g_ref, o_hbm,
                acc_ref, act_a, act_b, sem_in, sem_out):
    k = pl.program_id(0)
    f_prev = lax.rem(k + (N_F - 1), N_F)
    tile = jnp.maximum(k - 1, 0) // N_F
    slot = lax.rem(tile, 2)

    def rows(j):
        return pl.ds(pl.multiple_of(j * TM_FFN, TM_FFN), TM_FFN)

    def copy_in(j, s):
        return pltpu.make_async_copy(h_hbm.at[rows(j), :], acc_ref.at[s], sem_in.at[s])

    def copy_out(j, s):
        return pltpu.make_async_copy(acc_ref.at[s], o_hbm.at[rows(j), :], sem_out.at[s])

    @pl.when(k == 0)
    def _():
        act_b[...] = jnp.zeros_like(act_b)
        acc_ref[0] = jnp.zeros((TM_FFN, D_MODEL), F32)

    @pl.when(f_prev == 0)
    def _():
        copy_in(tile, slot).wait()

    def step(act_prev, act_next):
        part = jnp.dot(act_prev[...], wd_ref[...].astype(BF16), preferred_element_type=F32)
        hn = hn_ref[...]
        gate = jnp.dot(hn, wg_ref[...].astype(BF16), preferred_element_type=F32)
        up = jnp.dot(hn, wu_ref[...].astype(BF16), preferred_element_type=F32)
        act_next[...] = (jax.nn.silu(gate) * up).astype(BF16)
        acc_ref[slot] = acc_ref[slot] + part

    @pl.when(lax.rem(k, 2) == 0)
    def _():
        step(act_b, act_a)

    @pl.when(lax.rem(k, 2) == 1)
    def _():
        step(act_a, act_b)

    @pl.when(k == 0)
    def _():
        copy_in(0, 0).start()

    @pl.when((f_prev == LOAD_AT) & (tile >= 1))
    def _():
        copy_out(tile - 1, 1 - slot).wait()

    @pl.when((f_prev == LOAD_AT) & (tile + 1 < n_tiles))
    def _():
        copy_in(tile + 1, 1 - slot).start()

    @pl.when((f_prev == N_F - 1) & (k > 0))
    def _():
        h = acc_ref[slot]
        ms = jnp.mean(h * h, axis=-1, keepdims=True)
        acc_ref[slot] = h * lax.rsqrt(ms + RMS_EPS) * g_ref[...]
        copy_out(tile, slot).start()

    @pl.when(k == pl.num_programs(0) - 1)
    def _():
        copy_out(tile, slot).wait()


def _ffn(hn, h, w_gate, w_up, w_down, g):
    s = h.shape[0]
    n_i = s // TM_FFN
    return pl.pallas_call(
        functools.partial(_ffn_kernel, n_i),
        grid=(n_i * N_F + 1,),
        in_specs=[
            pl.BlockSpec((TM_FFN, D_MODEL), lambda k: (jnp.minimum(k // N_F, n_i - 1), 0)),
            pl.BlockSpec(memory_space=pl.ANY),
            pl.BlockSpec((D_MODEL, TF_FFN), lambda k: (0, k % N_F)),
            pl.BlockSpec((D_MODEL, TF_FFN), lambda k: (0, k % N_F)),
            pl.BlockSpec((TF_FFN, D_MODEL), lambda k: (jnp.maximum(k - 1, 0) % N_F, 0)),
            _resident((1, D_MODEL)),
        ],
        out_specs=pl.BlockSpec(memory_space=pl.ANY),
        out_shape=jax.ShapeDtypeStruct((s, D_MODEL), F32),
        scratch_shapes=[pltpu.VMEM((2, TM_FFN, D_MODEL), F32),
                        pltpu.VMEM((TM_FFN, TF_FFN), BF16),
                        pltpu.VMEM((TM_FFN, TF_FFN), BF16),
                        pltpu.SemaphoreType.DMA((2,)),
                        pltpu.SemaphoreType.DMA((2,))],
        compiler_params=_cparams(("arbitrary",)),
        name="ffn",
    )(hn, h, w_gate, w_up, w_down, g)


def kernel(x, norm1_g, w_in, a_re, a_im, log_dt, b_re, b_im, c_re, c_im, d_skip, w_glu, b_glu,
           w_out, norm2_g, w_gate, w_up, w_down, final_g):
    b, s, _ = x.shape
    assert b == 1 and s % SUPER == 0 and w_in.shape[0] == 1
    x2 = x[0]
    qkv, u = _inproj(x2, norm1_g[0][None, :], w_in[0].astype(BF16), _rope_tables(s))
    attn = _attention(qkv)
    s5_w, sc = _s5_weights(a_re[0], a_im[0], log_dt[0], b_re[0], b_im[0], c_re[0], c_im[0],
                           d_skip[0])
    y = _s5_core(u, s5_w, sc)
    h, hn = _outproj(x2, attn, y, w_glu[0].astype(BF16), b_glu[0][None, :].astype(F32),
                     w_out[0].astype(BF16), norm2_g[0][None, :])
    out = _ffn(hn, h, w_gate[0], w_up[0], w_down[0], final_g[None, :])
    return out[None]
```

```python
import functools

import numpy as np
import jax
import jax.numpy as jnp
from jax import lax
from jax.experimental import pallas as pl
from jax.experimental.pallas import tpu as pltpu

F32 = jnp.float32
BF16 = jnp.bfloat16
HI = lax.Precision.HIGHEST

D_MODEL = 2048
ATTN_WIDTH = 1024
SSM_WIDTH = 1024
HEAD_DIM = 128
N_HEADS = ATTN_WIDTH // HEAD_DIM
ROT_DIM = HEAD_DIM // 4
ROPE_THETA = 500000.0
BAND = 128
MAX_DIL = 16
SUPER = BAND * MAX_DIL
SSM_GROUP = 16
N_GROUPS = SSM_WIDTH // SSM_GROUP
SSM_STATE = 64
CHUNK = 16
D_FF = 5632
IN_WIDTH = 3 * ATTN_WIDTH + SSM_WIDTH
RMS_EPS = 1e-6
LANES = 128
SUBLANES = 8

TM = 512
TM_CHUNKS = TM // CHUNK

VMEM_LIMIT = 58 * 1024 * 1024


def _cparams(sem):
    return pltpu.CompilerParams(dimension_semantics=sem, vmem_limit_bytes=VMEM_LIMIT)


def _resident(shape):
    zeros = (0,) * len(shape)
    return pl.BlockSpec(shape, lambda *_: zeros, pipeline_mode=pl.Buffered(1))


PERM = CHUNK * CHUNK
N_PERM = TM // PERM


def _piece(blk, t):
    start = blk * PERM + t * CHUNK
    return slice(start, start + CHUNK)


def _chunks(blk):
    return slice(blk * CHUNK, (blk + 1) * CHUNK)


def _tile_positions():
    rho = np.arange(TM)
    r = rho % PERM
    return (rho // PERM) * PERM + CHUNK * (r % CHUNK) + r // CHUNK


def _lane(r):
    return slice(r * LANES, (r + 1) * LANES)


SEGS = LANES // SSM_GROUP


def _transpose_segments(src, seg):
    x = list(src)
    d = SEGS // 2
    while d:
        upper = (seg & d) != 0
        for i in range(SEGS):
            if i & d:
                continue
            a, b = x[i], x[i + d]
            x[i] = jnp.where(upper, pltpu.roll(b, d * SSM_GROUP, 1), a)
            x[i + d] = jnp.where(upper, b, pltpu.roll(a, LANES - d * SSM_GROUP, 1))
        d //= 2
    return x


TN_IN = 512
HEADS_PER_BLK = TN_IN // HEAD_DIM
Q_SCALE = float(HEAD_DIM ** -0.5 * np.log2(np.e))


def _inproj_kernel(x_ref, g_ref, w_ref, rb_ref, ro_ref, rs_ref, perm_ref, qkv_ref, u_ref, hn_ref):
    x = x_ref[...]
    ms = jnp.mean(x * x, axis=-1, keepdims=True)
    hn = (x * lax.rsqrt(ms + RMS_EPS) * g_ref[...]).astype(BF16)
    for blk in range(N_PERM):
        rows = slice(blk * PERM, (blk + 1) * PERM)
        hn_ref[rows, :] = jnp.dot(perm_ref[...], hn[rows, :],
                                  preferred_element_type=F32).astype(BF16)

    cb, sb = rb_ref[0:1, :], rb_ref[1:2, :]
    co, so = ro_ref[0], ro_ref[1]
    cos, sin = cb * co - sb * so, sb * co + cb * so
    sin_hi, sin_lo = sin * rs_ref[0:1, :], sin * rs_ref[1:2, :]
    n_blk, n_qkv = IN_WIDTH // TN_IN, 3 * ATTN_WIDTH // TN_IN
    for j in list(range(n_qkv, n_blk)) + list(range(n_qkv)):
        acc = jnp.dot(hn_ref[...], w_ref[:, j * TN_IN:(j + 1) * TN_IN],
                      preferred_element_type=F32)
        col = j * TN_IN
        if col >= 3 * ATTN_WIDTH:
            seg = lax.broadcasted_iota(jnp.int32, (CHUNK, LANES), 1) // SSM_GROUP
            for lt in range(TN_IN // LANES):
                tile = (col - 3 * ATTN_WIDTH) // LANES + lt
                for blk in range(N_PERM):
                    for th in range(CHUNK // SEGS):
                        src = [acc[_piece(blk, th * SEGS + k), _lane(lt)] for k in range(SEGS)]
                        for gl, folded in enumerate(_transpose_segments(src, seg)):
                            u_ref[tile * SEGS + gl, _chunks(blk), _lane(th)] = folded.astype(BF16)
            continue
        for hh in range(HEADS_PER_BLK):
            r = acc[:, hh * HEAD_DIM:(hh + 1) * HEAD_DIM]
            if col < 2 * ATTN_WIDTH:
                r = (r * cos + pltpu.roll(r, ROT_DIM // 2, 1) * sin_hi
                     + pltpu.roll(r, HEAD_DIM - ROT_DIM // 2, 1) * sin_lo)
            if col < ATTN_WIDTH:
                r = r * Q_SCALE
            r = r.astype(BF16)
            head = j * HEADS_PER_BLK + hh
            for blk in range(N_PERM):
                for t in range(CHUNK):
                    qkv_ref[head, _chunks(blk), _lane(t)] = r[_piece(blk, t), :]


def _rope_tables(s):
    half = ROT_DIM // 2
    freq = np.zeros(HEAD_DIM)
    freq[:ROT_DIM] = np.tile(ROPE_THETA ** (-np.arange(0, ROT_DIM, 2) / ROT_DIM), 2)
    base = (np.arange(s // TM) * TM)[:, None] * freq[None, :]
    off = _tile_positions()[:, None] * freq[None, :]
    signs = np.zeros((2, HEAD_DIM))
    signs[0, half:ROT_DIM] = 1.0
    signs[1, :half] = -1.0
    as_f32 = lambda a: jnp.asarray(a.astype(np.float32))
    return (as_f32(np.stack([np.cos(base), np.sin(base)], axis=1)),
            as_f32(np.stack([np.cos(off), np.sin(off)])), as_f32(signs))


def _block_permutation():
    pos = _tile_positions()[:PERM]
    return (pos[:, None] == np.arange(PERM)[None, :]).astype(np.float32)


def _inproj(x, g, w_bf16, rope):
    s = x.shape[0]
    width = MAX_DIL * HEAD_DIM
    perm = jnp.asarray(_block_permutation(), BF16)
    rope_base, rope_off, rope_signs = rope
    return pl.pallas_call(
        _inproj_kernel,
        grid=(s // TM,),
        in_specs=[
            pl.BlockSpec((TM, D_MODEL), lambda i: (i, 0)),
            _resident((1, D_MODEL)),
            _resident((D_MODEL, IN_WIDTH)),
            pl.BlockSpec((None, 2, HEAD_DIM), lambda i: (i, 0, 0)),
            _resident(rope_off.shape),
            _resident(rope_signs.shape),
            _resident((PERM, PERM)),
        ],
        out_specs=[
            pl.BlockSpec((3 * N_HEADS, TM_CHUNKS, width), lambda i: (0, i, 0)),
            pl.BlockSpec((N_GROUPS, TM_CHUNKS, CHUNK * SSM_GROUP), lambda i: (0, i, 0)),
        ],
        out_shape=[
            jax.ShapeDtypeStruct((3 * N_HEADS, s // MAX_DIL, width), BF16),
            jax.ShapeDtypeStruct((N_GROUPS, s // CHUNK, CHUNK * SSM_GROUP), BF16),
        ],
        scratch_shapes=[pltpu.VMEM((TM, D_MODEL), BF16)],
        compiler_params=_cparams(("parallel",)),
        name="inproj",
    )(x, g, w_bf16, rope_base, rope_off, rope_signs, perm)


def _band_bias(tile, perm_mod, perm_mul):
    rho = np.arange(tile)
    lat = perm_mul * (rho % perm_mod) + rho // perm_mod
    jq = lat[:, None]
    jk = np.concatenate([lat - tile, lat])[None, :]
    dist = jq - jk
    valid = (dist >= 0) & (dist <= BAND)
    normal = np.where(valid, 0.0, -np.inf).astype(np.float32)
    first = np.where(valid & (jk >= 0), 0.0, -np.inf).astype(np.float32)
    return np.stack([normal, first])


def _attn_tile(q, k, v, bias, old):
    n = k.shape[0]
    s = lax.dot_general(q, k, (((1,), (1,)), ((), ())), preferred_element_type=F32) + bias
    mt = jnp.max(s, axis=-1, keepdims=True)
    v1 = jnp.concatenate([v, jnp.ones((n, LANES), BF16)], axis=1)
    if old is None:
        m_new = jnp.broadcast_to(mt, (q.shape[0], LANES))
    else:
        acc_o, m_o, l_o = old
        m_new = jnp.maximum(m_o, mt)
    p = jnp.exp2(s - jnp.concatenate([m_new] * (n // LANES), axis=1))
    pv = jnp.dot(p.astype(BF16), v1, preferred_element_type=F32)
    o, l = pv[:, :HEAD_DIM], pv[:, HEAD_DIM:]
    if old is not None:
        alpha = jnp.exp2(m_o - m_new)
        o = alpha * acc_o + o
        l = alpha * l_o + l
    return o, m_new, l


def _attn_kernel(q_ref, kp_ref, kc_ref, vp_ref, vc_ref, b16_ref, b4_ref, b1_ref,
                 o_ref, acc_ref, m_ref, l_ref):
    first = jnp.where(pl.program_id(1) == 0, 1, 0)

    bias = b16_ref[first]
    for r in range(MAX_DIL):
        k = jnp.concatenate([kp_ref[:, _lane(r)], kc_ref[:, _lane(r)]], axis=0)
        v = jnp.concatenate([vp_ref[:, _lane(r)], vc_ref[:, _lane(r)]], axis=0)
        o, m, l = _attn_tile(q_ref[:, _lane(r)], k, v, bias, None)
        acc_ref[:, _lane(r)] = o
        m_ref[:, _lane(r)] = m
        l_ref[:, _lane(r)] = l

    def run_pattern(dil, rows, final):
        n_c = MAX_DIL // dil
        n_b = BAND // rows
        b_ref = b4_ref if dil == 4 else b1_ref
        for b in range(n_b):
            bias = b_ref[first] if b == 0 else b_ref[0]
            cur = slice(b * rows, (b + 1) * rows)
            prev = slice((b - 1) * rows, b * rows) if b > 0 else slice(BAND - rows, BAND)
            for r in range(dil):
                blocks = [_lane(r + dil * c) for c in range(n_c)]

                def gather(ref, rsl):
                    return jnp.concatenate([ref[rsl, bl] for bl in blocks], axis=0)

                k = jnp.concatenate(
                    [gather(kc_ref if b > 0 else kp_ref, prev), gather(kc_ref, cur)], axis=0)
                v = jnp.concatenate(
                    [gather(vc_ref if b > 0 else vp_ref, prev), gather(vc_ref, cur)], axis=0)
                per = BAND // rows
                for q0 in range(0, n_c, per):
                    qblocks = blocks[q0:q0 + per]

                    def qgather(ref):
                        return jnp.concatenate([ref[cur, bl] for bl in qblocks], axis=0)

                    old = (qgather(acc_ref), qgather(m_ref), qgather(l_ref))
                    o, m, l = _attn_tile(qgather(q_ref), k, v,
                                         bias[q0 * rows:(q0 + per) * rows, :], old)
                    for c, bl in enumerate(qblocks):
                        piece = slice(c * rows, (c + 1) * rows)
                        if final:
                            o_ref[cur, bl] = (o[piece] / l[piece]).astype(o_ref.dtype)
                        else:
                            acc_ref[cur, bl] = o[piece]
                            m_ref[cur, bl] = m[piece]
                            l_ref[cur, bl] = l[piece]

    run_pattern(4, 32, False)
    run_pattern(1, 16, True)


def _attention(qkv):
    rows, width = qkv.shape[1:]
    b16 = jnp.asarray(_band_bias(BAND, BAND, 1))
    b4 = jnp.asarray(_band_bias(BAND, BAND // 4, 4))
    b1 = jnp.asarray(_band_bias(2 * BAND, MAX_DIL, MAX_DIL))
    blk = (None, BAND, width)

    def spec(base, prev):
        if prev:
            return pl.BlockSpec(blk, lambda h, i: (base + h, jnp.maximum(i - 1, 0), 0))
        return pl.BlockSpec(blk, lambda h, i: (base + h, i, 0))

    return pl.pallas_call(
        _attn_kernel,
        grid=(N_HEADS, rows // BAND),
        in_specs=[spec(0, False), spec(N_HEADS, True), spec(N_HEADS, False),
                  spec(2 * N_HEADS, True), spec(2 * N_HEADS, False),
                  _resident(b16.shape), _resident(b4.shape), _resident(b1.shape)],
        out_specs=pl.BlockSpec(blk, lambda h, i: (h, i, 0)),
        out_shape=jax.ShapeDtypeStruct((N_HEADS, rows, width), BF16),
        scratch_shapes=[pltpu.VMEM((BAND, width), F32)] * 3,
        compiler_params=_cparams(("parallel", "parallel")),
        name="dilated_attn",
    )(qkv, qkv, qkv, qkv, qkv, b16, b4, b1)


GROUPS_PER_TILE = SEGS
N_TILES = N_GROUPS // GROUPS_PER_TILE
PAIRS = GROUPS_PER_TILE // 2
STATE_W = GROUPS_PER_TILE * SSM_STATE
FOLD = CHUNK * SSM_GROUP
PAIR_STATE = 2 * SSM_STATE


def _s5_kernel(u_ref, kern_ref, wb_ref, wc_ref, sc_ref, y_ref, h_ref, toe_ref):
    n_rows = u_ref.shape[1]
    lane = lax.broadcasted_iota(jnp.int32, (SSM_GROUP, LANES), 1)
    for g in range(GROUPS_PER_TILE):
        lo, hi = kern_ref[g, :, 0:LANES], kern_ref[g, :, LANES:FOLD]
        for t in range(CHUNK):
            s = (SSM_GROUP * t) % LANES
            rlo = pltpu.roll(lo, s, 1) if s else lo
            if t < SEGS:
                rhi = pltpu.roll(hi, s, 1) if s else hi
                blk = [jnp.where(lane >= s, rlo, 0.0), jnp.where(lane >= s, rhi, rlo)]
            else:
                blk = [jnp.zeros_like(lo), jnp.where(lane >= s, rlo, 0.0)]
            toe_ref[g, t * SSM_GROUP:(t + 1) * SSM_GROUP, :] = jnp.concatenate(
                blk, axis=1).astype(BF16)
    for k in range(PAIRS):
        e = (jnp.dot(u_ref[2 * k], wb_ref[2 * k], preferred_element_type=F32)
             + jnp.dot(u_ref[2 * k + 1], wb_ref[2 * k + 1], preferred_element_type=F32))
        h_ref[:, _lane(k)] = e[:, :PAIR_STATE]
        h_ref[:, STATE_W + k * PAIR_STATE:STATE_W + (k + 1) * PAIR_STATE] = e[:, PAIR_STATE:]

    row = lax.broadcasted_iota(jnp.int32, (SUBLANES, STATE_W), 0)

    def block(b, carry):
        cr, ci = carry
        r0 = pl.multiple_of(b * SUBLANES, SUBLANES)
        xr = h_ref[pl.ds(r0, SUBLANES), 0:STATE_W]
        xi = h_ref[pl.ds(r0, SUBLANES), STATE_W:2 * STATE_W]
        for i in range(3):
            base = 2 * SUBLANES * (i + 1)
            kr = sc_ref[base:base + SUBLANES, :]
            ki = sc_ref[base + SUBLANES:base + 2 * SUBLANES, :]
            sr, si = pltpu.roll(xr, 1 << i, 0), pltpu.roll(xi, 1 << i, 0)
            xr, xi = xr + (kr * sr - ki * si), xi + (kr * si + ki * sr)
        pr, pi = sc_ref[0:8, :], sc_ref[8:16, :]
        hr = xr + (pr * cr - pi * ci)
        hi = xi + (pr * ci + pi * cr)
        h_ref[pl.ds(r0, SUBLANES), 0:STATE_W] = jnp.where(row >= 1, pltpu.roll(hr, 1, 0), cr)
        h_ref[pl.ds(r0, SUBLANES), STATE_W:2 * STATE_W] = jnp.where(
            row >= 1, pltpu.roll(hi, 1, 0), ci)
        return hr[SUBLANES - 1:SUBLANES, :], hi[SUBLANES - 1:SUBLANES, :]

    zero = jnp.zeros((1, STATE_W), F32)
    lax.fori_loop(0, n_rows // SUBLANES, block, (zero, zero), unroll=4)

    for k in range(PAIRS):
        hin = jnp.concatenate(
            [h_ref[:, _lane(k)],
             h_ref[:, STATE_W + k * PAIR_STATE:STATE_W + (k + 1) * PAIR_STATE]],
            axis=1).astype(BF16)
        for s in range(2):
            g = 2 * k + s
            y = (jnp.dot(u_ref[g], toe_ref[g], preferred_element_type=F32)
                 + jnp.dot(hin, wc_ref[g], preferred_element_type=F32))
            y_ref[g] = y.astype(y_ref.dtype)


def _s5_weights(a_re, a_im, log_dt, b_re, b_im, c_re, c_im, d_skip):
    g, n, p = N_GROUPS, SSM_STATE, SSM_GROUP
    nt, gl = N_TILES, GROUPS_PER_TILE
    ar, ai = a_re.astype(F32), a_im.astype(F32)
    dt = jnp.exp(log_dt.astype(F32))[:, None]

    def apow(ks):
        k = jnp.asarray(ks, F32)[:, None, None]
        mag, ph = jnp.exp(ar * dt * k), ai * dt * k
        return mag * jnp.cos(ph), mag * jnp.sin(ph)

    pwr, pwi = apow(np.arange(CHUNK + 1))
    abr, abi = pwr[1], pwi[1]
    nr, ni, den = abr - 1.0, abi, ar * ar + ai * ai
    fr, fi = (nr * ar + ni * ai) / den, (ni * ar - nr * ai) / den
    bre = b_re.astype(F32).transpose(0, 2, 1)
    bim = b_im.astype(F32).transpose(0, 2, 1)
    bbr = fr[:, None, :] * bre - fi[:, None, :] * bim
    bbi = fr[:, None, :] * bim + fi[:, None, :] * bre
    cre, cim = c_re.astype(F32), c_im.astype(F32)

    pr0, pi0 = pwr[:CHUNK].transpose(1, 0, 2)[:, :, None, :], pwi[:CHUNK].transpose(1, 0, 2)[:, :, None, :]
    ca0 = jnp.concatenate([cre[:, None] * pr0 - cim[:, None] * pi0,
                           cre[:, None] * pi0 + cim[:, None] * pr0], axis=-1)
    bb2 = jnp.concatenate([bbr, -bbi], axis=-1)
    kern = jnp.einsum('gqk,gak->gqa', bb2, ca0.reshape(g, CHUNK * p, 2 * n), precision=HI)
    skip = jnp.eye(p, dtype=F32)[None] * d_skip.astype(F32)[:, None, :]
    kern = kern + jnp.pad(skip, ((0, 0), (0, 0), (0, FOLD - p)))

    def dup(z):
        return jnp.concatenate([z, z], axis=-1)

    odd = np.arange(g) % 2 == 1
    own = jnp.asarray((np.arange(LANES) >= n)[None, :] == odd[:, None])
    er = dup(pwr[CHUNK - 1::-1]).transpose(1, 0, 2)[:, :, None, :]
    ei = dup(pwi[CHUNK - 1::-1]).transpose(1, 0, 2)[:, :, None, :]
    br, bi, own4 = dup(bbr)[:, None], dup(bbi)[:, None], own[:, None, None, :]
    wb = jnp.concatenate([jnp.where(own4, er * br - ei * bi, 0.0),
                          jnp.where(own4, er * bi + ei * br, 0.0)], axis=-1)
    wb = wb.reshape(g, FOLD, 2 * LANES)
    cre_f = jnp.tile(cre.transpose(0, 2, 1), (1, 1, CHUNK))
    cim_f = jnp.tile(cim.transpose(0, 2, 1), (1, 1, CHUNK))
    ar1 = jnp.repeat(pwr[1:].transpose(1, 2, 0), p, axis=2)
    ai1 = jnp.repeat(pwi[1:].transpose(1, 2, 0), p, axis=2)
    car, cai = cre_f * ar1 - cim_f * ai1, cre_f * ai1 + cim_f * ar1
    zero = jnp.zeros_like(car)
    wc = jnp.where(jnp.asarray(odd)[:, None, None],
                   jnp.concatenate([zero, car, zero, -cai], axis=1),
                   jnp.concatenate([car, zero, -cai, zero], axis=1))

    def tile_lanes(z):
        return z.reshape(z.shape[0], nt, gl * n).transpose(1, 0, 2)

    cyr, cyi = apow(CHUNK * (np.arange(SUBLANES) + 1))
    str_, sti = apow(CHUNK * (1 << np.arange(3)))
    keep = (np.arange(SUBLANES)[None, :] >= (1 << np.arange(3))[:, None]).astype(np.float32)
    step_rows = (jnp.stack([str_, sti], axis=1)[:, :, None] * keep[:, None, :, None, None])
    step_rows = step_rows.reshape(6 * SUBLANES, g, n)
    sc = jnp.concatenate([tile_lanes(cyr), tile_lanes(cyi), tile_lanes(step_rows)],
                         axis=1)
    return (kern, wb.astype(BF16), wc.astype(BF16)), sc


def _s5_core(u2, weights, sc):
    _, rows, _ = u2.shape
    blk = pl.BlockSpec((GROUPS_PER_TILE, rows, FOLD), lambda j: (j, 0, 0))
    w_spec = pl.BlockSpec((GROUPS_PER_TILE, FOLD, FOLD), lambda j: (j, 0, 0))
    return pl.pallas_call(
        _s5_kernel,
        grid=(N_TILES,),
        in_specs=[
            blk,
            pl.BlockSpec((GROUPS_PER_TILE, SSM_GROUP, FOLD), lambda j: (j, 0, 0)),
            w_spec, w_spec,
            pl.BlockSpec((None, 8 * SUBLANES, STATE_W), lambda j: (j, 0, 0)),
        ],
        out_specs=blk,
        out_shape=jax.ShapeDtypeStruct(u2.shape, BF16),
        scratch_shapes=[pltpu.VMEM((rows, 2 * STATE_W), F32),
                        pltpu.VMEM((GROUPS_PER_TILE, FOLD, FOLD), BF16)],
        compiler_params=_cparams(("parallel",)),
        name="s5_scan",
    )(u2, *weights, sc)


def _outproj_kernel(x_ref, a_ref, y_ref, wglu_ref, bglu_ref, wout_ref, g_ref, unperm_ref,
                    h_ref, hn_ref):
    seg = lax.broadcasted_iota(jnp.int32, (CHUNK, LANES), 1) // SSM_GROUP
    for blk in range(N_PERM):
        rows = slice(blk * PERM, (blk + 1) * PERM)
        y_t = [[] for _ in range(CHUNK)]
        for tile in range(N_GROUPS // SEGS):
            for th in range(CHUNK // SEGS):
                src = [y_ref[tile * SEGS + gl, _chunks(blk), _lane(th)].astype(F32)
                       for gl in range(SEGS)]
                for k, unfolded in enumerate(_transpose_segments(src, seg)):
                    y_t[th * SEGS + k].append(unfolded)
        y = jnp.concatenate([jnp.concatenate(parts, axis=1) for parts in y_t], axis=0)
        y = jax.nn.gelu(y)
        gate = jax.nn.sigmoid(
            jnp.dot(y.astype(BF16), wglu_ref[...], preferred_element_type=F32) + bglu_ref[...])
        ssm = (y * gate).astype(BF16)
        attn = jnp.concatenate(
            [jnp.concatenate([a_ref[hh, _chunks(blk), _lane(t)] for t in range(CHUNK)], axis=0)
             for hh in range(N_HEADS)], axis=1)
        mix = jnp.concatenate([attn, ssm], axis=1)
        mix = jnp.dot(unperm_ref[...], mix, preferred_element_type=F32).astype(BF16)
        h = x_ref[rows, :] + jnp.dot(mix, wout_ref[...], preferred_element_type=F32)
        h_ref[rows, :] = h
        ms = jnp.mean(h * h, axis=-1, keepdims=True)
        hn_ref[rows, :] = (h * lax.rsqrt(ms + RMS_EPS) * g_ref[...]).astype(BF16)


def _outproj(x, attn, y, w_glu, b_glu, w_out, g2):
    s = x.shape[0]
    width = MAX_DIL * HEAD_DIM
    unperm = jnp.asarray(_block_permutation(), BF16)
    return pl.pallas_call(
        _outproj_kernel,
        grid=(s // TM,),
        in_specs=[
            pl.BlockSpec((TM, D_MODEL), lambda i: (i, 0)),
            pl.BlockSpec((N_HEADS, TM_CHUNKS, width), lambda i: (0, i, 0)),
            pl.BlockSpec((N_GROUPS, TM_CHUNKS, FOLD), lambda i: (0, i, 0)),
            _resident((SSM_WIDTH, SSM_WIDTH)),
            _resident((1, SSM_WIDTH)),
            _resident((D_MODEL, D_MODEL)),
            _resident((1, D_MODEL)),
            _resident((PERM, PERM)),
        ],
        out_specs=[pl.BlockSpec((TM, D_MODEL), lambda i: (i, 0))] * 2,
        out_shape=[jax.ShapeDtypeStruct((s, D_MODEL), F32),
                   jax.ShapeDtypeStruct((s, D_MODEL), BF16)],
        compiler_params=_cparams(("parallel",)),
        name="outproj",
    )(x, attn, y, w_glu, b_glu, w_out, g2, unperm)


TM_FFN = 1024
TF_FFN = 512
N_F = D_FF // TF_FFN
LOAD_AT = N_F // 2


def _ffn_kernel(n_tiles, hn_ref, h_hbm, wg_ref, wu_ref, wd_ref, g_ref, o_hbm,
                acc_ref, act_a, act_b, sem_in, sem_out):
    k = pl.program_id(0)
    f_prev = lax.rem(k + (N_F - 1), N_F)
    tile = jnp.maximum(k - 1, 0) // N_F
    slot = lax.rem(tile, 2)

    def rows(j):
        return pl.ds(pl.multiple_of(j * TM_FFN, TM_FFN), TM_FFN)

    def copy_in(j, s):
        return pltpu.make_async_copy(h_hbm.at[rows(j), :], acc_ref.at[s], sem_in.at[s])

    def copy_out(j, s):
        return pltpu.make_async_copy(acc_ref.at[s], o_hbm.at[rows(j), :], sem_out.at[s])

    @pl.when(k == 0)
    def _():
        act_b[...] = jnp.zeros_like(act_b)
        acc_ref[0] = jnp.zeros((TM_FFN, D_MODEL), F32)

    @pl.when(f_prev == 0)
    def _():
        copy_in(tile, slot).wait()

    def step(act_prev, act_next):
        part = jnp.dot(act_prev[...], wd_ref[...].astype(BF16), preferred_element_type=F32)
        hn = hn_ref[...]
        gate = jnp.dot(hn, wg_ref[...].astype(BF16), preferred_element_type=F32)
        up = jnp.dot(hn, wu_ref[...].astype(BF16), preferred_element_type=F32)
        act_next[...] = (jax.nn.silu(gate) * up).astype(BF16)
        acc_ref[slot] = acc_ref[slot] + part

    @pl.when(lax.rem(k, 2) == 0)
    def _():
        step(act_b, act_a)

    @pl.when(lax.rem(k, 2) == 1)
    def _():
        step(act_a, act_b)

    @pl.when(k == 0)
    def _():
        copy_in(0, 0).start()

    @pl.when((f_prev == LOAD_AT) & (tile >= 1))
    def _():
        copy_out(tile - 1, 1 - slot).wait()

    @pl.when((f_prev == LOAD_AT) & (tile + 1 < n_tiles))
    def _():
        copy_in(tile + 1, 1 - slot).start()

    @pl.when((f_prev == N_F - 1) & (k > 0))
    def _():
        h = acc_ref[slot]
        ms = jnp.mean(h * h, axis=-1, keepdims=True)
        acc_ref[slot] = h * lax.rsqrt(ms + RMS_EPS) * g_ref[...]
        copy_out(tile, slot).start()

    @pl.when(k == pl.num_programs(0) - 1)
    def _():
        copy_out(tile, slot).wait()


def _ffn(hn, h, w_gate, w_up, w_down, g):
    s = h.shape[0]
    n_i = s // TM_FFN
    return pl.pallas_call(
        functools.partial(_ffn_kernel, n_i),
        grid=(n_i * N_F + 1,),
        in_specs=[
            pl.BlockSpec((TM_FFN, D_MODEL), lambda k: (jnp.minimum(k // N_F, n_i - 1), 0)),
            pl.BlockSpec(memory_space=pl.ANY),
            pl.BlockSpec((D_MODEL, TF_FFN), lambda k: (0, k % N_F)),
            pl.BlockSpec((D_MODEL, TF_FFN), lambda k: (0, k % N_F)),
            pl.BlockSpec((TF_FFN, D_MODEL), lambda k: (jnp.maximum(k - 1, 0) % N_F, 0)),
            _resident((1, D_MODEL)),
        ],
        out_specs=pl.BlockSpec(memory_space=pl.ANY),
        out_shape=jax.ShapeDtypeStruct((s, D_MODEL), F32),
        scratch_shapes=[pltpu.VMEM((2, TM_FFN, D_MODEL), F32),
                        pltpu.VMEM((TM_FFN, TF_FFN), BF16),
                        pltpu.VMEM((TM_FFN, TF_FFN), BF16),
                        pltpu.SemaphoreType.DMA((2,)),
                        pltpu.SemaphoreType.DMA((2,))],
        compiler_params=_cparams(("arbitrary",)),
        name="ffn",
    )(hn, h, w_gate, w_up, w_down, g)


def kernel(x, norm1_g, w_in, a_re, a_im, log_dt, b_re, b_im, c_re, c_im, d_skip, w_glu, b_glu,
           w_out, norm2_g, w_gate, w_up, w_down, final_g):
    b, s, _ = x.shape
    assert b == 1 and s % SUPER == 0 and w_in.shape[0] == 1
    x2 = x[0]
    qkv, u = _inproj(x2, norm1_g[0][None, :], w_in[0].astype(BF16), _rope_tables(s))
    attn = _attention(qkv)
    s5_w, sc = _s5_weights(a_re[0], a_im[0], log_dt[0], b_re[0], b_im[0], c_re[0], c_im[0],
                           d_skip[0])
    y = _s5_core(u, s5_w, sc)
    h, hn = _outproj(x2, attn, y, w_glu[0].astype(BF16), b_glu[0][None, :].astype(F32),
                     w_out[0].astype(BF16), norm2_g[0][None, :])
    out = _ffn(hn, h, w_gate[0], w_up[0], w_down[0], final_g[None, :])
    return out[None]
```

```python
import functools

import numpy as np
import jax
import jax.numpy as jnp
from jax import lax
from jax.experimental import pallas as pl
from jax.experimental.pallas import tpu as pltpu

F32 = jnp.float32
BF16 = jnp.bfloat16
HI = lax.Precision.HIGHEST

D_MODEL = 2048
ATTN_WIDTH = 1024
SSM_WIDTH = 1024
HEAD_DIM = 128
N_HEADS = ATTN_WIDTH // HEAD_DIM
ROT_DIM = HEAD_DIM // 4
ROPE_THETA = 500000.0
BAND = 128
MAX_DIL = 16
SUPER = BAND * MAX_DIL
SSM_GROUP = 16
N_GROUPS = SSM_WIDTH // SSM_GROUP
SSM_STATE = 64
CHUNK = 16
D_FF = 5632
IN_WIDTH = 3 * ATTN_WIDTH + SSM_WIDTH
RMS_EPS = 1e-6
LANES = 128
SUBLANES = 8

TM = 512
TM_CHUNKS = TM // CHUNK

VMEM_LIMIT = 58 * 1024 * 1024


def _cparams(sem):
    return pltpu.CompilerParams(dimension_semantics=sem, vmem_limit_bytes=VMEM_LIMIT)


def _resident(shape):
    zeros = (0,) * len(shape)
    return pl.BlockSpec(shape, lambda *_: zeros, pipeline_mode=pl.Buffered(1))


PERM = CHUNK * CHUNK
N_PERM = TM // PERM


def _piece(blk, t):
    start = blk * PERM + t * CHUNK
    return slice(start, start + CHUNK)


def _chunks(blk):
    return slice(blk * CHUNK, (blk + 1) * CHUNK)


def _tile_positions():
    rho = np.arange(TM)
    r = rho % PERM
    return (rho // PERM) * PERM + CHUNK * (r % CHUNK) + r // CHUNK


def _lane(r):
    return slice(r * LANES, (r + 1) * LANES)


SEGS = LANES // SSM_GROUP


def _transpose_segments(src, seg):
    x = list(src)
    d = SEGS // 2
    while d:
        upper = (seg & d) != 0
        for i in range(SEGS):
            if i & d:
                continue
            a, b = x[i], x[i + d]
            x[i] = jnp.where(upper, pltpu.roll(b, d * SSM_GROUP, 1), a)
            x[i + d] = jnp.where(upper, b, pltpu.roll(a, LANES - d * SSM_GROUP, 1))
        d //= 2
    return x


TN_IN = 512
HEADS_PER_BLK = TN_IN // HEAD_DIM
Q_SCALE = float(HEAD_DIM ** -0.5 * np.log2(np.e))


def _inproj_kernel(x_ref, g_ref, w_ref, rb_ref, ro_ref, rs_ref, perm_ref, qkv_ref, u_ref, hn_ref):
    x = x_ref[...]
    ms = jnp.mean(x * x, axis=-1, keepdims=True)
    hn = (x * lax.rsqrt(ms + RMS_EPS) * g_ref[...]).astype(BF16)
    for blk in range(N_PERM):
        rows = slice(blk * PERM, (blk + 1) * PERM)
        hn_ref[rows, :] = jnp.dot(perm_ref[...], hn[rows, :],
                                  preferred_element_type=F32).astype(BF16)

    cb, sb = rb_ref[0:1, :], rb_ref[1:2, :]
    co, so = ro_ref[0], ro_ref[1]
    cos, sin = cb * co - sb * so, sb * co + cb * so
    sin_hi, sin_lo = sin * rs_ref[0:1, :], sin * rs_ref[1:2, :]
    n_blk, n_qkv = IN_WIDTH // TN_IN, 3 * ATTN_WIDTH // TN_IN
    for j in list(range(n_qkv, n_blk)) + list(range(n_qkv)):
        acc = jnp.dot(hn_ref[...], w_ref[:, j * TN_IN:(j + 1) * TN_IN],
                      preferred_element_type=F32)
        col = j * TN_IN
        if col >= 3 * ATTN_WIDTH:
            seg = lax.broadcasted_iota(jnp.int32, (CHUNK, LANES), 1) // SSM_GROUP
            for lt in range(TN_IN // LANES):
                tile = (col - 3 * ATTN_WIDTH) // LANES + lt
                for blk in range(N_PERM):
                    for th in range(CHUNK // SEGS):
                        src = [acc[_piece(blk, th * SEGS + k), _lane(lt)] for k in range(SEGS)]
                        for gl, folded in enumerate(_transpose_segments(src, seg)):
                            u_ref[tile * SEGS + gl, _chunks(blk), _lane(th)] = folded.astype(BF16)
            continue
        for hh in range(HEADS_PER_BLK):
            r = acc[:, hh * HEAD_DIM:(hh + 1) * HEAD_DIM]
            if col < 2 * ATTN_WIDTH:
                r = (r * cos + pltpu.roll(r, ROT_DIM // 2, 1) * sin_hi
                     + pltpu.roll(r, HEAD_DIM - ROT_DIM // 2, 1) * sin_lo)
            if col < ATTN_WIDTH:
                r = r * Q_SCALE
            r = r.astype(BF16)
            head = j * HEADS_PER_BLK + hh
            for blk in range(N_PERM):
                for t in range(CHUNK):
                    qkv_ref[head, _chunks(blk), _lane(t)] = r[_piece(blk, t), :]


def _rope_tables(s):
    half = ROT_DIM // 2
    freq = np.zeros(HEAD_DIM)
    freq[:ROT_DIM] = np.tile(ROPE_THETA ** (-np.arange(0, ROT_DIM, 2) / ROT_DIM), 2)
    base = (np.arange(s // TM) * TM)[:, None] * freq[None, :]
    off = _tile_positions()[:, None] * freq[None, :]
    signs = np.zeros((2, HEAD_DIM))
    signs[0, half:ROT_DIM] = 1.0
    signs[1, :half] = -1.0
    as_f32 = lambda a: jnp.asarray(a.astype(np.float32))
    return (as_f32(np.stack([np.cos(base), np.sin(base)], axis=1)),
            as_f32(np.stack([np.cos(off), np.sin(off)])), as_f32(signs))


def _block_permutation():
    pos = _tile_positions()[:PERM]
    return (pos[:, None] == np.arange(PERM)[None, :]).astype(np.float32)


def _inproj(x, g, w_bf16, rope):
    s = x.shape[0]
    width = MAX_DIL * HEAD_DIM
    perm = jnp.asarray(_block_permutation(), BF16)
    rope_base, rope_off, rope_signs = rope
    return pl.pallas_call(
        _inproj_kernel,
        grid=(s // TM,),
        in_specs=[
            pl.BlockSpec((TM, D_MODEL), lambda i: (i, 0)),
            _resident((1, D_MODEL)),
            _resident((D_MODEL, IN_WIDTH)),
            pl.BlockSpec((None, 2, HEAD_DIM), lambda i: (i, 0, 0)),
            _resident(rope_off.shape),
            _resident(rope_signs.shape),
            _resident((PERM, PERM)),
        ],
        out_specs=[
            pl.BlockSpec((3 * N_HEADS, TM_CHUNKS, width), lambda i: (0, i, 0)),
            pl.BlockSpec((N_GROUPS, TM_CHUNKS, CHUNK * SSM_GROUP), lambda i: (0, i, 0)),
        ],
        out_shape=[
            jax.ShapeDtypeStruct((3 * N_HEADS, s // MAX_DIL, width), BF16),
            jax.ShapeDtypeStruct((N_GROUPS, s // CHUNK, CHUNK * SSM_GROUP), BF16),
        ],
        scratch_shapes=[pltpu.VMEM((TM, D_MODEL), BF16)],
        compiler_params=_cparams(("parallel",)),
        name="inproj",
    )(x, g, w_bf16, rope_base, rope_off, rope_signs, perm)


def _band_bias(tile, perm_mod, perm_mul):
    rho = np.arange(tile)
    lat = perm_mul * (rho % perm_mod) + rho // perm_mod
    jq = lat[:, None]
    jk = np.concatenate([lat - tile, lat])[None, :]
    dist = jq - jk
    valid = (dist >= 0) & (dist <= BAND)
    normal = np.where(valid, 0.0, -np.inf).astype(np.float32)
    first = np.where(valid & (jk >= 0), 0.0, -np.inf).astype(np.float32)
    return np.stack([normal, first])


def _attn_tile(q, k, v, bias, old):
    n = k.shape[0]
    slabs = [slice(j, j + 2 * LANES) for j in range(0, n, 2 * LANES)]
    s = [lax.dot_general(q, k[sl], (((1,), (1,)), ((), ())), preferred_element_type=F32)
         + bias[:, sl] for sl in slabs]
    mt = functools.reduce(jnp.maximum, [jnp.max(x, axis=-1, keepdims=True) for x in s])
    v1 = jnp.concatenate([v, jnp.ones((n, LANES), BF16)], axis=1)
    if old is None:
        m_new = jnp.broadcast_to(mt, (q.shape[0], LANES))
    else:
        acc_o, m_o, l_o = old
        m_new = jnp.maximum(m_o, mt)
    m2 = jnp.concatenate([m_new, m_new], axis=1)
    pv = functools.reduce(jnp.add, [
        jnp.dot(jnp.exp2(x - m2).astype(BF16), v1[sl], preferred_element_type=F32)
        for x, sl in zip(s, slabs)])
    o, l = pv[:, :HEAD_DIM], pv[:, HEAD_DIM:]
    if old is not None:
        alpha = jnp.exp2(m_o - m_new)
        o = alpha * acc_o + o
        l = alpha * l_o + l
    return o, m_new, l


def _attn_kernel(q_ref, kp_ref, kc_ref, vp_ref, vc_ref, b16_ref, b4_ref, b1_ref,
                 o_ref, acc_ref, m_ref, l_ref):
    first = jnp.where(pl.program_id(1) == 0, 1, 0)

    bias = b16_ref[first]
    for r in range(MAX_DIL):
        k = jnp.concatenate([kp_ref[:, _lane(r)], kc_ref[:, _lane(r)]], axis=0)
        v = jnp.concatenate([vp_ref[:, _lane(r)], vc_ref[:, _lane(r)]], axis=0)
        o, m, l = _attn_tile(q_ref[:, _lane(r)], k, v, bias, None)
        acc_ref[:, _lane(r)] = o
        m_ref[:, _lane(r)] = m
        l_ref[:, _lane(r)] = l

    def run_pattern(dil, rows, final):
        n_c = MAX_DIL // dil
        n_b = BAND // rows
        b_ref = b4_ref if dil == 4 else b1_ref
        for b in range(n_b):
            bias = b_ref[first] if b == 0 else b_ref[0]
            cur = slice(b * rows, (b + 1) * rows)
            prev = slice((b - 1) * rows, b * rows) if b > 0 else slice(BAND - rows, BAND)
            for r in range(dil):
                blocks = [_lane(r + dil * c) for c in range(n_c)]

                def gather(ref, rsl):
                    return jnp.concatenate([ref[rsl, bl] for bl in blocks], axis=0)

                k = jnp.concatenate(
                    [gather(kc_ref if b > 0 else kp_ref, prev), gather(kc_ref, cur)], axis=0)
                v = jnp.concatenate(
                    [gather(vc_ref if b > 0 else vp_ref, prev), gather(vc_ref, cur)], axis=0)
                per = BAND // rows
                for q0 in range(0, n_c, per):
                    qblocks = blocks[q0:q0 + per]

                    def qgather(ref):
                        return jnp.concatenate([ref[cur, bl] for bl in qblocks], axis=0)

                    old = (qgather(acc_ref), qgather(m_ref), qgather(l_ref))
                    o, m, l = _attn_tile(qgather(q_ref), k, v,
                                         bias[q0 * rows:(q0 + per) * rows, :], old)
                    for c, bl in enumerate(qblocks):
                        piece = slice(c * rows, (c + 1) * rows)
                        if final:
                            o_ref[cur, bl] = (o[piece] / l[piece]).astype(o_ref.dtype)
                        else:
                            acc_ref[cur, bl] = o[piece]
                            m_ref[cur, bl] = m[piece]
                            l_ref[cur, bl] = l[piece]

    run_pattern(4, 32, False)
    run_pattern(1, 16, True)


def _attention(qkv):
    rows, width = qkv.shape[1:]
    b16 = jnp.asarray(_band_bias(BAND, BAND, 1))
    b4 = jnp.asarray(_band_bias(BAND, BAND // 4, 4))
    b1 = jnp.asarray(_band_bias(2 * BAND, MAX_DIL, MAX_DIL))
    blk = (None, BAND, width)

    def spec(base, prev):
        if prev:
            return pl.BlockSpec(blk, lambda h, i: (base + h, jnp.maximum(i - 1, 0), 0))
        return pl.BlockSpec(blk, lambda h, i: (base + h, i, 0))

    return pl.pallas_call(
        _attn_kernel,
        grid=(N_HEADS, rows // BAND),
        in_specs=[spec(0, False), spec(N_HEADS, True), spec(N_HEADS, False),
                  spec(2 * N_HEADS, True), spec(2 * N_HEADS, False),
                  _resident(b16.shape), _resident(b4.shape), _resident(b1.shape)],
        out_specs=pl.BlockSpec(blk, lambda h, i: (h, i, 0)),
        out_shape=jax.ShapeDtypeStruct((N_HEADS, rows, width), BF16),
        scratch_shapes=[pltpu.VMEM((BAND, width), F32)] * 3,
        compiler_params=_cparams(("parallel", "parallel")),
        name="dilated_attn",
    )(qkv, qkv, qkv, qkv, qkv, b16, b4, b1)


GROUPS_PER_TILE = SEGS
N_TILES = N_GROUPS // GROUPS_PER_TILE
PAIRS = GROUPS_PER_TILE // 2
STATE_W = GROUPS_PER_TILE * SSM_STATE
FOLD = CHUNK * SSM_GROUP
PAIR_STATE = 2 * SSM_STATE


def _s5_kernel(u_ref, kern_ref, wb_ref, wc_ref, sc_ref, y_ref, h_ref, toe_ref):
    n_rows = u_ref.shape[1]
    lane = lax.broadcasted_iota(jnp.int32, (SSM_GROUP, LANES), 1)
    for g in range(GROUPS_PER_TILE):
        lo, hi = kern_ref[g, :, 0:LANES], kern_ref[g, :, LANES:FOLD]
        for t in range(CHUNK):
            s = (SSM_GROUP * t) % LANES
            rlo = pltpu.roll(lo, s, 1) if s else lo
            if t < SEGS:
                rhi = pltpu.roll(hi, s, 1) if s else hi
                blk = [jnp.where(lane >= s, rlo, 0.0), jnp.where(lane >= s, rhi, rlo)]
            else:
                blk = [jnp.zeros_like(lo), jnp.where(lane >= s, rlo, 0.0)]
            toe_ref[g, t * SSM_GROUP:(t + 1) * SSM_GROUP, :] = jnp.concatenate(
                blk, axis=1).astype(BF16)
    for k in range(PAIRS):
        e = (jnp.dot(u_ref[2 * k], wb_ref[2 * k], preferred_element_type=F32)
             + jnp.dot(u_ref[2 * k + 1], wb_ref[2 * k + 1], preferred_element_type=F32))
        h_ref[:, _lane(k)] = e[:, :PAIR_STATE]
        h_ref[:, STATE_W + k * PAIR_STATE:STATE_W + (k + 1) * PAIR_STATE] = e[:, PAIR_STATE:]

    row = lax.broadcasted_iota(jnp.int32, (SUBLANES, STATE_W), 0)

    def block(b, carry):
        cr, ci = carry
        r0 = pl.multiple_of(b * SUBLANES, SUBLANES)
        xr = h_ref[pl.ds(r0, SUBLANES), 0:STATE_W]
        xi = h_ref[pl.ds(r0, SUBLANES), STATE_W:2 * STATE_W]
        for i in range(3):
            base = 2 * SUBLANES * (i + 1)
            kr = sc_ref[base:base + SUBLANES, :]
            ki = sc_ref[base + SUBLANES:base + 2 * SUBLANES, :]
            sr, si = pltpu.roll(xr, 1 << i, 0), pltpu.roll(xi, 1 << i, 0)
            xr, xi = xr + (kr * sr - ki * si), xi + (kr * si + ki * sr)
        pr, pi = sc_ref[0:8, :], sc_ref[8:16, :]
        hr = xr + (pr * cr - pi * ci)
        hi = xi + (pr * ci + pi * cr)
        h_ref[pl.ds(r0, SUBLANES), 0:STATE_W] = jnp.where(row >= 1, pltpu.roll(hr, 1, 0), cr)
        h_ref[pl.ds(r0, SUBLANES), STATE_W:2 * STATE_W] = jnp.where(
            row >= 1, pltpu.roll(hi, 1, 0), ci)
        return hr[SUBLANES - 1:SUBLANES, :], hi[SUBLANES - 1:SUBLANES, :]

    zero = jnp.zeros((1, STATE_W), F32)
    lax.fori_loop(0, n_rows // SUBLANES, block, (zero, zero), unroll=4)

    for k in range(PAIRS):
        hin = jnp.concatenate(
            [h_ref[:, _lane(k)],
             h_ref[:, STATE_W + k * PAIR_STATE:STATE_W + (k + 1) * PAIR_STATE]],
            axis=1).astype(BF16)
        for s in range(2):
            g = 2 * k + s
            y = (jnp.dot(u_ref[g], toe_ref[g], preferred_element_type=F32)
                 + jnp.dot(hin, wc_ref[g], preferred_element_type=F32))
            y_ref[g] = y.astype(y_ref.dtype)


def _s5_weights(a_re, a_im, log_dt, b_re, b_im, c_re, c_im, d_skip):
    g, n, p = N_GROUPS, SSM_STATE, SSM_GROUP
    nt, gl = N_TILES, GROUPS_PER_TILE
    ar, ai = a_re.astype(F32), a_im.astype(F32)
    dt = jnp.exp(log_dt.astype(F32))[:, None]

    def apow(ks):
        k = jnp.asarray(ks, F32)[:, None, None]
        mag, ph = jnp.exp(ar * dt * k), ai * dt * k
        return mag * jnp.cos(ph), mag * jnp.sin(ph)

    pwr, pwi = apow(np.arange(CHUNK + 1))
    abr, abi = pwr[1], pwi[1]
    nr, ni, den = abr - 1.0, abi, ar * ar + ai * ai
    fr, fi = (nr * ar + ni * ai) / den, (ni * ar - nr * ai) / den
    bre = b_re.astype(F32).transpose(0, 2, 1)
    bim = b_im.astype(F32).transpose(0, 2, 1)
    bbr = fr[:, None, :] * bre - fi[:, None, :] * bim
    bbi = fr[:, None, :] * bim + fi[:, None, :] * bre
    cre, cim = c_re.astype(F32), c_im.astype(F32)

    pr0, pi0 = pwr[:CHUNK].transpose(1, 0, 2)[:, :, None, :], pwi[:CHUNK].transpose(1, 0, 2)[:, :, None, :]
    ca0 = jnp.concatenate([cre[:, None] * pr0 - cim[:, None] * pi0,
                           cre[:, None] * pi0 + cim[:, None] * pr0], axis=-1)
    bb2 = jnp.concatenate([bbr, -bbi], axis=-1)
    kern = jnp.einsum('gqk,gak->gqa', bb2, ca0.reshape(g, CHUNK * p, 2 * n), precision=HI)
    skip = jnp.eye(p, dtype=F32)[None] * d_skip.astype(F32)[:, None, :]
    kern = kern + jnp.pad(skip, ((0, 0), (0, 0), (0, FOLD - p)))

    def dup(z):
        return jnp.concatenate([z, z], axis=-1)

    odd = np.arange(g) % 2 == 1
    own = jnp.asarray((np.arange(LANES) >= n)[None, :] == odd[:, None])
    er = dup(pwr[CHUNK - 1::-1]).transpose(1, 0, 2)[:, :, None, :]
    ei = dup(pwi[CHUNK - 1::-1]).transpose(1, 0, 2)[:, :, None, :]
    br, bi, own4 = dup(bbr)[:, None], dup(bbi)[:, None], own[:, None, None, :]
    wb = jnp.concatenate([jnp.where(own4, er * br - ei * bi, 0.0),
                          jnp.where(own4, er * bi + ei * br, 0.0)], axis=-1)
    wb = wb.reshape(g, FOLD, 2 * LANES)
    cre_f = jnp.tile(cre.transpose(0, 2, 1), (1, 1, CHUNK))
    cim_f = jnp.tile(cim.transpose(0, 2, 1), (1, 1, CHUNK))
    ar1 = jnp.repeat(pwr[1:].transpose(1, 2, 0), p, axis=2)
    ai1 = jnp.repeat(pwi[1:].transpose(1, 2, 0), p, axis=2)
    car, cai = cre_f * ar1 - cim_f * ai1, cre_f * ai1 + cim_f * ar1
    zero = jnp.zeros_like(car)
    wc = jnp.where(jnp.asarray(odd)[:, None, None],
                   jnp.concatenate([zero, car, zero, -cai], axis=1),
                   jnp.concatenate([car, zero, -cai, zero], axis=1))

    def tile_lanes(z):
        return z.reshape(z.shape[0], nt, gl * n).transpose(1, 0, 2)

    cyr, cyi = apow(CHUNK * (np.arange(SUBLANES) + 1))
    str_, sti = apow(CHUNK * (1 << np.arange(3)))
    keep = (np.arange(SUBLANES)[None, :] >= (1 << np.arange(3))[:, None]).astype(np.float32)
    step_rows = (jnp.stack([str_, sti], axis=1)[:, :, None] * keep[:, None, :, None, None])
    step_rows = step_rows.reshape(6 * SUBLANES, g, n)
    sc = jnp.concatenate([tile_lanes(cyr), tile_lanes(cyi), tile_lanes(step_rows)],
                         axis=1)
    return (kern, wb.astype(BF16), wc.astype(BF16)), sc


def _s5_core(u2, weights, sc):
    _, rows, _ = u2.shape
    blk = pl.BlockSpec((GROUPS_PER_TILE, rows, FOLD), lambda j: (j, 0, 0))
    w_spec = pl.BlockSpec((GROUPS_PER_TILE, FOLD, FOLD), lambda j: (j, 0, 0))
    return pl.pallas_call(
        _s5_kernel,
        grid=(N_TILES,),
        in_specs=[
            blk,
            pl.BlockSpec((GROUPS_PER_TILE, SSM_GROUP, FOLD), lambda j: (j, 0, 0)),
            w_spec, w_spec,
            pl.BlockSpec((None, 8 * SUBLANES, STATE_W), lambda j: (j, 0, 0)),
        ],
        out_specs=blk,
        out_shape=jax.ShapeDtypeStruct(u2.shape, BF16),
        scratch_shapes=[pltpu.VMEM((rows, 2 * STATE_W), F32),
                        pltpu.VMEM((GROUPS_PER_TILE, FOLD, FOLD), BF16)],
        compiler_params=_cparams(("parallel",)),
        name="s5_scan",
    )(u2, *weights, sc)


def _outproj_kernel(x_ref, a_ref, y_ref, wglu_ref, bglu_ref, wout_ref, g_ref, unperm_ref,
                    h_ref, hn_ref):
    seg = lax.broadcasted_iota(jnp.int32, (CHUNK, LANES), 1) // SSM_GROUP
    for blk in range(N_PERM):
        rows = slice(blk * PERM, (blk + 1) * PERM)
        attn = jnp.concatenate(
            [jnp.concatenate([a_ref[hh, _chunks(blk), _lane(t)] for t in range(CHUNK)], axis=0)
             for hh in range(N_HEADS)], axis=1)
        attn = jnp.dot(unperm_ref[...], attn, preferred_element_type=F32).astype(BF16)
        h_attn = x_ref[rows, :] + jnp.dot(attn, wout_ref[0:ATTN_WIDTH, :],
                                          preferred_element_type=F32)
        y_t = [[] for _ in range(CHUNK)]
        for tile in range(N_GROUPS // SEGS):
            for th in range(CHUNK // SEGS):
                src = [y_ref[tile * SEGS + gl, _chunks(blk), _lane(th)].astype(F32)
                       for gl in range(SEGS)]
                for k, unfolded in enumerate(_transpose_segments(src, seg)):
                    y_t[th * SEGS + k].append(unfolded)
        y = jnp.concatenate([jnp.concatenate(parts, axis=1) for parts in y_t], axis=0)
        y = jax.nn.gelu(y)
        gate = jax.nn.sigmoid(
            jnp.dot(y.astype(BF16), wglu_ref[...], preferred_element_type=F32) + bglu_ref[...])
        ssm = (y * gate).astype(BF16)
        ssm = jnp.dot(unperm_ref[...], ssm, preferred_element_type=F32).astype(BF16)
        h = h_attn + jnp.dot(ssm, wout_ref[ATTN_WIDTH:D_MODEL, :], preferred_element_type=F32)
        h_ref[rows, :] = h
        ms = jnp.mean(h * h, axis=-1, keepdims=True)
        hn_ref[rows, :] = (h * lax.rsqrt(ms + RMS_EPS) * g_ref[...]).astype(BF16)


def _outproj(x, attn, y, w_glu, b_glu, w_out, g2):
    s = x.shape[0]
    width = MAX_DIL * HEAD_DIM
    unperm = jnp.asarray(_block_permutation(), BF16)
    return pl.pallas_call(
        _outproj_kernel,
        grid=(s // TM,),
        in_specs=[
            pl.BlockSpec((TM, D_MODEL), lambda i: (i, 0)),
            pl.BlockSpec((N_HEADS, TM_CHUNKS, width), lambda i: (0, i, 0)),
            pl.BlockSpec((N_GROUPS, TM_CHUNKS, FOLD), lambda i: (0, i, 0)),
            _resident((SSM_WIDTH, SSM_WIDTH)),
            _resident((1, SSM_WIDTH)),
            _resident((D_MODEL, D_MODEL)),
            _resident((1, D_MODEL)),
            _resident((PERM, PERM)),
        ],
        out_specs=[pl.BlockSpec((TM, D_MODEL), lambda i: (i, 0))] * 2,
        out_shape=[jax.ShapeDtypeStruct((s, D_MODEL), F32),
                   jax.ShapeDtypeStruct((s, D_MODEL), BF16)],
        compiler_params=_cparams(("parallel",)),
        name="outproj",
    )(x, attn, y, w_glu, b_glu, w_out, g2, unperm)


TM_FFN = 1024
TF_FFN = 512
N_F = D_FF // TF_FFN
LOAD_AT = N_F // 2


def _ffn_kernel(n_tiles, hn_ref, h_hbm, wg_ref, wu_ref, wd_ref, g_ref, o_hbm,
                acc_ref, act_a, act_b, sem_in, sem_out):
    k = pl.program_id(0)
    f_prev = lax.rem(k + (N_F - 1), N_F)
    tile = jnp.maximum(k - 1, 0) // N_F
    slot = lax.rem(tile, 2)

    def rows(j):
        return pl.ds(pl.multiple_of(j * TM_FFN, TM_FFN), TM_FFN)

    def copy_in(j, s):
        return pltpu.make_async_copy(h_hbm.at[rows(j), :], acc_ref.at[s], sem_in.at[s])

    def copy_out(j, s):
        return pltpu.make_async_copy(acc_ref.at[s], o_hbm.at[rows(j), :], sem_out.at[s])

    @pl.when(k == 0)
    def _():
        act_b[...] = jnp.zeros_like(act_b)
        acc_ref[0] = jnp.zeros((TM_FFN, D_MODEL), F32)

    @pl.when(f_prev == 0)
    def _():
        copy_in(tile, slot).wait()

    def step(act_prev, act_next):
        part = jnp.dot(act_prev[...], wd_ref[...].astype(BF16), preferred_element_type=F32)
        hn = hn_ref[...]
        gate = jnp.dot(hn, wg_ref[...].astype(BF16), preferred_element_type=F32)
        up = jnp.dot(hn, wu_ref[...].astype(BF16), preferred_element_type=F32)
        act_next[...] = (jax.nn.silu(gate) * up).astype(BF16)
        acc_ref[slot] = acc_ref[slot] + part

    @pl.when(lax.rem(k, 2) == 0)
    def _():
        step(act_b, act_a)

    @pl.when(lax.rem(k, 2) == 1)
    def _():
        step(act_a, act_b)

    @pl.when(k == 0)
    def _():
        copy_in(0, 0).start()

    @pl.when((f_prev == LOAD_AT) & (tile >= 1))
    def _():
        copy_out(tile - 1, 1 - slot).wait()

    @pl.when((f_prev == LOAD_AT) & (tile + 1 < n_tiles))
    def _():
        copy_in(tile + 1, 1 - slot).start()

    @pl.when((f_prev == N_F - 1) & (k > 0))
    def _():
        h = acc_ref[slot]
        ms = jnp.mean(h * h, axis=-1, keepdims=True)
        acc_ref[slot] = h * lax.rsqrt(ms + RMS_EPS) * g_ref[...]
        copy_out(tile, slot).start()

    @pl.when(k == pl.num_programs(0) - 1)
    def _():
        copy_out(tile, slot).wait()


def _ffn(hn, h, w_gate, w_up, w_down, g):
    s = h.shape[0]
    n_i = s // TM_FFN
    return pl.pallas_call(
        functools.partial(_ffn_kernel, n_i),
        grid=(n_i * N_F + 1,),
        in_specs=[
            pl.BlockSpec((TM_FFN, D_MODEL), lambda k: (jnp.minimum(k // N_F, n_i - 1), 0)),
            pl.BlockSpec(memory_space=pl.ANY),
            pl.BlockSpec((D_MODEL, TF_FFN), lambda k: (0, k % N_F)),
            pl.BlockSpec((D_MODEL, TF_FFN), lambda k: (0, k % N_F)),
            pl.BlockSpec((TF_FFN, D_MODEL), lambda k: (jnp.maximum(k - 1, 0) % N_F, 0)),
            _resident((1, D_MODEL)),
        ],
        out_specs=pl.BlockSpec(memory_space=pl.ANY),
        out_shape=jax.ShapeDtypeStruct((s, D_MODEL), F32),
        scratch_shapes=[pltpu.VMEM((2, TM_FFN, D_MODEL), F32),
                        pltpu.VMEM((TM_FFN, TF_FFN), BF16),
                        pltpu.VMEM((TM_FFN, TF_FFN), BF16),
                        pltpu.SemaphoreType.DMA((2,)),
                        pltpu.SemaphoreType.DMA((2,))],
        compiler_params=_cparams(("arbitrary",)),
        name="ffn",
    )(hn, h, w_gate, w_up, w_down, g)


def kernel(x, norm1_g, w_in, a_re, a_im, log_dt, b_re, b_im, c_re, c_im, d_skip, w_glu, b_glu,
           w_out, norm2_g, w_gate, w_up, w_down, final_g):
    b, s, _ = x.shape
    assert b == 1 and s % SUPER == 0 and w_in.shape[0] == 1
    x2 = x[0]
    qkv, u = _inproj(x2, norm1_g[0][None, :], w_in[0].astype(BF16), _rope_tables(s))
    attn = _attention(qkv)
    s5_w, sc = _s5_weights(a_re[0], a_im[0], log_dt[0], b_re[0], b_im[0], c_re[0], c_im[0],
                           d_skip[0])
    y = _s5_core(u, s5_w, sc)
    h, hn = _outproj(x2, attn, y, w_glu[0].astype(BF16), b_glu[0][None, :].astype(F32),
                     w_out[0].astype(BF16), norm2_g[0][None, :])
    out = _ffn(hn, h, w_gate[0], w_up[0], w_down[0], final_g[None, :])
    return out[None]
```

```python
import functools

import numpy as np
import jax
import jax.numpy as jnp
from jax import lax
from jax.experimental import pallas as pl
from jax.experimental.pallas import tpu as pltpu

F32 = jnp.float32
BF16 = jnp.bfloat16
HI = lax.Precision.HIGHEST

D_MODEL = 2048
ATTN_WIDTH = 1024
SSM_WIDTH = 1024
HEAD_DIM = 128
N_HEADS = ATTN_WIDTH // HEAD_DIM
ROT_DIM = HEAD_DIM // 4
ROPE_THETA = 500000.0
BAND = 128
MAX_DIL = 16
SUPER = BAND * MAX_DIL
SSM_GROUP = 16
N_GROUPS = SSM_WIDTH // SSM_GROUP
SSM_STATE = 64
CHUNK = 16
D_FF = 5632
IN_WIDTH = 3 * ATTN_WIDTH + SSM_WIDTH
RMS_EPS = 1e-6
LANES = 128
SUBLANES = 8

TM = 512
TM_CHUNKS = TM // CHUNK

VMEM_LIMIT = 58 * 1024 * 1024


def _cparams(sem):
    return pltpu.CompilerParams(dimension_semantics=sem, vmem_limit_bytes=VMEM_LIMIT)


def _resident(shape):
    zeros = (0,) * len(shape)
    return pl.BlockSpec(shape, lambda *_: zeros, pipeline_mode=pl.Buffered(1))


PERM = CHUNK * CHUNK
N_PERM = TM // PERM


def _piece(blk, t):
    start = blk * PERM + t * CHUNK
    return slice(start, start + CHUNK)


def _chunks(blk):
    return slice(blk * CHUNK, (blk + 1) * CHUNK)


def _tile_positions():
    rho = np.arange(TM)
    r = rho % PERM
    return (rho // PERM) * PERM + CHUNK * (r % CHUNK) + r // CHUNK


def _lane(r):
    return slice(r * LANES, (r + 1) * LANES)


SEGS = LANES // SSM_GROUP


def _transpose_segments(src, seg):
    x = list(src)
    d = SEGS // 2
    while d:
        upper = (seg & d) != 0
        for i in range(SEGS):
            if i & d:
                continue
            a, b = x[i], x[i + d]
            x[i] = jnp.where(upper, pltpu.roll(b, d * SSM_GROUP, 1), a)
            x[i + d] = jnp.where(upper, b, pltpu.roll(a, LANES - d * SSM_GROUP, 1))
        d //= 2
    return x


TN_IN = 512
HEADS_PER_BLK = TN_IN // HEAD_DIM
Q_SCALE = float(HEAD_DIM ** -0.5 * np.log2(np.e))


def _inproj_kernel(x_ref, g_ref, w_ref, rb_ref, ro_ref, rs_ref, perm_ref, qkv_ref, u_ref, hn_ref):
    x = x_ref[...]
    ms = jnp.mean(x * x, axis=-1, keepdims=True)
    hn = (x * lax.rsqrt(ms + RMS_EPS) * g_ref[...]).astype(BF16)
    for blk in range(N_PERM):
        rows = slice(blk * PERM, (blk + 1) * PERM)
        hn_ref[rows, :] = jnp.dot(perm_ref[...], hn[rows, :],
                                  preferred_element_type=F32).astype(BF16)

    cb, sb = rb_ref[0:1, :], rb_ref[1:2, :]
    co, so = ro_ref[0], ro_ref[1]
    cos, sin = cb * co - sb * so, sb * co + cb * so
    sin_hi, sin_lo = sin * rs_ref[0:1, :], sin * rs_ref[1:2, :]
    n_blk, n_qkv = IN_WIDTH // TN_IN, 3 * ATTN_WIDTH // TN_IN
    for j in list(range(n_qkv, n_blk)) + list(range(n_qkv)):
        acc = jnp.dot(hn_ref[...], w_ref[:, j * TN_IN:(j + 1) * TN_IN],
                      preferred_element_type=F32)
        col = j * TN_IN
        if col >= 3 * ATTN_WIDTH:
            seg = lax.broadcasted_iota(jnp.int32, (CHUNK, LANES), 1) // SSM_GROUP
            for lt in range(TN_IN // LANES):
                tile = (col - 3 * ATTN_WIDTH) // LANES + lt
                for blk in range(N_PERM):
                    for th in range(CHUNK // SEGS):
                        src = [acc[_piece(blk, th * SEGS + k), _lane(lt)] for k in range(SEGS)]
                        for gl, folded in enumerate(_transpose_segments(src, seg)):
                            u_ref[tile * SEGS + gl, _chunks(blk), _lane(th)] = folded.astype(BF16)
            continue
        for hh in range(HEADS_PER_BLK):
            r = acc[:, hh * HEAD_DIM:(hh + 1) * HEAD_DIM]
            if col < 2 * ATTN_WIDTH:
                r = (r * cos + pltpu.roll(r, ROT_DIM // 2, 1) * sin_hi
                     + pltpu.roll(r, HEAD_DIM - ROT_DIM // 2, 1) * sin_lo)
            if col < ATTN_WIDTH:
                r = r * Q_SCALE
            r = r.astype(BF16)
            head = j * HEADS_PER_BLK + hh
            for blk in range(N_PERM):
                for t in range(CHUNK):
                    qkv_ref[head, _chunks(blk), _lane(t)] = r[_piece(blk, t), :]


def _rope_tables(s):
    half = ROT_DIM // 2
    freq = np.zeros(HEAD_DIM)
    freq[:ROT_DIM] = np.tile(ROPE_THETA ** (-np.arange(0, ROT_DIM, 2) / ROT_DIM), 2)
    base = (np.arange(s // TM) * TM)[:, None] * freq[None, :]
    off = _tile_positions()[:, None] * freq[None, :]
    signs = np.zeros((2, HEAD_DIM))
    signs[0, half:ROT_DIM] = 1.0
    signs[1, :half] = -1.0
    as_f32 = lambda a: jnp.asarray(a.astype(np.float32))
    return (as_f32(np.stack([np.cos(base), np.sin(base)], axis=1)),
            as_f32(np.stack([np.cos(off), np.sin(off)])), as_f32(signs))


def _block_permutation():
    pos = _tile_positions()[:PERM]
    return (pos[:, None] == np.arange(PERM)[None, :]).astype(np.float32)


def _inproj(x, g, w_bf16, rope):
    s = x.shape[0]
    width = MAX_DIL * HEAD_DIM
    perm = jnp.asarray(_block_permutation(), BF16)
    rope_base, rope_off, rope_signs = rope
    return pl.pallas_call(
        _inproj_kernel,
        grid=(s // TM,),
        in_specs=[
            pl.BlockSpec((TM, D_MODEL), lambda i: (i, 0)),
            _resident((1, D_MODEL)),
            _resident((D_MODEL, IN_WIDTH)),
            pl.BlockSpec((None, 2, HEAD_DIM), lambda i: (i, 0, 0)),
            _resident(rope_off.shape),
            _resident(rope_signs.shape),
            _resident((PERM, PERM)),
        ],
        out_specs=[
            pl.BlockSpec((3 * N_HEADS, TM_CHUNKS, width), lambda i: (0, i, 0)),
            pl.BlockSpec((N_GROUPS, TM_CHUNKS, CHUNK * SSM_GROUP), lambda i: (0, i, 0)),
        ],
        out_shape=[
            jax.ShapeDtypeStruct((3 * N_HEADS, s // MAX_DIL, width), BF16),
            jax.ShapeDtypeStruct((N_GROUPS, s // CHUNK, CHUNK * SSM_GROUP), BF16),
        ],
        scratch_shapes=[pltpu.VMEM((TM, D_MODEL), BF16)],
        compiler_params=_cparams(("parallel",)),
        name="inproj",
    )(x, g, w_bf16, rope_base, rope_off, rope_signs, perm)


def _band_bias(tile, perm_mod, perm_mul):
    rho = np.arange(tile)
    lat = perm_mul * (rho % perm_mod) + rho // perm_mod
    jq = lat[:, None]
    jk = np.concatenate([lat - tile, lat])[None, :]
    dist = jq - jk
    valid = (dist >= 0) & (dist <= BAND)
    normal = np.where(valid, 0.0, -np.inf).astype(np.float32)
    first = np.where(valid & (jk >= 0), 0.0, -np.inf).astype(np.float32)
    return np.stack([normal, first])


def _attn_tile(q, k, v, bias, old):
    n = k.shape[0]
    slabs = [slice(j, j + 2 * LANES) for j in range(0, n, 2 * LANES)]
    s = [lax.dot_general(q, k[sl], (((1,), (1,)), ((), ())), preferred_element_type=F32)
         + bias[:, sl] for sl in slabs]
    mt = functools.reduce(jnp.maximum, [jnp.max(x, axis=-1, keepdims=True) for x in s])
    v1 = jnp.concatenate([v, jnp.ones((n, LANES), BF16)], axis=1)
    if old is None:
        m_new = jnp.broadcast_to(mt, (q.shape[0], LANES))
    else:
        acc_o, m_o, l_o = old
        m_new = jnp.maximum(m_o, mt)
    m2 = jnp.concatenate([m_new, m_new], axis=1)
    pv = functools.reduce(jnp.add, [
        jnp.dot(jnp.exp2(x - m2).astype(BF16), v1[sl], preferred_element_type=F32)
        for x, sl in zip(s, slabs)])
    o, l = pv[:, :HEAD_DIM], pv[:, HEAD_DIM:]
    if old is not None:
        alpha = jnp.exp2(m_o - m_new)
        o = alpha * acc_o + o
        l = alpha * l_o + l
    return o, m_new, l


HEADS_PER_STEP = 2


def _attn_kernel(q_ref, kp_ref, kc_ref, vp_ref, vc_ref, b16_ref, b4_ref, b1_ref,
                 o_ref, acc_ref, m_ref, l_ref):
    first = jnp.where(pl.program_id(1) == 0, 1, 0)
    heads = [_attn_head(q_ref.at[h], kp_ref.at[h], kc_ref.at[h], vp_ref.at[h], vc_ref.at[h],
                        b16_ref, b4_ref, b1_ref, o_ref.at[h], acc_ref.at[h], m_ref.at[h],
                        l_ref.at[h], first) for h in range(HEADS_PER_STEP)]
    for _ in DILATED_STAGES:
        for head in heads:
            next(head)


DILATED_STAGES = (16, 4, 1)


def _attn_head(q_ref, kp_ref, kc_ref, vp_ref, vc_ref, b16_ref, b4_ref, b1_ref,
               o_ref, acc_ref, m_ref, l_ref, first):
    bias = b16_ref[first]
    for r in range(MAX_DIL):
        k = jnp.concatenate([kp_ref[:, _lane(r)], kc_ref[:, _lane(r)]], axis=0)
        v = jnp.concatenate([vp_ref[:, _lane(r)], vc_ref[:, _lane(r)]], axis=0)
        o, m, l = _attn_tile(q_ref[:, _lane(r)], k, v, bias, None)
        acc_ref[:, _lane(r)] = o
        m_ref[:, _lane(r)] = m
        l_ref[:, _lane(r)] = l
    yield

    def run_pattern(dil, rows, final):
        n_c = MAX_DIL // dil
        n_b = BAND // rows
        b_ref = b4_ref if dil == 4 else b1_ref
        for b in range(n_b):
            bias = b_ref[first] if b == 0 else b_ref[0]
            cur = slice(b * rows, (b + 1) * rows)
            prev = slice((b - 1) * rows, b * rows) if b > 0 else slice(BAND - rows, BAND)
            for r in range(dil):
                blocks = [_lane(r + dil * c) for c in range(n_c)]

                def gather(ref, rsl):
                    return jnp.concatenate([ref[rsl, bl] for bl in blocks], axis=0)

                k = jnp.concatenate(
                    [gather(kc_ref if b > 0 else kp_ref, prev), gather(kc_ref, cur)], axis=0)
                v = jnp.concatenate(
                    [gather(vc_ref if b > 0 else vp_ref, prev), gather(vc_ref, cur)], axis=0)
                per = BAND // rows
                for q0 in range(0, n_c, per):
                    qblocks = blocks[q0:q0 + per]

                    def qgather(ref):
                        return jnp.concatenate([ref[cur, bl] for bl in qblocks], axis=0)

                    old = (qgather(acc_ref), qgather(m_ref), qgather(l_ref))
                    o, m, l = _attn_tile(qgather(q_ref), k, v,
                                         bias[q0 * rows:(q0 + per) * rows, :], old)
                    for c, bl in enumerate(qblocks):
                        piece = slice(c * rows, (c + 1) * rows)
                        if final:
                            o_ref[cur, bl] = (o[piece] / l[piece]).astype(o_ref.dtype)
                        else:
                            acc_ref[cur, bl] = o[piece]
                            m_ref[cur, bl] = m[piece]
                            l_ref[cur, bl] = l[piece]

    run_pattern(4, 32, False)
    yield
    run_pattern(1, 16, True)
    yield


def _attention(qkv):
    rows, width = qkv.shape[1:]
    b16 = jnp.asarray(_band_bias(BAND, BAND, 1))
    b4 = jnp.asarray(_band_bias(BAND, BAND // 4, 4))
    b1 = jnp.asarray(_band_bias(2 * BAND, MAX_DIL, MAX_DIL))
    blk = (HEADS_PER_STEP, BAND, width)

    def spec(first_head, prev):
        base = first_head // HEADS_PER_STEP
        if prev:
            return pl.BlockSpec(blk, lambda h, i: (base + h, jnp.maximum(i - 1, 0), 0))
        return pl.BlockSpec(blk, lambda h, i: (base + h, i, 0))

    return pl.pallas_call(
        _attn_kernel,
        grid=(N_HEADS // HEADS_PER_STEP, rows // BAND),
        in_specs=[spec(0, False), spec(N_HEADS, True), spec(N_HEADS, False),
                  spec(2 * N_HEADS, True), spec(2 * N_HEADS, False),
                  _resident(b16.shape), _resident(b4.shape), _resident(b1.shape)],
        out_specs=pl.BlockSpec(blk, lambda h, i: (h, i, 0)),
        out_shape=jax.ShapeDtypeStruct((N_HEADS, rows, width), BF16),
        scratch_shapes=[pltpu.VMEM((HEADS_PER_STEP, BAND, width), F32)] * 3,
        compiler_params=_cparams(("parallel", "parallel")),
        name="dilated_attn",
    )(qkv, qkv, qkv, qkv, qkv, b16, b4, b1)


GROUPS_PER_TILE = SEGS
N_TILES = N_GROUPS // GROUPS_PER_TILE
PAIRS = GROUPS_PER_TILE // 2
STATE_W = GROUPS_PER_TILE * SSM_STATE
FOLD = CHUNK * SSM_GROUP
PAIR_STATE = 2 * SSM_STATE


def _s5_kernel(u_ref, kern_ref, wb_ref, wc_ref, sc_ref, y_ref, h_ref, toe_ref):
    n_rows = u_ref.shape[1]
    lane = lax.broadcasted_iota(jnp.int32, (SSM_GROUP, LANES), 1)
    for g in range(GROUPS_PER_TILE):
        lo, hi = kern_ref[g, :, 0:LANES], kern_ref[g, :, LANES:FOLD]
        for t in range(CHUNK):
            s = (SSM_GROUP * t) % LANES
            rlo = pltpu.roll(lo, s, 1) if s else lo
            if t < SEGS:
                rhi = pltpu.roll(hi, s, 1) if s else hi
                blk = [jnp.where(lane >= s, rlo, 0.0), jnp.where(lane >= s, rhi, rlo)]
            else:
                blk = [jnp.zeros_like(lo), jnp.where(lane >= s, rlo, 0.0)]
            toe_ref[g, t * SSM_GROUP:(t + 1) * SSM_GROUP, :] = jnp.concatenate(
                blk, axis=1).astype(BF16)
    for k in range(PAIRS):
        e = (jnp.dot(u_ref[2 * k], wb_ref[2 * k], preferred_element_type=F32)
             + jnp.dot(u_ref[2 * k + 1], wb_ref[2 * k + 1], preferred_element_type=F32))
        h_ref[:, _lane(k)] = e[:, :PAIR_STATE]
        h_ref[:, STATE_W + k * PAIR_STATE:STATE_W + (k + 1) * PAIR_STATE] = e[:, PAIR_STATE:]

    row = lax.broadcasted_iota(jnp.int32, (SUBLANES, STATE_W), 0)

    def block(b, carry):
        cr, ci = carry
        r0 = pl.multiple_of(b * SUBLANES, SUBLANES)
        xr = h_ref[pl.ds(r0, SUBLANES), 0:STATE_W]
        xi = h_ref[pl.ds(r0, SUBLANES), STATE_W:2 * STATE_W]
        for i in range(3):
            base = 2 * SUBLANES * (i + 1)
            kr = sc_ref[base:base + SUBLANES, :]
            ki = sc_ref[base + SUBLANES:base + 2 * SUBLANES, :]
            sr, si = pltpu.roll(xr, 1 << i, 0), pltpu.roll(xi, 1 << i, 0)
            xr, xi = xr + (kr * sr - ki * si), xi + (kr * si + ki * sr)
        pr, pi = sc_ref[0:8, :], sc_ref[8:16, :]
        hr = xr + (pr * cr - pi * ci)
        hi = xi + (pr * ci + pi * cr)
        h_ref[pl.ds(r0, SUBLANES), 0:STATE_W] = jnp.where(row >= 1, pltpu.roll(hr, 1, 0), cr)
        h_ref[pl.ds(r0, SUBLANES), STATE_W:2 * STATE_W] = jnp.where(
            row >= 1, pltpu.roll(hi, 1, 0), ci)
        return hr[SUBLANES - 1:SUBLANES, :], hi[SUBLANES - 1:SUBLANES, :]

    zero = jnp.zeros((1, STATE_W), F32)
    lax.fori_loop(0, n_rows // SUBLANES, block, (zero, zero), unroll=4)

    for k in range(PAIRS):
        hin = jnp.concatenate(
            [h_ref[:, _lane(k)],
             h_ref[:, STATE_W + k * PAIR_STATE:STATE_W + (k + 1) * PAIR_STATE]],
            axis=1).astype(BF16)
        for s in range(2):
            g = 2 * k + s
            y = (jnp.dot(u_ref[g], toe_ref[g], preferred_element_type=F32)
                 + jnp.dot(hin, wc_ref[g], preferred_element_type=F32))
            y_ref[g] = y.astype(y_ref.dtype)


def _s5_weights(a_re, a_im, log_dt, b_re, b_im, c_re, c_im, d_skip):
    g, n, p = N_GROUPS, SSM_STATE, SSM_GROUP
    nt, gl = N_TILES, GROUPS_PER_TILE
    ar, ai = a_re.astype(F32), a_im.astype(F32)
    dt = jnp.exp(log_dt.astype(F32))[:, None]

    def apow(ks):
        k = jnp.asarray(ks, F32)[:, None, None]
        mag, ph = jnp.exp(ar * dt * k), ai * dt * k
        return mag * jnp.cos(ph), mag * jnp.sin(ph)

    pwr, pwi = apow(np.arange(CHUNK + 1))
    abr, abi = pwr[1], pwi[1]
    nr, ni, den = abr - 1.0, abi, ar * ar + ai * ai
    fr, fi = (nr * ar + ni * ai) / den, (ni * ar - nr * ai) / den
    bre = b_re.astype(F32).transpose(0, 2, 1)
    bim = b_im.astype(F32).transpose(0, 2, 1)
    bbr = fr[:, None, :] * bre - fi[:, None, :] * bim
    bbi = fr[:, None, :] * bim + fi[:, None, :] * bre
    cre, cim = c_re.astype(F32), c_im.astype(F32)

    pr0, pi0 = pwr[:CHUNK].transpose(1, 0, 2)[:, :, None, :], pwi[:CHUNK].transpose(1, 0, 2)[:, :, None, :]
    ca0 = jnp.concatenate([cre[:, None] * pr0 - cim[:, None] * pi0,
                           cre[:, None] * pi0 + cim[:, None] * pr0], axis=-1)
    bb2 = jnp.concatenate([bbr, -bbi], axis=-1)
    kern = jnp.einsum('gqk,gak->gqa', bb2, ca0.reshape(g, CHUNK * p, 2 * n), precision=HI)
    skip = jnp.eye(p, dtype=F32)[None] * d_skip.astype(F32)[:, None, :]
    kern = kern + jnp.pad(skip, ((0, 0), (0, 0), (0, FOLD - p)))

    def dup(z):
        return jnp.concatenate([z, z], axis=-1)

    odd = np.arange(g) % 2 == 1
    own = jnp.asarray((np.arange(LANES) >= n)[None, :] == odd[:, None])
    er = dup(pwr[CHUNK - 1::-1]).transpose(1, 0, 2)[:, :, None, :]
    ei = dup(pwi[CHUNK - 1::-1]).transpose(1, 0, 2)[:, :, None, :]
    br, bi, own4 = dup(bbr)[:, None], dup(bbi)[:, None], own[:, None, None, :]
    wb = jnp.concatenate([jnp.where(own4, er * br - ei * bi, 0.0),
                          jnp.where(own4, er * bi + ei * br, 0.0)], axis=-1)
    wb = wb.reshape(g, FOLD, 2 * LANES)
    cre_f = jnp.tile(cre.transpose(0, 2, 1), (1, 1, CHUNK))
    cim_f = jnp.tile(cim.transpose(0, 2, 1), (1, 1, CHUNK))
    ar1 = jnp.repeat(pwr[1:].transpose(1, 2, 0), p, axis=2)
    ai1 = jnp.repeat(pwi[1:].transpose(1, 2, 0), p, axis=2)
    car, cai = cre_f * ar1 - cim_f * ai1, cre_f * ai1 + cim_f * ar1
    zero = jnp.zeros_like(car)
    wc = jnp.where(jnp.asarray(odd)[:, None, None],
                   jnp.concatenate([zero, car, zero, -cai], axis=1),
                   jnp.concatenate([car, zero, -cai, zero], axis=1))

    def tile_lanes(z):
        return z.reshape(z.shape[0], nt, gl * n).transpose(1, 0, 2)

    cyr, cyi = apow(CHUNK * (np.arange(SUBLANES) + 1))
    str_, sti = apow(CHUNK * (1 << np.arange(3)))
    keep = (np.arange(SUBLANES)[None, :] >= (1 << np.arange(3))[:, None]).astype(np.float32)
    step_rows = (jnp.stack([str_, sti], axis=1)[:, :, None] * keep[:, None, :, None, None])
    step_rows = step_rows.reshape(6 * SUBLANES, g, n)
    sc = jnp.concatenate([tile_lanes(cyr), tile_lanes(cyi), tile_lanes(step_rows)],
                         axis=1)
    return (kern, wb.astype(BF16), wc.astype(BF16)), sc


def _s5_core(u2, weights, sc):
    _, rows, _ = u2.shape
    blk = pl.BlockSpec((GROUPS_PER_TILE, rows, FOLD), lambda j: (j, 0, 0))
    w_spec = pl.BlockSpec((GROUPS_PER_TILE, FOLD, FOLD), lambda j: (j, 0, 0))
    return pl.pallas_call(
        _s5_kernel,
        grid=(N_TILES,),
        in_specs=[
            blk,
            pl.BlockSpec((GROUPS_PER_TILE, SSM_GROUP, FOLD), lambda j: (j, 0, 0)),
            w_spec, w_spec,
            pl.BlockSpec((None, 8 * SUBLANES, STATE_W), lambda j: (j, 0, 0)),
        ],
        out_specs=blk,
        out_shape=jax.ShapeDtypeStruct(u2.shape, BF16),
        scratch_shapes=[pltpu.VMEM((rows, 2 * STATE_W), F32),
                        pltpu.VMEM((GROUPS_PER_TILE, FOLD, FOLD), BF16)],
        compiler_params=_cparams(("parallel",)),
        name="s5_scan",
    )(u2, *weights, sc)


def _outproj_kernel(x_ref, a_ref, y_ref, wglu_ref, bglu_ref, wout_ref, g_ref, unperm_ref,
                    h_ref, hn_ref):
    seg = lax.broadcasted_iota(jnp.int32, (CHUNK, LANES), 1) // SSM_GROUP
    for blk in range(N_PERM):
        rows = slice(blk * PERM, (blk + 1) * PERM)
        attn = jnp.concatenate(
            [jnp.concatenate([a_ref[hh, _chunks(blk), _lane(t)] for t in range(CHUNK)], axis=0)
             for hh in range(N_HEADS)], axis=1)
        attn = jnp.dot(unperm_ref[...], attn, preferred_element_type=F32).astype(BF16)
        h_attn = x_ref[rows, :] + jnp.dot(attn, wout_ref[0:ATTN_WIDTH, :],
                                          preferred_element_type=F32)
        y_t = [[] for _ in range(CHUNK)]
        for tile in range(N_GROUPS // SEGS):
            for th in range(CHUNK // SEGS):
                src = [y_ref[tile * SEGS + gl, _chunks(blk), _lane(th)].astype(F32)
                       for gl in range(SEGS)]
                for k, unfolded in enumerate(_transpose_segments(src, seg)):
                    y_t[th * SEGS + k].append(unfolded)
        y = jnp.concatenate([jnp.concatenate(parts, axis=1) for parts in y_t], axis=0)
        y = jax.nn.gelu(y)
        gate = jax.nn.sigmoid(
            jnp.dot(y.astype(BF16), wglu_ref[...], preferred_element_type=F32) + bglu_ref[...])
        ssm = (y * gate).astype(BF16)
        ssm = jnp.dot(unperm_ref[...], ssm, preferred_element_type=F32).astype(BF16)
        h = h_attn + jnp.dot(ssm, wout_ref[ATTN_WIDTH:D_MODEL, :], preferred_element_type=F32)
        h_ref[rows, :] = h
        ms = jnp.mean(h * h, axis=-1, keepdims=True)
        hn_ref[rows, :] = (h * lax.rsqrt(ms + RMS_EPS) * g_ref[...]).astype(BF16)


def _outproj(x, attn, y, w_glu, b_glu, w_out, g2):
    s = x.shape[0]
    width = MAX_DIL * HEAD_DIM
    unperm = jnp.asarray(_block_permutation(), BF16)
    return pl.pallas_call(
        _outproj_kernel,
        grid=(s // TM,),
        in_specs=[
            pl.BlockSpec((TM, D_MODEL), lambda i: (i, 0)),
            pl.BlockSpec((N_HEADS, TM_CHUNKS, width), lambda i: (0, i, 0)),
            pl.BlockSpec((N_GROUPS, TM_CHUNKS, FOLD), lambda i: (0, i, 0)),
            _resident((SSM_WIDTH, SSM_WIDTH)),
            _resident((1, SSM_WIDTH)),
            _resident((D_MODEL, D_MODEL)),
            _resident((1, D_MODEL)),
            _resident((PERM, PERM)),
        ],
        out_specs=[pl.BlockSpec((TM, D_MODEL), lambda i: (i, 0))] * 2,
        out_shape=[jax.ShapeDtypeStruct((s, D_MODEL), F32),
                   jax.ShapeDtypeStruct((s, D_MODEL), BF16)],
        compiler_params=_cparams(("parallel",)),
        name="outproj",
    )(x, attn, y, w_glu, b_glu, w_out, g2, unperm)


TM_FFN = 1024
TF_FFN = 512
N_F = D_FF // TF_FFN
LOAD_AT = N_F // 2


def _ffn_kernel(n_tiles, hn_ref, h_hbm, wg_ref, wu_ref, wd_ref, g_ref, o_hbm,
                acc_ref, act_a, act_b, sem_in, sem_out):
    k = pl.program_id(0)
    f_prev = lax.rem(k + (N_F - 1), N_F)
    tile = jnp.maximum(k - 1, 0) // N_F
    slot = lax.rem(tile, 2)

    def rows(j):
        return pl.ds(pl.multiple_of(j * TM_FFN, TM_FFN), TM_FFN)

    def copy_in(j, s):
        return pltpu.make_async_copy(h_hbm.at[rows(j), :], acc_ref.at[s], sem_in.at[s])

    def copy_out(j, s):
        return pltpu.make_async_copy(acc_ref.at[s], o_hbm.at[rows(j), :], sem_out.at[s])

    @pl.when(k == 0)
    def _():
        act_b[...] = jnp.zeros_like(act_b)
        acc_ref[0] = jnp.zeros((TM_FFN, D_MODEL), F32)

    @pl.when(f_prev == 0)
    def _():
        copy_in(tile, slot).wait()

    def step(act_prev, act_next):
        part = jnp.dot(act_prev[...], wd_ref[...].astype(BF16), preferred_element_type=F32)
        hn = hn_ref[...]
        gate = jnp.dot(hn, wg_ref[...].astype(BF16), preferred_element_type=F32)
        up = jnp.dot(hn, wu_ref[...].astype(BF16), preferred_element_type=F32)
        act_next[...] = (jax.nn.silu(gate) * up).astype(BF16)
        acc_ref[slot] = acc_ref[slot] + part

    @pl.when(lax.rem(k, 2) == 0)
    def _():
        step(act_b, act_a)

    @pl.when(lax.rem(k, 2) == 1)
    def _():
        step(act_a, act_b)

    @pl.when(k == 0)
    def _():
        copy_in(0, 0).start()

    @pl.when((f_prev == LOAD_AT) & (tile >= 1))
    def _():
        copy_out(tile - 1, 1 - slot).wait()

    @pl.when((f_prev == LOAD_AT) & (tile + 1 < n_tiles))
    def _():
        copy_in(tile + 1, 1 - slot).start()

    @pl.when((f_prev == N_F - 1) & (k > 0))
    def _():
        h = acc_ref[slot]
        ms = jnp.mean(h * h, axis=-1, keepdims=True)
        acc_ref[slot] = h * lax.rsqrt(ms + RMS_EPS) * g_ref[...]
        copy_out(tile, slot).start()

    @pl.when(k == pl.num_programs(0) - 1)
    def _():
        copy_out(tile, slot).wait()


def _ffn(hn, h, w_gate, w_up, w_down, g):
    s = h.shape[0]
    n_i = s // TM_FFN
    return pl.pallas_call(
        functools.partial(_ffn_kernel, n_i),
        grid=(n_i * N_F + 1,),
        in_specs=[
            pl.BlockSpec((TM_FFN, D_MODEL), lambda k: (jnp.minimum(k // N_F, n_i - 1), 0)),
            pl.BlockSpec(memory_space=pl.ANY),
            pl.BlockSpec((D_MODEL, TF_FFN), lambda k: (0, k % N_F)),
            pl.BlockSpec((D_MODEL, TF_FFN), lambda k: (0, k % N_F)),
            pl.BlockSpec((TF_FFN, D_MODEL), lambda k: (jnp.maximum(k - 1, 0) % N_F, 0)),
            _resident((1, D_MODEL)),
        ],
        out_specs=pl.BlockSpec(memory_space=pl.ANY),
        out_shape=jax.ShapeDtypeStruct((s, D_MODEL), F32),
        scratch_shapes=[pltpu.VMEM((2, TM_FFN, D_MODEL), F32),
                        pltpu.VMEM((TM_FFN, TF_FFN), BF16),
                        pltpu.VMEM((TM_FFN, TF_FFN), BF16),
                        pltpu.SemaphoreType.DMA((2,)),
                        pltpu.SemaphoreType.DMA((2,))],
        compiler_params=_cparams(("arbitrary",)),
        name="ffn",
    )(hn, h, w_gate, w_up, w_down, g)


def kernel(x, norm1_g, w_in, a_re, a_im, log_dt, b_re, b_im, c_re, c_im, d_skip, w_glu, b_glu,
           w_out, norm2_g, w_gate, w_up, w_down, final_g):
    b, s, _ = x.shape
    assert b == 1 and s % SUPER == 0 and w_in.shape[0] == 1
    x2 = x[0]
    qkv, u = _inproj(x2, norm1_g[0][None, :], w_in[0].astype(BF16), _rope_tables(s))
    attn = _attention(qkv)
    s5_w, sc = _s5_weights(a_re[0], a_im[0], log_dt[0], b_re[0], b_im[0], c_re[0], c_im[0],
                           d_skip[0])
    y = _s5_core(u, s5_w, sc)
    h, hn = _outproj(x2, attn, y, w_glu[0].astype(BF16), b_glu[0][None, :].astype(F32),
                     w_out[0].astype(BF16), norm2_g[0][None, :])
    out = _ffn(hn, h, w_gate[0], w_up[0], w_down[0], final_g[None, :])
    return out[None]
```
